```python
import jax, jax.numpy as jnp
from jax import lax
import numpy as np

D_MODEL = 1024
BATCH = 4
SEQ = 8192
DEPTH = 4

MEM_LEN = 256

ATTN_HEADS = 8
ATTN_WIDTH = D_MODEL // 2
ATTN_HEAD_DIM = ATTN_WIDTH // ATTN_HEADS
POOL_GROUPS = 4
POOL_WIDTH = D_MODEL // 4
POOL_GROUP_DIM = POOL_WIDTH // POOL_GROUPS
POOL_WINDOWS = (2, 4, 8, 16)
CONV_GROUPS = 4
CONV_WIDTH = D_MODEL // 4
CONV_K = 3
MIX_WIDTH = ATTN_WIDTH + POOL_WIDTH + CONV_WIDTH
IN_SPLIT_WIDTHS = (ATTN_WIDTH, ATTN_WIDTH, ATTN_WIDTH, POOL_WIDTH, CONV_WIDTH, CONV_WIDTH, CONV_WIDTH)
IN_PROJ_WIDTH = sum(IN_SPLIT_WIDTHS)

DILATED_BRANCHES = ((128, 1), (512, 4), (2048, 16))
MAX_WINDOW = 2048
Q_BLOCK = 128
NEG_INF = -1e30

REL_BUCKETS = 32
REL_MAX_DIST = 128

XATTN_HEADS = 4
XATTN_HEAD_DIM = 128
XATTN_WIDTH = XATTN_HEADS * XATTN_HEAD_DIM

D_FF_DENSE = 2816
N_EXPERTS = 8
TOP_K = 2
D_FF_EXPERT = 3584
N_DENSE = (DEPTH + 1) // 2
N_MOE = DEPTH // 2

EPS = 1e-6

kernel_name = "hybrid_dilated_pool_conv_moe_trunk"


def rmsnorm(x, g):
    xf = x.astype(jnp.float32)
    xf = xf * lax.rsqrt(jnp.mean(xf * xf, axis=-1, keepdims=True) + EPS)
    return xf.astype(x.dtype) * g


def _t5_causal_bucket(dist):
    max_exact = REL_BUCKETS // 2
    d = np.maximum(dist, 1).astype(np.float32)
    large = max_exact + (np.log(d / max_exact) / np.log(REL_MAX_DIST / max_exact)
                         * (REL_BUCKETS - max_exact)).astype(np.int32)
    large = np.minimum(large, REL_BUCKETS - 1)
    return np.where(dist < max_exact, dist, large).astype(np.int32)


def dilated_attention(q, k, v, rel_bias):
    B, S, H, Dh = q.shape
    scale = Dh ** -0.5
    span = Q_BLOCK + MAX_WINDOW
    k_pad = jnp.pad(k, ((0, 0), (MAX_WINDOW, 0), (0, 0), (0, 0)))
    v_pad = jnp.pad(v, ((0, 0), (MAX_WINDOW, 0), (0, 0), (0, 0)))
    q_loc = np.arange(Q_BLOCK)
    branches = []
    for window, dil in DILATED_BRANCHES:
        offs = dil * np.arange(window // dil + 1)
        idx = q_loc[:, None] + MAX_WINDOW - offs[None, :]
        rel_pos = q_loc[:, None] - offs[None, :]
        bias = rel_bias[_t5_causal_bucket(offs)].T.astype(jnp.float32)
        branches.append((idx, rel_pos, bias))

    def block(i):
        t0 = i * Q_BLOCK
        qb = lax.dynamic_slice_in_dim(q, t0, Q_BLOCK, axis=1)
        kb = lax.dynamic_slice_in_dim(k_pad, t0, span, axis=1)
        vb = lax.dynamic_slice_in_dim(v_pad, t0, span, axis=1)
        outs, lses = [], []
        for idx, rel_pos, bias in branches:
            kg = kb[:, idx]
            vg = vb[:, idx]
            logits = jnp.einsum('bqhd,bqjhd->bhqj', qb, kg).astype(jnp.float32) * scale
            logits = logits + bias[None, :, None, :]
            valid = (t0 + rel_pos) >= 0
            logits = jnp.where(valid[None, None], logits, NEG_INF)
            lse = jax.nn.logsumexp(logits, axis=-1)
            p = jnp.exp(logits - lse[..., None]).astype(vg.dtype)
            outs.append(jnp.einsum('bhqj,bqjhd->bqhd', p, vg))
            lses.append(lse)
        w = jax.nn.softmax(jnp.stack(lses, axis=0), axis=0)
        w = jnp.transpose(w, (0, 1, 3, 2))[..., None].astype(q.dtype)
        return jnp.sum(w * jnp.stack(outs, axis=0), axis=0)

    out = lax.map(block, jnp.arange(S // Q_BLOCK))
    return jnp.transpose(out, (1, 0, 2, 3, 4)).reshape(B, S, H * Dh)


def multiscale_pool(u, pool_w, pool_scale):
    B, S, _ = u.shape
    uf = u.astype(jnp.float32).reshape(B, S, POOL_GROUPS, POOL_GROUP_DIM)
    csum = jnp.cumsum(uf, axis=1)
    count_base = jnp.arange(1, S + 1, dtype=jnp.float32)
    outs = []
    for g, w in enumerate(POOL_WINDOWS):
        c = csum[:, :, g]
        lag = jnp.pad(c, ((0, 0), (w, 0), (0, 0)))[:, :S]
        mean = (c - lag) / jnp.minimum(count_base, float(w))[None, :, None]
        outs.append(mean - uf[:, :, g])
    pooled = jnp.stack(outs, axis=2).astype(u.dtype)
    mixed = jnp.einsum('bsgc,gcd->bsgd', pooled, pool_w).reshape(B, S, POOL_WIDTH)
    return mixed * pool_scale


def gated_short_conv(b_gate, c_gate, h, conv_w):
    u = c_gate * h
    y = conv_w[0] * u
    for tap in range(1, CONV_K):
        y = y + conv_w[tap] * jnp.pad(u, ((0, 0), (tap, 0), (0, 0)))[:, :-tap]
    return b_gate * y


def hybrid_mixer(xn, w_in, conv_w, pool_w, pool_scale, w_out, rel_bias):
    B, S, _ = xn.shape
    proj = xn @ w_in
    split_pts = np.cumsum(IN_SPLIT_WIDTHS)[:-1].tolist()
    q, k, v, u_pool, b_gate, c_gate, h_conv = jnp.split(proj, split_pts, axis=-1)
    heads = lambda t: t.reshape(B, S, ATTN_HEADS, ATTN_HEAD_DIM)
    y_attn = dilated_attention(heads(q), heads(k), heads(v), rel_bias)
    y_pool = multiscale_pool(u_pool, pool_w, pool_scale)
    y_conv = gated_short_conv(b_gate, c_gate, h_conv, conv_w)
    y = jnp.concatenate([y_attn, y_pool, y_conv], axis=-1)
    return y @ w_out


def memory_cross_attention(xn, mem_n, wq, wkv, wo):
    B, S, _ = xn.shape
    M = mem_n.shape[1]
    q = (xn @ wq).reshape(B, S, XATTN_HEADS, XATTN_HEAD_DIM)
    k, v = jnp.split(mem_n @ wkv, 2, axis=-1)
    k = k.reshape(B, M, XATTN_HEADS, XATTN_HEAD_DIM)
    v = v.reshape(B, M, XATTN_HEADS, XATTN_HEAD_DIM)
    logits = jnp.einsum('bshd,bmhd->bhsm', q, k).astype(jnp.float32) * (XATTN_HEAD_DIM ** -0.5)
    p = jax.nn.softmax(logits, axis=-1).astype(v.dtype)
    o = jnp.einsum('bhsm,bmhd->bshd', p, v).reshape(B, S, XATTN_WIDTH)
    return o @ wo


def swiglu(x, w13, w2):
    g, u = jnp.split(x @ w13, 2, axis=-1)
    return (jax.nn.silu(g) * u) @ w2


def moe_swiglu(xn, router_w, w13, w2):
    B, S, D = xn.shape
    xt = xn.reshape(B * S, D)
    logits = (xt @ router_w).astype(jnp.float32)
    top_vals, top_idx = lax.top_k(logits, TOP_K)
    top_p = jax.nn.softmax(top_vals, axis=-1)
    gates = jnp.sum(jax.nn.one_hot(top_idx, N_EXPERTS, dtype=jnp.float32) * top_p[..., None],
                    axis=1).astype(xn.dtype)
    out = jnp.zeros_like(xt)
    for e in range(N_EXPERTS):
        out = out + gates[:, e:e + 1] * swiglu(xt, w13[e], w2[e])
    return out.reshape(B, S, D)


def setup_inputs(seed: int = 0) -> dict:
    key = jax.random.key(seed)
    ks = jax.random.split(key, 24)
    f32 = jnp.float32

    def w(k, shape, fan_in):
        return jax.random.normal(k, shape, f32) * (fan_in ** -0.5)

    def gain(k, shape):
        return 1.0 + 0.02 * jax.random.normal(k, shape, f32)

    return {
        "x": jax.random.normal(ks[0], (BATCH, SEQ, D_MODEL), f32),
        "mem": jax.random.normal(ks[1], (BATCH, MEM_LEN, D_MODEL), f32),
        "rel_bias": 0.5 * jax.random.normal(ks[2], (REL_BUCKETS, ATTN_HEADS), f32),
        "mem_norm_g": gain(ks[3], (D_MODEL,)),
        "mix_norm_g": gain(ks[4], (DEPTH, D_MODEL)),
        "w_in": w(ks[5], (DEPTH, D_MODEL, IN_PROJ_WIDTH), D_MODEL),
        "conv_w": w(ks[6], (DEPTH, CONV_K, CONV_WIDTH), CONV_K),
        "pool_w": w(ks[7], (DEPTH, POOL_GROUPS, POOL_GROUP_DIM, POOL_GROUP_DIM), POOL_GROUP_DIM),
        "pool_scale": gain(ks[8], (DEPTH, POOL_WIDTH)) + 0.1 * jax.random.normal(ks[9], (DEPTH, POOL_WIDTH), f32),
        "w_out": w(ks[10], (DEPTH, MIX_WIDTH, D_MODEL), MIX_WIDTH),
        "xattn_norm_g": gain(ks[11], (DEPTH, D_MODEL)),
        "wq_x": w(ks[12], (DEPTH, D_MODEL, XATTN_WIDTH), D_MODEL),
        "wkv_x": w(ks[13], (DEPTH, D_MODEL, 2 * XATTN_WIDTH), D_MODEL),
        "wo_x": w(ks[14], (DEPTH, XATTN_WIDTH, D_MODEL), XATTN_WIDTH),
        "ffn_norm_g": gain(ks[15], (DEPTH, D_MODEL)),
        "w13_dense": w(ks[16], (N_DENSE, D_MODEL, 2 * D_FF_DENSE), D_MODEL),
        "w2_dense": w(ks[17], (N_DENSE, D_FF_DENSE, D_MODEL), D_FF_DENSE),
        "router_w": w(ks[18], (N_MOE, D_MODEL, N_EXPERTS), D_MODEL),
        "w13_moe": w(ks[19], (N_MOE, N_EXPERTS, D_MODEL, 2 * D_FF_EXPERT), D_MODEL),
        "w2_moe": w(ks[20], (N_MOE, N_EXPERTS, D_FF_EXPERT, D_MODEL), D_FF_EXPERT),
        "final_norm_g": gain(ks[21], (D_MODEL,)),
    }


def reference(x, mem, rel_bias, mem_norm_g, mix_norm_g, w_in, conv_w, pool_w, pool_scale,
              w_out, xattn_norm_g, wq_x, wkv_x, wo_x, ffn_norm_g, w13_dense, w2_dense,
              router_w, w13_moe, w2_moe, final_norm_g):
    mem_n = rmsnorm(mem, mem_norm_g)
    for layer in range(DEPTH):
        x = x + hybrid_mixer(rmsnorm(x, mix_norm_g[layer]), w_in[layer], conv_w[layer],
                             pool_w[layer], pool_scale[layer], w_out[layer], rel_bias)
        x = x + memory_cross_attention(rmsnorm(x, xattn_norm_g[layer]), mem_n,
                                       wq_x[layer], wkv_x[layer], wo_x[layer])
        hn = rmsnorm(x, ffn_norm_g[layer])
        if layer % 2 == 0:
            x = x + swiglu(hn, w13_dense[layer // 2], w2_dense[layer // 2])
        else:
            x = x + moe_swiglu(hn, router_w[layer // 2], w13_moe[layer // 2], w2_moe[layer // 2])
    return rmsnorm(x, final_norm_g)
```

```python
from functools import partial

import numpy as np
import jax
import jax.numpy as jnp
from jax import lax
from jax.experimental import pallas as pl
from jax.experimental.pallas import tpu as pltpu

F32 = jnp.float32
BF16 = jnp.bfloat16

D_MODEL = 1024
ATTN_HEADS = 8
ATTN_HEAD_DIM = 64
ATTN_WIDTH = ATTN_HEADS * ATTN_HEAD_DIM
POOL_WIDTH = 256
POOL_GROUP_DIM = 64
POOL_WINDOWS = (2, 4, 8, 16)
POOL_HALO = 16
CONV_WIDTH = 256
REST_WIDTH = POOL_WIDTH + 3 * CONV_WIDTH
DILATIONS = (1, 4, 16)
WINDOW_STEPS = 128
Q_BLOCK = 128
NEG_INF = -1e30
REL_BUCKETS = 32
REL_MAX_DIST = 128
XATTN_HEADS = 4
XATTN_HEAD_DIM = 128
XATTN_WIDTH = XATTN_HEADS * XATTN_HEAD_DIM
N_EXPERTS = 8
TOP_K = 2
EPS = 1e-6

LANES = 128
SLAB_ROWS = D_MODEL // LANES
ROW_TILE = 512
EXPERT_ROW_TILE = 512
VMEM_LIMIT = 56 * 1024 * 1024


def _cparams(*sem):
    return pltpu.CompilerParams(dimension_semantics=sem, vmem_limit_bytes=VMEM_LIMIT)


def _rms(x, g):
    return x * lax.rsqrt(jnp.mean(x * x, axis=-1, keepdims=True) + EPS) * g


def _norm_proj_kernel(x_ref, g_ref, w_ref, o_ref):
    xn = _rms(x_ref[...], g_ref[...]).astype(BF16)
    o_ref[...] = jnp.dot(xn, w_ref[...], preferred_element_type=F32).astype(o_ref.dtype)


def _norm_proj(x, g, w, out_dtype, tm):
    T, D = x.shape
    N = w.shape[1]
    return pl.pallas_call(
        _norm_proj_kernel,
        grid=(T // tm,),
        in_specs=[pl.BlockSpec((tm, D), lambda i: (i, 0)),
                  pl.BlockSpec((1, D), lambda i: (0, 0)),
                  pl.BlockSpec((D, N), lambda i: (0, 0))],
        out_specs=pl.BlockSpec((tm, N), lambda i: (i, 0)),
        out_shape=jax.ShapeDtypeStruct((T, N), out_dtype),
        compiler_params=_cparams("parallel"),
    )(x, g, w)


def _in_proj_kernel(x_ref, g_ref, w_ref, qkv_ref, rest_ref):
    xn = _rms(x_ref[...], g_ref[...]).astype(BF16)
    nq = qkv_ref.shape[1]
    qkv_ref[...] = jnp.dot(xn, w_ref[:, :nq], preferred_element_type=F32).astype(BF16)
    rest_ref[...] = jnp.dot(xn, w_ref[:, nq:], preferred_element_type=F32)


def _in_proj(x, g, w, tm):
    T, D = x.shape
    N = w.shape[1]
    nq = 3 * ATTN_WIDTH
    return pl.pallas_call(
        _in_proj_kernel,
        grid=(T // tm,),
        in_specs=[pl.BlockSpec((tm, D), lambda i: (i, 0)),
                  pl.BlockSpec((1, D), lambda i: (0, 0)),
                  pl.BlockSpec((D, N), lambda i: (0, 0))],
        out_specs=[pl.BlockSpec((tm, nq), lambda i: (i, 0)),
                   pl.BlockSpec((tm, N - nq), lambda i: (i, 0))],
        out_shape=[jax.ShapeDtypeStruct((T, nq), BF16),
                   jax.ShapeDtypeStruct((T, N - nq), F32)],
        compiler_params=_cparams("parallel"),
    )(x, g, w)


def _t5_causal_bucket(dist):
    max_exact = REL_BUCKETS // 2
    d = np.maximum(dist, 1).astype(np.float32)
    large = max_exact + (np.log(d / max_exact) / np.log(REL_MAX_DIST / max_exact)
                         * (REL_BUCKETS - max_exact)).astype(np.int32)
    large = np.minimum(large, REL_BUCKETS - 1)
    return np.where(dist < max_exact, dist, large).astype(np.int32)


def _band_bias(rel_bias, dil):
    row = np.arange(Q_BLOCK)[:, None]
    col = np.arange(2 * Q_BLOCK)[None, :]
    steps = Q_BLOCK + row - col
    valid = (steps >= 0) & (steps <= WINDOW_STEPS)
    bucket = _t5_causal_bucket(dil * np.clip(steps, 0, WINDOW_STEPS))
    b = jnp.transpose(rel_bias[bucket], (2, 0, 1)).astype(F32)
    return jnp.where(jnp.asarray(valid)[None], b, NEG_INF)


def _dilated_attn_kernel(q_ref, kp_ref, kc_ref, vp_ref, vc_ref, bias_ref, o_ref, lse_ref):
    i = pl.program_id(2)
    q = q_ref[0]
    k = jnp.concatenate([kp_ref[0], kc_ref[0]], axis=0)
    v = jnp.concatenate([vp_ref[0], vc_ref[0]], axis=0)
    col = lax.broadcasted_iota(jnp.int32, (Q_BLOCK, 2 * Q_BLOCK), 1)
    has_key = (col >= Q_BLOCK) | (i > 0)
    outs, lses = [], []
    for h in range(ATTN_HEADS):
        sl = slice(h * ATTN_HEAD_DIM, (h + 1) * ATTN_HEAD_DIM)
        s = lax.dot_general(q[:, sl], k[:, sl], (((1,), (1,)), ((), ())),
                            preferred_element_type=F32)
        s = jnp.where(has_key, s + bias_ref[h], NEG_INF)
        m = jnp.max(s, axis=-1, keepdims=True)
        p = jnp.exp(s - m)
        l = jnp.sum(p, axis=-1, keepdims=True)
        o = jnp.dot(p.astype(BF16), v[:, sl], preferred_element_type=F32)
        outs.append(o / l)
        lses.append(m + jnp.log(l))
    o_ref[0] = jnp.concatenate(outs, axis=1).astype(o_ref.dtype)
    pad = jnp.zeros((Q_BLOCK, LANES - ATTN_HEADS), F32)
    lse_ref[0] = jnp.concatenate(lses + [pad], axis=1)


def _dilated_attn_branch(qkv, bias, B, S, dil):
    W = ATTN_WIDTH
    Sd = S // dil
    nb = Sd // Q_BLOCK
    qkv_v = qkv.reshape(B, Sd, dil * 3 * W)
    blk = (1, Q_BLOCK, W)
    prev = lambda i: jnp.maximum(i - 1, 0)
    out, lse = pl.pallas_call(
        _dilated_attn_kernel,
        grid=(B, dil, nb),
        in_specs=[pl.BlockSpec(blk, lambda b, r, i: (b, i, 3 * r)),
                  pl.BlockSpec(blk, lambda b, r, i: (b, prev(i), 3 * r + 1)),
                  pl.BlockSpec(blk, lambda b, r, i: (b, i, 3 * r + 1)),
                  pl.BlockSpec(blk, lambda b, r, i: (b, prev(i), 3 * r + 2)),
                  pl.BlockSpec(blk, lambda b, r, i: (b, i, 3 * r + 2)),
                  pl.BlockSpec((ATTN_HEADS, Q_BLOCK, 2 * Q_BLOCK), lambda b, r, i: (0, 0, 0))],
        out_specs=[pl.BlockSpec(blk, lambda b, r, i: (b, i, r)),
                   pl.BlockSpec((1, Q_BLOCK, LANES), lambda b, r, i: (b, i, r))],
        out_shape=[jax.ShapeDtypeStruct((B, Sd, dil * W), BF16),
                   jax.ShapeDtypeStruct((B, Sd, dil * LANES), F32)],
        compiler_params=_cparams("parallel", "parallel", "parallel"),
    )(qkv_v, qkv_v, qkv_v, qkv_v, qkv_v, bias)
    return out.reshape(B * S, W), lse.reshape(B * S, LANES)


def _mixer_out_kernel(x_ref, o1_ref, o2_ref, o3_ref, l1_ref, l2_ref, l3_ref, rest_ref, halo_ref,
                      convw_ref, poolw_ref, pscale_ref, pwin_ref, wout_ref, out_ref,
                      *, tm, tiles_per_seq):
    it = pl.program_id(0) % tiles_per_seq
    l1, l2, l3 = l1_ref[...], l2_ref[...], l3_ref[...]
    lm = jnp.maximum(jnp.maximum(l1, l2), l3)
    e1, e2, e3 = jnp.exp(l1 - lm), jnp.exp(l2 - lm), jnp.exp(l3 - lm)
    inv = 1.0 / (e1 + e2 + e3)
    ws = (e1 * inv, e2 * inv, e3 * inv)
    os_ = (o1_ref[...].astype(F32), o2_ref[...].astype(F32), o3_ref[...].astype(F32))
    heads = []
    for h in range(ATTN_HEADS):
        sl = slice(h * ATTN_HEAD_DIM, (h + 1) * ATTN_HEAD_DIM)
        heads.append(sum(w[:, h:h + 1] * o[:, sl] for w, o in zip(ws, os_)))
    y_attn = jnp.concatenate(heads, axis=1).astype(BF16)

    rest = rest_ref[...]
    halo = jnp.where(it == 0, 0.0, halo_ref[...])
    ext = jnp.concatenate([halo, rest], axis=0)

    u = ext[:, :POOL_WIDTH]
    s2 = u + pltpu.roll(u, 1, 0)
    s4 = s2 + pltpu.roll(s2, 2, 0)
    s8 = s4 + pltpu.roll(s4, 4, 0)
    s16 = s8 + pltpu.roll(s8, 8, 0)
    col = lax.broadcasted_iota(jnp.int32, u.shape, 1)
    g = POOL_GROUP_DIM
    wsum = jnp.where(col < g, s2, jnp.where(col < 2 * g, s4, jnp.where(col < 3 * g, s8, s16)))
    pos = it * tm + lax.broadcasted_iota(jnp.int32, (tm, POOL_WIDTH), 0)
    count = jnp.minimum((pos + 1).astype(F32), pwin_ref[...])
    pooled = wsum[POOL_HALO:] / count - u[POOL_HALO:]
    y_pool = jnp.dot(pooled.astype(BF16), poolw_ref[...], preferred_element_type=F32)
    y_pool = (y_pool * pscale_ref[...]).astype(BF16)

    b_gate = rest[:, POOL_WIDTH:POOL_WIDTH + CONV_WIDTH]
    cu = ext[:, POOL_WIDTH + CONV_WIDTH:POOL_WIDTH + 2 * CONV_WIDTH] * ext[:, POOL_WIDTH + 2 * CONV_WIDTH:]
    y = (convw_ref[0:1, :] * cu + convw_ref[1:2, :] * pltpu.roll(cu, 1, 0)
         + convw_ref[2:3, :] * pltpu.roll(cu, 2, 0))
    y_conv = (b_gate * y[POOL_HALO:]).astype(BF16)

    a0, a1 = ATTN_WIDTH, ATTN_WIDTH + POOL_WIDTH
    acc = jnp.dot(y_attn, wout_ref[:a0, :], preferred_element_type=F32)
    acc += jnp.dot(y_pool, wout_ref[a0:a1, :], preferred_element_type=F32)
    acc += jnp.dot(y_conv, wout_ref[a1:, :], preferred_element_type=F32)
    out_ref[...] = x_ref[...] + acc


def _mixer_out(x, outs, lses, rest, conv_w, pool_w_bd, pool_scale, w_out, S, tm):
    T, D = x.shape
    hb = tm // POOL_HALO
    row = lambda i: (i, 0)
    const = lambda i: (0, 0)
    pwin = jnp.asarray(np.repeat(np.array(POOL_WINDOWS, np.float32), POOL_GROUP_DIM)[None, :])
    return pl.pallas_call(
        partial(_mixer_out_kernel, tm=tm, tiles_per_seq=S // tm),
        grid=(T // tm,),
        in_specs=[pl.BlockSpec((tm, D), row)]
                 + [pl.BlockSpec((tm, ATTN_WIDTH), row)] * 3
                 + [pl.BlockSpec((tm, LANES), row)] * 3
                 + [pl.BlockSpec((tm, REST_WIDTH), row),
                    pl.BlockSpec((POOL_HALO, REST_WIDTH), lambda i: (jnp.maximum(i * hb - 1, 0), 0)),
                    pl.BlockSpec(conv_w.shape, const),
                    pl.BlockSpec(pool_w_bd.shape, const),
                    pl.BlockSpec((1, POOL_WIDTH), const),
                    pl.BlockSpec((1, POOL_WIDTH), const),
                    pl.BlockSpec(w_out.shape, const)],
        out_specs=pl.BlockSpec((tm, D), row),
        out_shape=jax.ShapeDtypeStruct((T, D), F32),
        compiler_params=_cparams("parallel"),
    )(x, *outs, *lses, rest, rest, conv_w, pool_w_bd, pool_scale, pwin, w_out)


def _xattn_kernel(x_ref, g_ref, wq_ref, kv_ref, wo_ref, out_ref):
    x = x_ref[...]
    xn = _rms(x, g_ref[...]).astype(BF16)
    q = jnp.dot(xn, wq_ref[...], preferred_element_type=F32).astype(BF16)
    kv = kv_ref[0]
    scale = XATTN_HEAD_DIM ** -0.5
    heads = []
    for h in range(XATTN_HEADS):
        sl = slice(h * XATTN_HEAD_DIM, (h + 1) * XATTN_HEAD_DIM)
        k = kv[:, sl]
        v = kv[:, XATTN_WIDTH + h * XATTN_HEAD_DIM:XATTN_WIDTH + (h + 1) * XATTN_HEAD_DIM]
        s = lax.dot_general(q[:, sl], k, (((1,), (1,)), ((), ())),
                            preferred_element_type=F32) * scale
        m = jnp.max(s, axis=-1, keepdims=True)
        p = jnp.exp(s - m)
        l = jnp.sum(p, axis=-1, keepdims=True)
        heads.append(jnp.dot(p.astype(BF16), v, preferred_element_type=F32) / l)
    o = jnp.concatenate(heads, axis=1).astype(BF16)
    out_ref[...] = x + jnp.dot(o, wo_ref[...], preferred_element_type=F32)


def _xattn(x, g, wq, kv, wo, S, tm):
    T, D = x.shape
    M = kv.shape[1]
    tiles_per_seq = S // tm
    return pl.pallas_call(
        _xattn_kernel,
        grid=(T // tm,),
        in_specs=[pl.BlockSpec((tm, D), lambda i: (i, 0)),
                  pl.BlockSpec((1, D), lambda i: (0, 0)),
                  pl.BlockSpec(wq.shape, lambda i: (0, 0)),
                  pl.BlockSpec((1, M, 2 * XATTN_WIDTH), lambda i: (i // tiles_per_seq, 0, 0)),
                  pl.BlockSpec(wo.shape, lambda i: (0, 0))],
        out_specs=pl.BlockSpec((tm, D), lambda i: (i, 0)),
        out_shape=jax.ShapeDtypeStruct((T, D), F32),
        compiler_params=_cparams("parallel"),
    )(x, g, wq, kv, wo)


def _swiglu_chunk(hn, wg_ref, wu_ref, w2_ref):
    a = jnp.dot(hn, wg_ref[...], preferred_element_type=F32)
    b = jnp.dot(hn, wu_ref[...], preferred_element_type=F32)
    h = (a * jax.nn.sigmoid(a) * b).astype(BF16)
    return jnp.dot(h, w2_ref[...], preferred_element_type=F32)


def _dense_ffn_kernel(x_ref, g_ref, wg_ref, wu_ref, w2_ref, out_ref, hn_ref, acc_ref):
    j = pl.program_id(1)

    @pl.when(j == 0)
    def _():
        hn_ref[...] = _rms(x_ref[...], g_ref[...]).astype(BF16)
        acc_ref[...] = jnp.zeros_like(acc_ref)

    acc_ref[...] += _swiglu_chunk(hn_ref[...], wg_ref, wu_ref, w2_ref)

    @pl.when(j == pl.num_programs(1) - 1)
    def _():
        out_ref[...] = x_ref[...] + acc_ref[...]


def _dense_ffn(x, g, w13, w2, tm, tf):
    T, D = x.shape
    dff = w2.shape[0]
    nj = dff // tf
    return pl.pallas_call(
        _dense_ffn_kernel,
        grid=(T // tm, nj),
        in_specs=[pl.BlockSpec((tm, D), lambda i, j: (i, 0)),
                  pl.BlockSpec((1, D), lambda i, j: (0, 0)),
                  pl.BlockSpec((D, tf), lambda i, j: (0, j)),
                  pl.BlockSpec((D, tf), lambda i, j: (0, j + nj)),
                  pl.BlockSpec((tf, D), lambda i, j: (j, 0))],
        out_specs=pl.BlockSpec((tm, D), lambda i, j: (i, 0)),
        out_shape=jax.ShapeDtypeStruct((T, D), F32),
        scratch_shapes=[pltpu.VMEM((tm, D), BF16), pltpu.VMEM((tm, D), F32)],
        compiler_params=_cparams("parallel", "arbitrary"),
    )(x, g, w13, w13, w2)


def _store_slab(ref, v):
    n = v.shape[0]
    for s in range(SLAB_ROWS):
        ref[pl.ds(s, n, stride=SLAB_ROWS), :] = v[:, s * LANES:(s + 1) * LANES]


def _load_slab(ref, first, n):
    return jnp.concatenate(
        [ref[pl.ds(first * SLAB_ROWS + s, n, stride=SLAB_ROWS), :] for s in range(SLAB_ROWS)],
        axis=1)


def _slab(row):
    return pl.ds(pl.multiple_of(row * SLAB_ROWS, SLAB_ROWS), SLAB_ROWS)


def _router_kernel(x_ref, g_ref, wr_ref, hn_ref, route_ref):
    hn = _rms(x_ref[...], g_ref[...])
    _store_slab(hn_ref, hn)
    logits = jnp.dot(hn.astype(BF16), wr_ref[...], preferred_element_type=F32)
    lane = lax.broadcasted_iota(jnp.int32, logits.shape, 1).astype(F32)
    logits = jnp.where(lane < N_EXPERTS, logits, -jnp.inf)
    m1 = jnp.max(logits, axis=-1, keepdims=True)
    i1 = jnp.min(jnp.where(logits == m1, lane, float(LANES)), axis=-1, keepdims=True)
    rest = jnp.where(lane == i1, -jnp.inf, logits)
    m2 = jnp.max(rest, axis=-1, keepdims=True)
    i2 = jnp.min(jnp.where(rest == m2, lane, float(LANES)), axis=-1, keepdims=True)
    e2 = jnp.exp(m2 - m1)
    p1 = 1.0 / (1.0 + e2)
    p2 = e2 * p1
    route_ref[...] = jnp.where(lane == 0, i1, jnp.where(lane == 1, i2,
                     jnp.where(lane == 2, p1, jnp.where(lane == 3, p2, 0.0))))


def _router(x, g, wr, tm):
    T, D = x.shape
    return pl.pallas_call(
        _router_kernel,
        grid=(T // tm,),
        in_specs=[pl.BlockSpec((tm, D), lambda i: (i, 0)),
                  pl.BlockSpec((1, D), lambda i: (0, 0)),
                  pl.BlockSpec((D, LANES), lambda i: (0, 0))],
        out_specs=[pl.BlockSpec((tm * SLAB_ROWS, LANES), lambda i: (i, 0)),
                   pl.BlockSpec((tm, LANES), lambda i: (i, 0))],
        out_shape=[jax.ShapeDtypeStruct((T * SLAB_ROWS, LANES), F32),
                   jax.ShapeDtypeStruct((T, LANES), F32)],
        compiler_params=_cparams("parallel"),
    )(x, g, wr)


def _row_copy(src_ref, src_row, dst_ref, dst_row, sem):
    return pltpu.make_async_copy(src_ref.at[_slab(src_row)], dst_ref.at[_slab(dst_row)], sem)


def _dispatch_kernel(pos_ref, pad_ref, hn_ref, xs_ref, sem, *, tm):
    i = pl.program_id(0)
    base = i * tm * TOP_K

    def start(r, c):
        for kk in range(TOP_K):
            _row_copy(hn_ref, r, xs_ref, pos_ref[base + r * TOP_K + kk], sem).start()
        return c

    lax.fori_loop(0, tm, start, 0)

    def wait(r, c):
        _row_copy(hn_ref, 0, xs_ref, 0, sem).wait()
        return c

    lax.fori_loop(0, tm * TOP_K, wait, 0)

    @pl.when(i == 0)
    def _():
        npad = pad_ref.shape[0]

        def pstart(r, c):
            _row_copy(hn_ref, 0, xs_ref, pad_ref[r], sem).start()
            return c

        lax.fori_loop(0, npad, pstart, 0)
        lax.fori_loop(0, npad, wait, 0)


def _dispatch(hn_slab, pos, pad_rows, n_rows, tm):
    T = hn_slab.shape[0] // SLAB_ROWS
    return pl.pallas_call(
        partial(_dispatch_kernel, tm=tm),
        grid_spec=pltpu.PrefetchScalarGridSpec(
            num_scalar_prefetch=2,
            grid=(T // tm,),
            in_specs=[pl.BlockSpec((tm * SLAB_ROWS, LANES), lambda i, *_: (i, 0))],
            out_specs=pl.BlockSpec(memory_space=pl.ANY),
            scratch_shapes=[pltpu.SemaphoreType.DMA(())]),
        out_shape=jax.ShapeDtypeStruct((n_rows * SLAB_ROWS, LANES), F32),
        compiler_params=_cparams("arbitrary"),
    )(pos, pad_rows, hn_slab)


def _expert_ffn_kernel(te_ref, tv_ref, xs_ref, wg_ref, wu_ref, w2_ref, y_ref, hn_ref, acc_ref):
    i, j = pl.program_id(0), pl.program_id(1)
    last = pl.num_programs(1) - 1
    valid = tv_ref[i] > 0

    @pl.when(valid & (j == 0))
    def _():
        hn_ref[...] = _load_slab(xs_ref, 0, hn_ref.shape[0]).astype(BF16)
        acc_ref[...] = jnp.zeros_like(acc_ref)

    @pl.when(valid)
    def _():
        acc_ref[...] += _swiglu_chunk(hn_ref[...], wg_ref.at[0], wu_ref.at[0], w2_ref.at[0])

    @pl.when(valid & (j == last))
    def _():
        _store_slab(y_ref, acc_ref[...])

    @pl.when(jnp.logical_not(valid) & (j == last))
    def _():
        y_ref[...] = jnp.zeros_like(y_ref)


def _expert_ffn(xs, tile_expert, tile_valid, w13, w2, tm, tf):
    n_rows = xs.shape[0] // SLAB_ROWS
    D = D_MODEL
    dff = w2.shape[1]
    nj = dff // tf
    jj = lambda i, j, tv: jnp.where(tv[i] > 0, j, nj - 1)
    return pl.pallas_call(
        _expert_ffn_kernel,
        grid_spec=pltpu.PrefetchScalarGridSpec(
            num_scalar_prefetch=2,
            grid=(n_rows // tm, nj),
            in_specs=[pl.BlockSpec((tm * SLAB_ROWS, LANES), lambda i, j, te, tv: (i, 0)),
                      pl.BlockSpec((1, D, tf), lambda i, j, te, tv: (te[i], 0, jj(i, j, tv))),
                      pl.BlockSpec((1, D, tf), lambda i, j, te, tv: (te[i], 0, jj(i, j, tv) + nj)),
                      pl.BlockSpec((1, tf, D), lambda i, j, te, tv: (te[i], jj(i, j, tv), 0))],
            out_specs=pl.BlockSpec((tm * SLAB_ROWS, LANES), lambda i, j, te, tv: (i, 0)),
            scratch_shapes=[pltpu.VMEM((tm, D), BF16), pltpu.VMEM((tm, D), F32)]),
        out_shape=jax.ShapeDtypeStruct((n_rows * SLAB_ROWS, LANES), F32),
        compiler_params=_cparams("parallel", "arbitrary"),
    )(tile_expert, tile_valid, xs, w13, w13, w2)


def _combine_kernel(pos_ref, x_ref, route_ref, y_ref, out_ref, buf, sem, *, tm):
    i = pl.program_id(0)
    base = i * tm * TOP_K

    def start(r, c):
        for kk in range(TOP_K):
            _row_copy(y_ref, pos_ref[base + r * TOP_K + kk], buf, kk * tm + r, sem).start()
        return c

    lax.fori_loop(0, tm, start, 0)

    def wait(r, c):
        _row_copy(y_ref, 0, buf, 0, sem).wait()
        return c

    lax.fori_loop(0, tm * TOP_K, wait, 0)

    route = route_ref[...]
    acc = x_ref[...]
    for kk in range(TOP_K):
        gate = route[:, TOP_K + kk:TOP_K + kk + 1]
        acc = acc + gate * _load_slab(buf, kk * tm, tm)
    out_ref[...] = acc


def _combine(x, route, y, pos, tm):
    T, D = x.shape
    return pl.pallas_call(
        partial(_combine_kernel, tm=tm),
        grid_spec=pltpu.PrefetchScalarGridSpec(
            num_scalar_prefetch=1,
            grid=(T // tm,),
            in_specs=[pl.BlockSpec((tm, D), lambda i, *_: (i, 0)),
                      pl.BlockSpec((tm, LANES), lambda i, *_: (i, 0)),
                      pl.BlockSpec(memory_space=pl.ANY)],
            out_specs=pl.BlockSpec((tm, D), lambda i, *_: (i, 0)),
            scratch_shapes=[pltpu.VMEM((tm * TOP_K * SLAB_ROWS, LANES), F32),
                            pltpu.SemaphoreType.DMA(())]),
        out_shape=jax.ShapeDtypeStruct((T, D), F32),
        compiler_params=_cparams("arbitrary"),
    )(pos, x, route, y)


def _routing_tables(expert_ids, tm, n_tiles):
    e_flat = expert_ids.reshape(-1)
    onehot = (e_flat[:, None] == jnp.arange(N_EXPERTS, dtype=jnp.int32)[None, :]).astype(jnp.int32)
    csum = jnp.cumsum(onehot, axis=0)
    counts = csum[-1]
    rank = jnp.sum(onehot * (csum - 1), axis=1)
    padded = ((counts + tm - 1) // tm) * tm
    ends = jnp.cumsum(padded)
    starts = ends - padded
    pos = (jnp.sum(onehot * starts[None, :], axis=1) + rank).astype(jnp.int32)

    tile_start = jnp.arange(n_tiles, dtype=jnp.int32) * tm
    tile_valid = (tile_start < ends[-1]).astype(jnp.int32)
    te = jnp.sum((tile_start[:, None] >= ends[None, :]).astype(jnp.int32), axis=1)
    last_e = jnp.sum((ends[-1] - 1 >= ends).astype(jnp.int32))
    tile_expert = jnp.where(tile_valid > 0, te, last_e).astype(jnp.int32)

    npad_e = padded - counts
    pcum = jnp.cumsum(npad_e)
    k = jnp.arange(N_EXPERTS * tm, dtype=jnp.int32)
    ek = jnp.sum((k[:, None] >= pcum[None, :]).astype(jnp.int32), axis=1)
    ohk = (ek[:, None] == jnp.arange(N_EXPERTS, dtype=jnp.int32)[None, :]).astype(jnp.int32)
    in_group = jnp.sum(ohk * (starts + counts - (pcum - npad_e))[None, :], axis=1) + k
    pad = jnp.where(ek < N_EXPERTS, in_group, ends[-1] + k - pcum[-1]).astype(jnp.int32)
    return pos, tile_expert, tile_valid, pad


def _moe_ffn(x, g, wr, w13, w2, tm, tme, tf):
    T, D = x.shape
    n_tiles = (T * TOP_K) // tme + N_EXPERTS
    hn_slab, route = _router(x, g, wr, tm)
    expert_ids = route[:, :TOP_K].astype(jnp.int32)
    pos, tile_expert, tile_valid, pad = _routing_tables(expert_ids, tme, n_tiles)
    xs = _dispatch(hn_slab, pos, pad, n_tiles * tme, tm)
    y = _expert_ffn(xs, tile_expert, tile_valid, w13, w2, tme, tf)
    return _combine(x, route, y, pos, tm)


def _final_norm_kernel(x_ref, g_ref, o_ref):
    o_ref[...] = _rms(x_ref[...], g_ref[...])


def _final_norm(x, g, tm):
    T, D = x.shape
    return pl.pallas_call(
        _final_norm_kernel,
        grid=(T // tm,),
        in_specs=[pl.BlockSpec((tm, D), lambda i: (i, 0)), pl.BlockSpec((1, D), lambda i: (0, 0))],
        out_specs=pl.BlockSpec((tm, D), lambda i: (i, 0)),
        out_shape=jax.ShapeDtypeStruct((T, D), F32),
        compiler_params=_cparams("parallel"),
    )(x, g)


def _block_diag(pw):
    G, C, _ = pw.shape
    eye = jnp.eye(G, dtype=pw.dtype)
    return (eye[:, None, :, None] * pw[:, :, None, :]).reshape(G * C, G * C)


def kernel(x, mem, rel_bias, mem_norm_g, mix_norm_g, w_in, conv_w, pool_w, pool_scale, w_out,
           xattn_norm_g, wq_x, wkv_x, wo_x, ffn_norm_g, w13_dense, w2_dense, router_w, w13_moe,
           w2_moe, final_norm_g):
    B, S, D = x.shape
    depth = w_in.shape[0]
    M = mem.shape[1]
    T = B * S
    tm = min(ROW_TILE, S)
    assert D == D_MODEL and S % (max(DILATIONS) * Q_BLOCK) == 0 and S % tm == 0

    xf = x.reshape(T, D)
    row2 = lambda v: v.reshape(1, -1)
    q_scale = jnp.where(jnp.arange(w_in.shape[-1]) < ATTN_WIDTH, ATTN_HEAD_DIM ** -0.5, 1.0)
    w_in_b = (w_in * q_scale).astype(BF16)
    biases = [_band_bias(rel_bias, d) for d in DILATIONS]
    wr_pad = jnp.pad(router_w, ((0, 0), (0, 0), (0, LANES - N_EXPERTS))).astype(BF16)

    mem_f = mem.reshape(B * M, D)
    for layer in range(depth):
        qkv, rest = _in_proj(xf, row2(mix_norm_g[layer]), w_in_b[layer], tm)
        branch = [_dilated_attn_branch(qkv, bias, B, S, d) for bias, d in zip(biases, DILATIONS)]
        xf = _mixer_out(xf, [o for o, _ in branch], [l for _, l in branch], rest,
                        conv_w[layer], _block_diag(pool_w[layer]).astype(BF16),
                        row2(pool_scale[layer]), w_out[layer].astype(BF16), S, tm)

        kv = _norm_proj(mem_f, row2(mem_norm_g), wkv_x[layer].astype(BF16), BF16, min(tm, B * M))
        xf = _xattn(xf, row2(xattn_norm_g[layer]), wq_x[layer].astype(BF16),
                    kv.reshape(B, M, 2 * XATTN_WIDTH), wo_x[layer].astype(BF16), S, tm)

        g = row2(ffn_norm_g[layer])
        if layer % 2 == 0:
            w13, w2 = w13_dense[layer // 2].astype(BF16), w2_dense[layer // 2].astype(BF16)
            xf = _dense_ffn(xf, g, w13, w2, tm, w2.shape[0] // 2)
        else:
            w13, w2 = w13_moe[layer // 2].astype(BF16), w2_moe[layer // 2].astype(BF16)
            xf = _moe_ffn(xf, g, wr_pad[layer // 2], w13, w2, tm, EXPERT_ROW_TILE, 512)
    return _final_norm(xf, row2(final_norm_g), tm).reshape(B, S, D)
```

```python
from functools import partial

import numpy as np
import jax
import jax.numpy as jnp
from jax import lax
from jax.experimental import pallas as pl
from jax.experimental.pallas import tpu as pltpu

F32 = jnp.float32
BF16 = jnp.bfloat16

D_MODEL = 1024
ATTN_HEADS = 8
ATTN_HEAD_DIM = 64
ATTN_WIDTH = ATTN_HEADS * ATTN_HEAD_DIM
POOL_WIDTH = 256
POOL_GROUP_DIM = 64
POOL_WINDOWS = (2, 4, 8, 16)
POOL_HALO = 16
CONV_WIDTH = 256
REST_WIDTH = POOL_WIDTH + 3 * CONV_WIDTH
DILATIONS = (1, 4, 16)
WINDOW_STEPS = 128
Q_BLOCK = 128
NEG_INF = -1e30
REL_BUCKETS = 32
REL_MAX_DIST = 128
XATTN_HEADS = 4
XATTN_HEAD_DIM = 128
XATTN_WIDTH = XATTN_HEADS * XATTN_HEAD_DIM
N_EXPERTS = 8
TOP_K = 2
EPS = 1e-6

LANES = 128
SLAB_ROWS = D_MODEL // LANES
ROW_TILE = 512
EXPERT_ROW_TILE = 512
ATTN_Q_BLOCKS = 4
DMA_ISSUE_UNROLL = 4
VMEM_LIMIT = 56 * 1024 * 1024


def _cparams(*sem):
    return pltpu.CompilerParams(dimension_semantics=sem, vmem_limit_bytes=VMEM_LIMIT)


def _rms(x, g):
    return x * lax.rsqrt(jnp.mean(x * x, axis=-1, keepdims=True) + EPS) * g


def _norm_proj_kernel(x_ref, g_ref, w_ref, o_ref):
    xn = _rms(x_ref[...], g_ref[...]).astype(BF16)
    o_ref[...] = jnp.dot(xn, w_ref[...], preferred_element_type=F32).astype(o_ref.dtype)


def _norm_proj(x, g, w, out_dtype, tm):
    T, D = x.shape
    N = w.shape[1]
    return pl.pallas_call(
        _norm_proj_kernel,
        grid=(T // tm,),
        in_specs=[pl.BlockSpec((tm, D), lambda i: (i, 0)),
                  pl.BlockSpec((1, D), lambda i: (0, 0)),
                  pl.BlockSpec((D, N), lambda i: (0, 0))],
        out_specs=pl.BlockSpec((tm, N), lambda i: (i, 0)),
        out_shape=jax.ShapeDtypeStruct((T, N), out_dtype),
        compiler_params=_cparams("parallel"),
        name="norm_proj",
    )(x, g, w)


def _in_proj_kernel(x_ref, g_ref, w_ref, *refs):
    qkv_refs, rest_ref, acc_ref = refs[:-2], refs[-2], refs[-1]
    nchunk, tm, _ = acc_ref.shape
    nq = nchunk * LANES
    xn = _rms(x_ref[...], g_ref[...]).astype(BF16)
    acc = jnp.dot(xn, w_ref[:, :nq], preferred_element_type=F32)
    qkv_refs[0][0, 0] = acc.astype(BF16)
    for c in range(nchunk):
        acc_ref[c] = acc[:, c * LANES:(c + 1) * LANES]
    for d, qkv_ref in zip(DILATIONS[1:], qkv_refs[1:]):
        for r in range(d):
            rows = [acc_ref[c, pl.ds(r, tm // d, stride=d), :] for c in range(nchunk)]
            qkv_ref[0, r] = jnp.concatenate(rows, axis=1).astype(BF16)
    rest_ref[...] = jnp.dot(xn, w_ref[:, nq:], preferred_element_type=F32)


def _in_proj(x, g, w, B, S, tm):
    T, D = x.shape
    N = w.shape[1]
    nq = 3 * ATTN_WIDTH
    tps = S // tm
    res_spec = lambda d: pl.BlockSpec((1, d, tm // d, nq), lambda i: (i // tps, 0, i % tps, 0))
    return pl.pallas_call(
        _in_proj_kernel,
        grid=(T // tm,),
        in_specs=[pl.BlockSpec((tm, D), lambda i: (i, 0)),
                  pl.BlockSpec((1, D), lambda i: (0, 0)),
                  pl.BlockSpec((D, N), lambda i: (0, 0))],
        out_specs=[res_spec(d) for d in DILATIONS]
                  + [pl.BlockSpec((tm, N - nq), lambda i: (i, 0))],
        out_shape=[jax.ShapeDtypeStruct((B, d, S // d, nq), BF16) for d in DILATIONS]
                  + [jax.ShapeDtypeStruct((T, N - nq), F32)],
        scratch_shapes=[pltpu.VMEM((nq // LANES, tm, LANES), F32)],
        compiler_params=_cparams("parallel"),
        name="in_proj",
    )(x, g, w)


def _t5_causal_bucket(dist):
    max_exact = REL_BUCKETS // 2
    d = np.maximum(dist, 1).astype(np.float32)
    large = max_exact + (np.log(d / max_exact) / np.log(REL_MAX_DIST / max_exact)
                         * (REL_BUCKETS - max_exact)).astype(np.int32)
    large = np.minimum(large, REL_BUCKETS - 1)
    return np.where(dist < max_exact, dist, large).astype(np.int32)


def _band_bias(rel_bias, dil):
    row = np.arange(Q_BLOCK)[:, None]
    col = np.arange(2 * Q_BLOCK)[None, :]
    steps = Q_BLOCK + row - col
    valid = (steps >= 0) & (steps <= WINDOW_STEPS)
    bucket = _t5_causal_bucket(dil * np.clip(steps, 0, WINDOW_STEPS))
    onehot = (bucket[..., None] == np.arange(REL_BUCKETS)).astype(np.float32)
    b = jnp.einsum('qcb,bh->hqc', jnp.asarray(onehot), rel_bias.astype(F32),
                   precision=lax.Precision.HIGHEST)
    masks = np.stack([valid, valid & (col >= Q_BLOCK)])
    return jnp.where(jnp.asarray(masks)[:, None], b[None], NEG_INF)


def _window_attn_kernel(q_ref, kp_ref, k_ref, vp_ref, v_ref, bias_ref, o_ref, lse_ref, *, nq):
    first_step = jnp.where(pl.program_id(1) == 0, 1, 0)
    lane = lax.broadcasted_iota(jnp.int32, (Q_BLOCK, LANES), 1)
    low_half = lane < ATTN_HEAD_DIM
    Q = Q_BLOCK
    for j in range(nq):
        rows = slice(j * Q, (j + 1) * Q)
        if j == 0:
            k2 = jnp.concatenate([kp_ref[0], k_ref[0, :Q, :]], axis=0)
            v2 = jnp.concatenate([vp_ref[0], v_ref[0, :Q, :]], axis=0)
            table = first_step
        else:
            k2 = k_ref[0, (j - 1) * Q:(j + 1) * Q, :]
            v2 = v_ref[0, (j - 1) * Q:(j + 1) * Q, :]
            table = 0
        lse = jnp.zeros((Q, LANES), F32)
        for g in range(ATTN_WIDTH // LANES):
            cols = slice(g * LANES, (g + 1) * LANES)
            qg, kg, vg = q_ref[0, rows, cols], k2[:, cols], v2[:, cols]
            og = None
            for half in range(2):
                h = 2 * g + half
                keep = low_half if half == 0 else jnp.logical_not(low_half)
                qh = jnp.where(keep, qg, jnp.zeros_like(qg))
                s = lax.dot_general(qh, kg, (((1,), (1,)), ((), ())),
                                    preferred_element_type=F32)
                s = s + bias_ref[table, h]
                m = jnp.max(s, axis=-1, keepdims=True)
                p = jnp.exp(s - m)
                l = jnp.sum(p, axis=-1, keepdims=True)
                o = jnp.dot(p.astype(BF16), vg, preferred_element_type=F32) * (1.0 / l)
                og = o if half == 0 else jnp.where(low_half, og, o)
                lse = jnp.where(lane == h, m + jnp.log(l), lse)
            o_ref[0, rows, cols] = og.astype(o_ref.dtype)
        lse_ref[0, rows, :] = lse


def _window_attn(qkv, bias, nq):
    N, L, _ = qkv.shape
    W = ATTN_WIDTH
    nq = min(nq, L // Q_BLOCK)
    step = nq * Q_BLOCK
    prev = lambda i: jnp.maximum(i * nq - 1, 0)
    return pl.pallas_call(
        partial(_window_attn_kernel, nq=nq),
        grid=(N, L // step),
        in_specs=[pl.BlockSpec((1, step, W), lambda n, i: (n, i, 0)),
                  pl.BlockSpec((1, Q_BLOCK, W), lambda n, i: (n, prev(i), 1)),
                  pl.BlockSpec((1, step, W), lambda n, i: (n, i, 1)),
                  pl.BlockSpec((1, Q_BLOCK, W), lambda n, i: (n, prev(i), 2)),
                  pl.BlockSpec((1, step, W), lambda n, i: (n, i, 2)),
                  pl.BlockSpec(bias.shape, lambda n, i: (0, 0, 0, 0))],
        out_specs=[pl.BlockSpec((1, step, W), lambda n, i: (n, i, 0)),
                   pl.BlockSpec((1, step, LANES), lambda n, i: (n, i, 0))],
        out_shape=[jax.ShapeDtypeStruct((N, L, W), BF16),
                   jax.ShapeDtypeStruct((N, L, LANES), F32)],
        compiler_params=_cparams("parallel", "parallel"),
        name="window_attn",
    )(qkv, qkv, qkv, qkv, qkv, bias)


def _token_order(src_ref, scratch_ref, d):
    if d == 1:
        return src_ref[0, 0].astype(F32)
    nchunk, n, _ = scratch_ref.shape
    for r in range(d):
        part = src_ref[0, r].astype(F32)
        for c in range(nchunk):
            scratch_ref[c, pl.ds(r, n // d, stride=d), :] = part[:, c * LANES:(c + 1) * LANES]
    return jnp.concatenate([scratch_ref[c] for c in range(nchunk)], axis=1)


def _mixer_out_kernel(x_ref, o1_ref, o2_ref, o3_ref, l1_ref, l2_ref, l3_ref, rest_ref, halo_ref,
                      convw_ref, poolw_ref, pscale_ref, pwin_ref, expand_ref, wout_ref, out_ref,
                      os2_ref, os3_ref, ls2_ref, ls3_ref, *, tm, tiles_per_seq):
    it = pl.program_id(0) % tiles_per_seq
    lses = [_token_order(r, s, d)
            for r, s, d in zip((l1_ref, l2_ref, l3_ref), (None, ls2_ref, ls3_ref), DILATIONS)]
    outs = [_token_order(r, s, d)
            for r, s, d in zip((o1_ref, o2_ref, o3_ref), (None, os2_ref, os3_ref), DILATIONS)]
    lm = jnp.maximum(jnp.maximum(lses[0], lses[1]), lses[2])
    es = [jnp.exp(l - lm) for l in lses]
    inv = 1.0 / (es[0] + es[1] + es[2])
    y_attn = jnp.zeros((tm, ATTN_WIDTH), F32)
    for e, o in zip(es, outs):
        w = e * inv
        w_hi = w.astype(BF16)
        w_lo = (w - w_hi.astype(F32)).astype(BF16)
        wide = (jnp.dot(w_hi, expand_ref[...], preferred_element_type=F32)
                + jnp.dot(w_lo, expand_ref[...], preferred_element_type=F32))
        y_attn = y_attn + wide * o
    y_attn = y_attn.astype(BF16)

    rest = rest_ref[...]
    halo = jnp.where(it == 0, 0.0, halo_ref[...])
    ext = jnp.concatenate([halo, rest], axis=0)

    u = ext[:, :POOL_WIDTH]
    s2 = u + pltpu.roll(u, 1, 0)
    s4 = s2 + pltpu.roll(s2, 2, 0)
    s8 = s4 + pltpu.roll(s4, 4, 0)
    s16 = s8 + pltpu.roll(s8, 8, 0)
    col = lax.broadcasted_iota(jnp.int32, u.shape, 1)
    g = POOL_GROUP_DIM
    wsum = jnp.where(col < g, s2, jnp.where(col < 2 * g, s4, jnp.where(col < 3 * g, s8, s16)))
    pos = it * tm + lax.broadcasted_iota(jnp.int32, (tm, POOL_WIDTH), 0)
    count = jnp.minimum((pos + 1).astype(F32), pwin_ref[...])
    pooled = wsum[POOL_HALO:] / count - u[POOL_HALO:]
    y_pool = jnp.dot(pooled.astype(BF16), poolw_ref[...], preferred_element_type=F32)
    y_pool = (y_pool * pscale_ref[...]).astype(BF16)

    b_gate = rest[:, POOL_WIDTH:POOL_WIDTH + CONV_WIDTH]
    cu = ext[:, POOL_WIDTH + CONV_WIDTH:POOL_WIDTH + 2 * CONV_WIDTH] * ext[:, POOL_WIDTH + 2 * CONV_WIDTH:]
    y = (convw_ref[0:1, :] * cu + convw_ref[1:2, :] * pltpu.roll(cu, 1, 0)
         + convw_ref[2:3, :] * pltpu.roll(cu, 2, 0))
    y_conv = (b_gate * y[POOL_HALO:]).astype(BF16)

    a0, a1 = ATTN_WIDTH, ATTN_WIDTH + POOL_WIDTH
    acc = jnp.dot(y_attn, wout_ref[:a0, :], preferred_element_type=F32)
    acc += jnp.dot(y_pool, wout_ref[a0:a1, :], preferred_element_type=F32)
    acc += jnp.dot(y_conv, wout_ref[a1:, :], preferred_element_type=F32)
    out_ref[...] = x_ref[...] + acc


def _mixer_out(x, outs, lses, rest, conv_w, pool_w_bd, pool_scale, w_out, S, tm):
    T, D = x.shape
    hb = tm // POOL_HALO
    tps = S // tm
    row = lambda i: (i, 0)
    const = lambda i: (0, 0)
    res_spec = lambda d, c: pl.BlockSpec((1, d, tm // d, c), lambda i: (i // tps, 0, i % tps, 0))
    pwin = jnp.asarray(np.repeat(np.array(POOL_WINDOWS, np.float32), POOL_GROUP_DIM)[None, :])
    expand = np.zeros((LANES, ATTN_WIDTH), np.float32)
    for h in range(ATTN_HEADS):
        expand[h, h * ATTN_HEAD_DIM:(h + 1) * ATTN_HEAD_DIM] = 1.0
    expand = jnp.asarray(expand, dtype=BF16)
    return pl.pallas_call(
        partial(_mixer_out_kernel, tm=tm, tiles_per_seq=tps),
        grid=(T // tm,),
        scratch_shapes=[pltpu.VMEM((ATTN_WIDTH // LANES, tm, LANES), F32)] * 2
                       + [pltpu.VMEM((1, tm, LANES), F32)] * 2,
        name="mixer_out",
        in_specs=[pl.BlockSpec((tm, D), row)]
                 + [res_spec(d, ATTN_WIDTH) for d in DILATIONS]
                 + [res_spec(d, LANES) for d in DILATIONS]
                 + [pl.BlockSpec((tm, REST_WIDTH), row),
                    pl.BlockSpec((POOL_HALO, REST_WIDTH), lambda i: (jnp.maximum(i * hb - 1, 0), 0)),
                    pl.BlockSpec(conv_w.shape, const),
                    pl.BlockSpec(pool_w_bd.shape, const),
                    pl.BlockSpec((1, POOL_WIDTH), const),
                    pl.BlockSpec((1, POOL_WIDTH), const),
                    pl.BlockSpec(expand.shape, const),
                    pl.BlockSpec(w_out.shape, const)],
        out_specs=pl.BlockSpec((tm, D), row),
        out_shape=jax.ShapeDtypeStruct((T, D), F32),
        compiler_params=_cparams("parallel"),
    )(x, *outs, *lses, rest, rest, conv_w, pool_w_bd, pool_scale, pwin, expand, w_out)


def _xattn_kernel(x_ref, g_ref, wq_ref, kv_ref, wo_ref, out_ref):
    x = x_ref[...]
    xn = _rms(x, g_ref[...]).astype(BF16)
    q = jnp.dot(xn, wq_ref[...], preferred_element_type=F32).astype(BF16)
    kv = kv_ref[0]
    scale = XATTN_HEAD_DIM ** -0.5
    heads = []
    for h in range(XATTN_HEADS):
        sl = slice(h * XATTN_HEAD_DIM, (h + 1) * XATTN_HEAD_DIM)
        k = kv[:, sl]
        v = kv[:, XATTN_WIDTH + h * XATTN_HEAD_DIM:XATTN_WIDTH + (h + 1) * XATTN_HEAD_DIM]
        s = lax.dot_general(q[:, sl], k, (((1,), (1,)), ((), ())),
                            preferred_element_type=F32) * scale
        m = jnp.max(s, axis=-1, keepdims=True)
        p = jnp.exp(s - m)
        l = jnp.sum(p, axis=-1, keepdims=True)
        heads.append(jnp.dot(p.astype(BF16), v, preferred_element_type=F32) / l)
    o = jnp.concatenate(heads, axis=1).astype(BF16)
    out_ref[...] = x + jnp.dot(o, wo_ref[...], preferred_element_type=F32)


def _xattn(x, g, wq, kv, wo, S, tm):
    T, D = x.shape
    M = kv.shape[1]
    tiles_per_seq = S // tm
    return pl.pallas_call(
        _xattn_kernel,
        grid=(T // tm,),
        in_specs=[pl.BlockSpec((tm, D), lambda i: (i, 0)),
                  pl.BlockSpec((1, D), lambda i: (0, 0)),
                  pl.BlockSpec(wq.shape, lambda i: (0, 0)),
                  pl.BlockSpec((1, M, 2 * XATTN_WIDTH), lambda i: (i // tiles_per_seq, 0, 0)),
                  pl.BlockSpec(wo.shape, lambda i: (0, 0))],
        out_specs=pl.BlockSpec((tm, D), lambda i: (i, 0)),
        out_shape=jax.ShapeDtypeStruct((T, D), F32),
        compiler_params=_cparams("parallel"),
        name="xattn",
    )(x, g, wq, kv, wo)


def _swiglu_chunk(hn, wg_ref, wu_ref, w2_ref):
    a = jnp.dot(hn, wg_ref[...], preferred_element_type=F32)
    b = jnp.dot(hn, wu_ref[...], preferred_element_type=F32)
    h = (a * jax.nn.sigmoid(a) * b).astype(BF16)
    return jnp.dot(h, w2_ref[...], preferred_element_type=F32)


def _dense_ffn_kernel(x_ref, g_ref, wg_ref, wu_ref, w2_ref, out_ref, hn_ref, acc_ref):
    j = pl.program_id(1)

    @pl.when(j == 0)
    def _():
        hn_ref[...] = _rms(x_ref[...], g_ref[...]).astype(BF16)
        acc_ref[...] = jnp.zeros_like(acc_ref)

    acc_ref[...] += _swiglu_chunk(hn_ref[...], wg_ref, wu_ref, w2_ref)

    @pl.when(j == pl.num_programs(1) - 1)
    def _():
        out_ref[...] = x_ref[...] + acc_ref[...]


def _dense_ffn(x, g, w13, w2, tm, tf):
    T, D = x.shape
    dff = w2.shape[0]
    nj = dff // tf
    return pl.pallas_call(
        _dense_ffn_kernel,
        grid=(T // tm, nj),
        in_specs=[pl.BlockSpec((tm, D), lambda i, j: (i, 0)),
                  pl.BlockSpec((1, D), lambda i, j: (0, 0)),
                  pl.BlockSpec((D, tf), lambda i, j: (0, j)),
                  pl.BlockSpec((D, tf), lambda i, j: (0, j + nj)),
                  pl.BlockSpec((tf, D), lambda i, j: (j, 0))],
        out_specs=pl.BlockSpec((tm, D), lambda i, j: (i, 0)),
        out_shape=jax.ShapeDtypeStruct((T, D), F32),
        scratch_shapes=[pltpu.VMEM((tm, D), BF16), pltpu.VMEM((tm, D), F32)],
        compiler_params=_cparams("parallel", "arbitrary"),
        name="dense_ffn",
    )(x, g, w13, w13, w2)


def _store_slab(ref, v):
    n = v.shape[0]
    for s in range(SLAB_ROWS):
        ref[pl.ds(s, n, stride=SLAB_ROWS), :] = v[:, s * LANES:(s + 1) * LANES]


def _load_slab(ref, first, n):
    return jnp.concatenate(
        [ref[pl.ds(first * SLAB_ROWS + s, n, stride=SLAB_ROWS), :] for s in range(SLAB_ROWS)],
        axis=1)


def _slab(row):
    return pl.ds(pl.multiple_of(row * SLAB_ROWS, SLAB_ROWS), SLAB_ROWS)


def _router_kernel(x_ref, g_ref, wr_ref, hn_ref, route_ref):
    hn = _rms(x_ref[...], g_ref[...])
    _store_slab(hn_ref, hn)
    logits = jnp.dot(hn.astype(BF16), wr_ref[...], preferred_element_type=F32)
    lane = lax.broadcasted_iota(jnp.int32, logits.shape, 1).astype(F32)
    logits = jnp.where(lane < N_EXPERTS, logits, -jnp.inf)
    m1 = jnp.max(logits, axis=-1, keepdims=True)
    i1 = jnp.min(jnp.where(logits == m1, lane, float(LANES)), axis=-1, keepdims=True)
    rest = jnp.where(lane == i1, -jnp.inf, logits)
    m2 = jnp.max(rest, axis=-1, keepdims=True)
    i2 = jnp.min(jnp.where(rest == m2, lane, float(LANES)), axis=-1, keepdims=True)
    e2 = jnp.exp(m2 - m1)
    p1 = 1.0 / (1.0 + e2)
    p2 = e2 * p1
    route_ref[...] = jnp.where(lane == 0, i1, jnp.where(lane == 1, i2,
                     jnp.where(lane == 2, p1, jnp.where(lane == 3, p2, 0.0))))


def _router(x, g, wr, tm):
    T, D = x.shape
    return pl.pallas_call(
        _router_kernel,
        grid=(T // tm,),
        in_specs=[pl.BlockSpec((tm, D), lambda i: (i, 0)),
                  pl.BlockSpec((1, D), lambda i: (0, 0)),
                  pl.BlockSpec((D, LANES), lambda i: (0, 0))],
        out_specs=[pl.BlockSpec((tm * SLAB_ROWS, LANES), lambda i: (i, 0)),
                   pl.BlockSpec((tm, LANES), lambda i: (i, 0))],
        out_shape=[jax.ShapeDtypeStruct((T * SLAB_ROWS, LANES), F32),
                   jax.ShapeDtypeStruct((T, LANES), F32)],
        compiler_params=_cparams("parallel"),
        name="router",
    )(x, g, wr)


def _row_copy(src_ref, src_row, dst_ref, dst_row, sem):
    return pltpu.make_async_copy(src_ref.at[_slab(src_row)], dst_ref.at[_slab(dst_row)], sem)


def _wait_rows(src_ref, dst_ref, n, sem):
    span = pl.ds(0, n * SLAB_ROWS)
    pltpu.make_async_copy(src_ref.at[span], dst_ref.at[span], sem).wait()


def _dispatch_kernel(pos_ref, pad_ref, hn_ref, xs_ref, sem, *, tm):
    i = pl.program_id(0)
    base = i * tm * TOP_K

    def start(r, c):
        for kk in range(TOP_K):
            _row_copy(hn_ref, r, xs_ref, pos_ref[base + r * TOP_K + kk], sem).start()
        return c

    lax.fori_loop(0, tm, start, 0, unroll=DMA_ISSUE_UNROLL)
    _wait_rows(xs_ref, xs_ref, tm * TOP_K, sem)

    @pl.when(i == 0)
    def _():
        npad = pad_ref.shape[0]

        def pstart(r, c):
            _row_copy(hn_ref, 0, xs_ref, pad_ref[r], sem).start()
            return c

        lax.fori_loop(0, npad, pstart, 0, unroll=DMA_ISSUE_UNROLL)
        _wait_rows(xs_ref, xs_ref, npad, sem)


def _dispatch(hn_slab, pos, pad_rows, n_rows, tm):
    T = hn_slab.shape[0] // SLAB_ROWS
    return pl.pallas_call(
        partial(_dispatch_kernel, tm=tm),
        grid_spec=pltpu.PrefetchScalarGridSpec(
            num_scalar_prefetch=2,
            grid=(T // tm,),
            in_specs=[pl.BlockSpec((tm * SLAB_ROWS, LANES), lambda i, *_: (i, 0))],
            out_specs=pl.BlockSpec(memory_space=pl.ANY),
            scratch_shapes=[pltpu.SemaphoreType.DMA(())]),
        out_shape=jax.ShapeDtypeStruct((n_rows * SLAB_ROWS, LANES), F32),
        compiler_params=_cparams("arbitrary"),
        name="dispatch",
    )(pos, pad_rows, hn_slab)


def _expert_ffn_kernel(te_ref, tv_ref, xs_ref, wg_ref, wu_ref, w2_ref, y_ref, hn_ref, acc_ref):
    i, j = pl.program_id(0), pl.program_id(1)
    last = pl.num_programs(1) - 1
    valid = tv_ref[i] > 0

    @pl.when(valid & (j == 0))
    def _():
        hn_ref[...] = _load_slab(xs_ref, 0, hn_ref.shape[0]).astype(BF16)
        acc_ref[...] = jnp.zeros_like(acc_ref)

    @pl.when(valid)
    def _():
        acc_ref[...] += _swiglu_chunk(hn_ref[...], wg_ref.at[0], wu_ref.at[0], w2_ref.at[0])

    @pl.when(valid & (j == last))
    def _():
        _store_slab(y_ref, acc_ref[...])

    @pl.when(jnp.logical_not(valid) & (j == last))
    def _():
        y_ref[...] = jnp.zeros_like(y_ref)


def _expert_ffn(xs, tile_expert, tile_valid, w13, w2, tm, tf):
    n_rows = xs.shape[0] // SLAB_ROWS
    D = D_MODEL
    dff = w2.shape[1]
    nj = dff // tf
    jj = lambda i, j, tv: jnp.where(tv[i] > 0, j, nj - 1)
    return pl.pallas_call(
        _expert_ffn_kernel,
        grid_spec=pltpu.PrefetchScalarGridSpec(
            num_scalar_prefetch=2,
            grid=(n_rows // tm, nj),
            in_specs=[pl.BlockSpec((tm * SLAB_ROWS, LANES), lambda i, j, te, tv: (i, 0)),
                      pl.BlockSpec((1, D, tf), lambda i, j, te, tv: (te[i], 0, jj(i, j, tv))),
                      pl.BlockSpec((1, D, tf), lambda i, j, te, tv: (te[i], 0, jj(i, j, tv) + nj)),
                      pl.BlockSpec((1, tf, D), lambda i, j, te, tv: (te[i], jj(i, j, tv), 0))],
            out_specs=pl.BlockSpec((tm * SLAB_ROWS, LANES), lambda i, j, te, tv: (i, 0)),
            scratch_shapes=[pltpu.VMEM((tm, D), BF16), pltpu.VMEM((tm, D), F32)]),
        out_shape=jax.ShapeDtypeStruct((n_rows * SLAB_ROWS, LANES), F32),
        compiler_params=_cparams("parallel", "arbitrary"),
        name="expert_ffn",
    )(tile_expert, tile_valid, xs, w13, w13, w2)


def _combine_kernel(pos_ref, x_ref, route_ref, y_ref, out_ref, buf, sem, *, tm):
    i = pl.program_id(0)

    def issue(tile, slot):
        base = tile * tm * TOP_K

        def start(r, c):
            for kk in range(TOP_K):
                _row_copy(y_ref, pos_ref[base + r * TOP_K + kk], buf.at[slot], kk * tm + r,
                          sem.at[slot]).start()
            return c

        lax.fori_loop(0, tm, start, 0, unroll=DMA_ISSUE_UNROLL)

    @pl.when(i == 0)
    def _():
        issue(0, 0)

    @pl.when(i + 1 < pl.num_programs(0))
    def _():
        issue(i + 1, (i + 1) % 2)

    slot = i % 2
    _wait_rows(y_ref, buf.at[slot], tm * TOP_K, sem.at[slot])
    route = route_ref[...]
    acc = x_ref[...]
    for kk in range(TOP_K):
        gate = route[:, TOP_K + kk:TOP_K + kk + 1]
        acc = acc + gate * _load_slab(buf.at[slot], kk * tm, tm)
    out_ref[...] = acc


def _combine(x, route, y, pos, tm):
    T, D = x.shape
    return pl.pallas_call(
        partial(_combine_kernel, tm=tm),
        grid_spec=pltpu.PrefetchScalarGridSpec(
            num_scalar_prefetch=1,
            grid=(T // tm,),
            in_specs=[pl.BlockSpec((tm, D), lambda i, *_: (i, 0)),
                      pl.BlockSpec((tm, LANES), lambda i, *_: (i, 0)),
                      pl.BlockSpec(memory_space=pl.ANY)],
            out_specs=pl.BlockSpec((tm, D), lambda i, *_: (i, 0)),
            scratch_shapes=[pltpu.VMEM((2, tm * TOP_K * SLAB_ROWS, LANES), F32),
                            pltpu.SemaphoreType.DMA((2,))]),
        out_shape=jax.ShapeDtypeStruct((T, D), F32),
        compiler_params=_cparams("arbitrary"),
        name="combine",
    )(pos, x, route, y)


def _routing_tables(expert_ids, tm, n_tiles):
    e_flat = expert_ids.reshape(-1)
    onehot = (e_flat[:, None] == jnp.arange(N_EXPERTS, dtype=jnp.int32)[None, :]).astype(jnp.int32)
    csum = jnp.cumsum(onehot, axis=0)
    counts = csum[-1]
    rank = jnp.sum(onehot * (csum - 1), axis=1)
    padded = ((counts + tm - 1) // tm) * tm
    ends = jnp.cumsum(padded)
    starts = ends - padded
    pos = (jnp.sum(onehot * starts[None, :], axis=1) + rank).astype(jnp.int32)

    tile_start = jnp.arange(n_tiles, dtype=jnp.int32) * tm
    tile_valid = (tile_start < ends[-1]).astype(jnp.int32)
    te = jnp.sum((tile_start[:, None] >= ends[None, :]).astype(jnp.int32), axis=1)
    last_e = jnp.sum((ends[-1] - 1 >= ends).astype(jnp.int32))
    tile_expert = jnp.where(tile_valid > 0, te, last_e).astype(jnp.int32)

    npad_e = padded - counts
    pcum = jnp.cumsum(npad_e)
    k = jnp.arange(N_EXPERTS * tm, dtype=jnp.int32)
    ek = jnp.sum((k[:, None] >= pcum[None, :]).astype(jnp.int32), axis=1)
    ohk = (ek[:, None] == jnp.arange(N_EXPERTS, dtype=jnp.int32)[None, :]).astype(jnp.int32)
    in_group = jnp.sum(ohk * (starts + counts - (pcum - npad_e))[None, :], axis=1) + k
    pad = jnp.where(ek < N_EXPERTS, in_group, ends[-1] + k - pcum[-1]).astype(jnp.int32)
    return pos, tile_expert, tile_valid, pad


def _moe_ffn(x, g, wr, w13, w2, tm, tme, tf):
    T, D = x.shape
    n_tiles = (T * TOP_K) // tme + N_EXPERTS
    hn_slab, route = _router(x, g, wr, tm)
    expert_ids = route[:, :TOP_K].astype(jnp.int32)
    pos, tile_expert, tile_valid, pad = _routing_tables(expert_ids, tme, n_tiles)
    xs = _dispatch(hn_slab, pos, pad, n_tiles * tme, tm)
    y = _expert_ffn(xs, tile_expert, tile_valid, w13, w2, tme, tf)
    return _combine(x, route, y, pos, tm)


def _final_norm_kernel(x_ref, g_ref, o_ref):
    o_ref[...] = _rms(x_ref[...], g_ref[...])


def _final_norm(x, g, tm):
    T, D = x.shape
    return pl.pallas_call(
        _final_norm_kernel,
        grid=(T // tm,),
        in_specs=[pl.BlockSpec((tm, D), lambda i: (i, 0)), pl.BlockSpec((1, D), lambda i: (0, 0))],
        out_specs=pl.BlockSpec((tm, D), lambda i: (i, 0)),
        out_shape=jax.ShapeDtypeStruct((T, D), F32),
        compiler_params=_cparams("parallel"),
        name="final_norm",
    )(x, g)


def _block_diag(pw):
    G, C, _ = pw.shape
    eye = jnp.eye(G, dtype=pw.dtype)
    return (eye[:, None, :, None] * pw[:, :, None, :]).reshape(G * C, G * C)


def kernel(x, mem, rel_bias, mem_norm_g, mix_norm_g, w_in, conv_w, pool_w, pool_scale, w_out,
           xattn_norm_g, wq_x, wkv_x, wo_x, ffn_norm_g, w13_dense, w2_dense, router_w, w13_moe,
           w2_moe, final_norm_g):
    B, S, D = x.shape
    depth = w_in.shape[0]
    M = mem.shape[1]
    T = B * S
    tm = min(ROW_TILE, S)
    assert D == D_MODEL and S % (max(DILATIONS) * Q_BLOCK) == 0 and S % tm == 0

    xf = x.reshape(T, D)
    row2 = lambda v: v.reshape(1, -1)
    q_scale = jnp.where(jnp.arange(w_in.shape[-1]) < ATTN_WIDTH, ATTN_HEAD_DIM ** -0.5, 1.0)
    w_in_b = (w_in * q_scale).astype(BF16)
    biases = [_band_bias(rel_bias, d) for d in DILATIONS]
    wr_pad = jnp.pad(router_w, ((0, 0), (0, 0), (0, LANES - N_EXPERTS))).astype(BF16)

    mem_f = mem.reshape(B * M, D)
    for layer in range(depth):
        *qkvs, rest = _in_proj(xf, row2(mix_norm_g[layer]), w_in_b[layer], B, S, tm)
        branch = []
        for qkv, bias, d in zip(qkvs, biases, DILATIONS):
            o, lse = _window_attn(qkv.reshape(B * d, S // d, qkv.shape[-1]), bias, ATTN_Q_BLOCKS)
            branch.append((o.reshape(B, d, S // d, -1), lse.reshape(B, d, S // d, -1)))
        xf = _mixer_out(xf, [o for o, _ in branch], [l for _, l in branch], rest,
                        conv_w[layer], _block_diag(pool_w[layer]).astype(BF16),
                        row2(pool_scale[layer]), w_out[layer].astype(BF16), S, tm)

        kv = _norm_proj(mem_f, row2(mem_norm_g), wkv_x[layer].astype(BF16), BF16, min(tm, B * M))
        xf = _xattn(xf, row2(xattn_norm_g[layer]), wq_x[layer].astype(BF16),
                    kv.reshape(B, M, 2 * XATTN_WIDTH), wo_x[layer].astype(BF16), S, tm)

        g = row2(ffn_norm_g[layer])
        if layer % 2 == 0:
            w13, w2 = w13_dense[layer // 2].astype(BF16), w2_dense[layer // 2].astype(BF16)
            xf = _dense_ffn(xf, g, w13, w2, tm, w2.shape[0] // 2)
        else:
            w13, w2 = w13_moe[layer // 2].astype(BF16), w2_moe[layer // 2].astype(BF16)
            xf = _moe_ffn(xf, g, wr_pad[layer // 2], w13, w2, tm, EXPERT_ROW_TILE, 512)
    return _final_norm(xf, row2(final_norm_g), tm).reshape(B, S, D)
```

```python
from functools import partial

import numpy as np
import jax
import jax.numpy as jnp
from jax import lax
from jax.experimental import pallas as pl
from jax.experimental.pallas import tpu as pltpu

F32 = jnp.float32
BF16 = jnp.bfloat16

D_MODEL = 1024
ATTN_HEADS = 8
ATTN_HEAD_DIM = 64
ATTN_WIDTH = ATTN_HEADS * ATTN_HEAD_DIM
POOL_WIDTH = 256
POOL_GROUP_DIM = 64
POOL_WINDOWS = (2, 4, 8, 16)
POOL_HALO = 16
CONV_WIDTH = 256
REST_WIDTH = POOL_WIDTH + 3 * CONV_WIDTH
DILATIONS = (1, 4, 16)
WINDOW_STEPS = 128
Q_BLOCK = 128
NEG_INF = -1e30
REL_BUCKETS = 32
REL_MAX_DIST = 128
XATTN_HEADS = 4
XATTN_HEAD_DIM = 128
XATTN_WIDTH = XATTN_HEADS * XATTN_HEAD_DIM
N_EXPERTS = 8
TOP_K = 2
EPS = 1e-6

LANES = 128
SLAB_ROWS = D_MODEL // LANES
ROW_TILE = 512
EXPERT_ROW_TILE = 512
EXPERT_FF_TILE = 1792
DENSE_FF_TILE = 1408
LOG2_E = 1.4426950408889634
LN_2 = 0.6931471805599453
ATTN_Q_BLOCKS = 4
DMA_ISSUE_UNROLL = 4
VMEM_LIMIT = 56 * 1024 * 1024


def _cparams(*sem):
    return pltpu.CompilerParams(dimension_semantics=sem, vmem_limit_bytes=VMEM_LIMIT)


def _rms(x, g):
    return x * lax.rsqrt(jnp.mean(x * x, axis=-1, keepdims=True) + EPS) * g


def _norm_proj_kernel(x_ref, g_ref, w_ref, o_ref):
    xn = _rms(x_ref[...], g_ref[...]).astype(BF16)
    o_ref[...] = jnp.dot(xn, w_ref[...], preferred_element_type=F32).astype(o_ref.dtype)


def _layer_spec(w, layer):
    return pl.BlockSpec((None,) + w.shape[1:], lambda *_: (layer, 0, 0))


def _norm_proj(x, g, w, layer, out_dtype, tm):
    T, D = x.shape
    N = w.shape[-1]
    return pl.pallas_call(
        _norm_proj_kernel,
        grid=(T // tm,),
        in_specs=[pl.BlockSpec((tm, D), lambda i: (i, 0)),
                  pl.BlockSpec((1, D), lambda i: (0, 0)),
                  _layer_spec(w, layer)],
        out_specs=pl.BlockSpec((tm, N), lambda i: (i, 0)),
        out_shape=jax.ShapeDtypeStruct((T, N), out_dtype),
        compiler_params=_cparams("parallel"),
        name="norm_proj",
    )(x, g, w)


def _in_proj_kernel(x_ref, g_ref, w_ref, *refs):
    qkv_refs, rest_ref, acc_ref = refs[:-2], refs[-2], refs[-1]
    nchunk, tm, _ = acc_ref.shape
    nq = nchunk * LANES
    xn = _rms(x_ref[...], g_ref[...]).astype(BF16)
    acc = jnp.dot(xn, w_ref[:, :nq], preferred_element_type=F32)
    qkv_refs[0][0, 0] = acc.astype(BF16)
    for c in range(nchunk):
        acc_ref[c] = acc[:, c * LANES:(c + 1) * LANES]
    for d, qkv_ref in zip(DILATIONS[1:], qkv_refs[1:]):
        for r in range(d):
            rows = [acc_ref[c, pl.ds(r, tm // d, stride=d), :] for c in range(nchunk)]
            qkv_ref[0, r] = jnp.concatenate(rows, axis=1).astype(BF16)
    rest_ref[...] = jnp.dot(xn, w_ref[:, nq:], preferred_element_type=F32)


def _in_proj(x, g, w, layer, B, S, tm):
    T, D = x.shape
    N = w.shape[-1]
    nq = 3 * ATTN_WIDTH
    tps = S // tm
    res_spec = lambda d: pl.BlockSpec((1, d, tm // d, nq), lambda i: (i // tps, 0, i % tps, 0))
    return pl.pallas_call(
        _in_proj_kernel,
        grid=(T // tm,),
        in_specs=[pl.BlockSpec((tm, D), lambda i: (i, 0)),
                  pl.BlockSpec((1, D), lambda i: (0, 0)),
                  _layer_spec(w, layer)],
        out_specs=[res_spec(d) for d in DILATIONS]
                  + [pl.BlockSpec((tm, N - nq), lambda i: (i, 0))],
        out_shape=[jax.ShapeDtypeStruct((B, d, S // d, nq), BF16) for d in DILATIONS]
                  + [jax.ShapeDtypeStruct((T, N - nq), F32)],
        scratch_shapes=[pltpu.VMEM((nq // LANES, tm, LANES), F32)],
        compiler_params=_cparams("parallel"),
        name="in_proj",
    )(x, g, w)


def _t5_causal_bucket(dist):
    max_exact = REL_BUCKETS // 2
    d = np.maximum(dist, 1).astype(np.float32)
    large = max_exact + (np.log(d / max_exact) / np.log(REL_MAX_DIST / max_exact)
                         * (REL_BUCKETS - max_exact)).astype(np.int32)
    large = np.minimum(large, REL_BUCKETS - 1)
    return np.where(dist < max_exact, dist, large).astype(np.int32)


def _band_bias(rel_bias, dil):
    row = np.arange(Q_BLOCK)[:, None]
    col = np.arange(2 * Q_BLOCK)[None, :]
    steps = Q_BLOCK + row - col
    valid = (steps >= 0) & (steps <= WINDOW_STEPS)
    bucket = _t5_causal_bucket(dil * np.clip(steps, 0, WINDOW_STEPS))
    onehot = (bucket[..., None] == np.arange(REL_BUCKETS)).astype(np.float32)
    b = jnp.einsum('qcb,bh->hqc', jnp.asarray(onehot), rel_bias.astype(F32),
                   precision=lax.Precision.HIGHEST)
    masks = np.stack([valid, valid & (col >= Q_BLOCK)])
    return jnp.where(jnp.asarray(masks)[:, None], b[None], NEG_INF)


def _window_attn_kernel(q_ref, kp_ref, k_ref, vp_ref, v_ref, bias_ref, o_ref, lse_ref, *, nq):
    first_step = jnp.where(pl.program_id(1) == 0, 1, 0)
    lane = lax.broadcasted_iota(jnp.int32, (Q_BLOCK, LANES), 1)
    low_half = lane < ATTN_HEAD_DIM
    Q = Q_BLOCK
    for j in range(nq):
        rows = slice(j * Q, (j + 1) * Q)
        if j == 0:
            k2 = jnp.concatenate([kp_ref[0], k_ref[0, :Q, :]], axis=0)
            v2 = jnp.concatenate([vp_ref[0], v_ref[0, :Q, :]], axis=0)
            table = first_step
        else:
            k2 = k_ref[0, (j - 1) * Q:(j + 1) * Q, :]
            v2 = v_ref[0, (j - 1) * Q:(j + 1) * Q, :]
            table = 0
        lse = jnp.zeros((Q, LANES), F32)
        for g in range(ATTN_WIDTH // LANES):
            cols = slice(g * LANES, (g + 1) * LANES)
            qg, kg, vg = q_ref[0, rows, cols], k2[:, cols], v2[:, cols]
            og = None
            for half in range(2):
                h = 2 * g + half
                keep = low_half if half == 0 else jnp.logical_not(low_half)
                qh = jnp.where(keep, qg, jnp.zeros_like(qg))
                s = lax.dot_general(qh, kg, (((1,), (1,)), ((), ())),
                                    preferred_element_type=F32)
                s = s + bias_ref[table, h]
                m = jnp.max(s, axis=-1, keepdims=True)
                p = jnp.exp2(s - m)
                l = jnp.sum(p, axis=-1, keepdims=True)
                o = jnp.dot(p.astype(BF16), vg, preferred_element_type=F32) * (1.0 / l)
                og = o if half == 0 else jnp.where(low_half, og, o)
                lse = jnp.where(lane == h, (m + jnp.log2(l)) * LN_2, lse)
            o_ref[0, rows, cols] = og.astype(o_ref.dtype)
        lse_ref[0, rows, :] = lse


def _window_attn(qkv, bias, nq):
    N, L, _ = qkv.shape
    W = ATTN_WIDTH
    nq = min(nq, L // Q_BLOCK)
    step = nq * Q_BLOCK
    prev = lambda i: jnp.maximum(i * nq - 1, 0)
    return pl.pallas_call(
        partial(_window_attn_kernel, nq=nq),
        grid=(N, L // step),
        in_specs=[pl.BlockSpec((1, step, W), lambda n, i: (n, i, 0)),
                  pl.BlockSpec((1, Q_BLOCK, W), lambda n, i: (n, prev(i), 1)),
                  pl.BlockSpec((1, step, W), lambda n, i: (n, i, 1)),
                  pl.BlockSpec((1, Q_BLOCK, W), lambda n, i: (n, prev(i), 2)),
                  pl.BlockSpec((1, step, W), lambda n, i: (n, i, 2)),
                  pl.BlockSpec(bias.shape, lambda n, i: (0, 0, 0, 0))],
        out_specs=[pl.BlockSpec((1, step, W), lambda n, i: (n, i, 0)),
                   pl.BlockSpec((1, step, LANES), lambda n, i: (n, i, 0))],
        out_shape=[jax.ShapeDtypeStruct((N, L, W), BF16),
                   jax.ShapeDtypeStruct((N, L, LANES), F32)],
        compiler_params=_cparams("parallel", "parallel"),
        name="window_attn",
    )(qkv, qkv, qkv, qkv, qkv, bias)


def _token_order(src_ref, scratch_ref, d):
    if d == 1:
        return src_ref[0, 0].astype(F32)
    nchunk, n, _ = scratch_ref.shape
    for r in range(d):
        part = src_ref[0, r].astype(F32)
        for c in range(nchunk):
            scratch_ref[c, pl.ds(r, n // d, stride=d), :] = part[:, c * LANES:(c + 1) * LANES]
    return jnp.concatenate([scratch_ref[c] for c in range(nchunk)], axis=1)


def _mixer_out_kernel(x_ref, o1_ref, o2_ref, o3_ref, l1_ref, l2_ref, l3_ref, rest_ref, halo_ref,
                      convw_ref, poolw_ref, pscale_ref, pwin_ref, expand_ref, wout_ref, out_ref,
                      os2_ref, os3_ref, ls2_ref, ls3_ref, *, tm, tiles_per_seq):
    it = pl.program_id(0) % tiles_per_seq
    lses = [_token_order(r, s, d)
            for r, s, d in zip((l1_ref, l2_ref, l3_ref), (None, ls2_ref, ls3_ref), DILATIONS)]
    outs = [_token_order(r, s, d)
            for r, s, d in zip((o1_ref, o2_ref, o3_ref), (None, os2_ref, os3_ref), DILATIONS)]
    lm = jnp.maximum(jnp.maximum(lses[0], lses[1]), lses[2])
    es = [jnp.exp(l - lm) for l in lses]
    inv = 1.0 / (es[0] + es[1] + es[2])
    y_attn = jnp.zeros((tm, ATTN_WIDTH), F32)
    for e, o in zip(es, outs):
        w = e * inv
        w_hi = w.astype(BF16)
        w_lo = (w - w_hi.astype(F32)).astype(BF16)
        wide = (jnp.dot(w_hi, expand_ref[...], preferred_element_type=F32)
                + jnp.dot(w_lo, expand_ref[...], preferred_element_type=F32))
        y_attn = y_attn + wide * o
    y_attn = y_attn.astype(BF16)

    rest = rest_ref[...]
    halo = jnp.where(it == 0, 0.0, halo_ref[...])
    ext = jnp.concatenate([halo, rest], axis=0)

    u = ext[:, :POOL_WIDTH]
    s2 = u + pltpu.roll(u, 1, 0)
    s4 = s2 + pltpu.roll(s2, 2, 0)
    s8 = s4 + pltpu.roll(s4, 4, 0)
    s16 = s8 + pltpu.roll(s8, 8, 0)
    col = lax.broadcasted_iota(jnp.int32, u.shape, 1)
    g = POOL_GROUP_DIM
    wsum = jnp.where(col < g, s2, jnp.where(col < 2 * g, s4, jnp.where(col < 3 * g, s8, s16)))
    pos = it * tm + lax.broadcasted_iota(jnp.int32, (tm, POOL_WIDTH), 0)
    count = jnp.minimum((pos + 1).astype(F32), pwin_ref[...])
    pooled = wsum[POOL_HALO:] / count - u[POOL_HALO:]
    y_pool = jnp.dot(pooled.astype(BF16), poolw_ref[...], preferred_element_type=F32)
    y_pool = (y_pool * pscale_ref[...]).astype(BF16)

    b_gate = rest[:, POOL_WIDTH:POOL_WIDTH + CONV_WIDTH]
    cu = ext[:, POOL_WIDTH + CONV_WIDTH:POOL_WIDTH + 2 * CONV_WIDTH] * ext[:, POOL_WIDTH + 2 * CONV_WIDTH:]
    y = (convw_ref[0:1, :] * cu + convw_ref[1:2, :] * pltpu.roll(cu, 1, 0)
         + convw_ref[2:3, :] * pltpu.roll(cu, 2, 0))
    y_conv = (b_gate * y[POOL_HALO:]).astype(BF16)

    a0, a1 = ATTN_WIDTH, ATTN_WIDTH + POOL_WIDTH
    acc = jnp.dot(y_attn, wout_ref[:a0, :], preferred_element_type=F32)
    acc += jnp.dot(y_pool, wout_ref[a0:a1, :], preferred_element_type=F32)
    acc += jnp.dot(y_conv, wout_ref[a1:, :], preferred_element_type=F32)
    out_ref[...] = x_ref[...] + acc


def _mixer_out(x, outs, lses, rest, conv_w, pool_w_bd, pool_scale, w_out, layer, S, tm):
    T, D = x.shape
    hb = tm // POOL_HALO
    tps = S // tm
    row = lambda i: (i, 0)
    const = lambda i: (0, 0)
    res_spec = lambda d, c: pl.BlockSpec((1, d, tm // d, c), lambda i: (i // tps, 0, i % tps, 0))
    pwin = jnp.asarray(np.repeat(np.array(POOL_WINDOWS, np.float32), POOL_GROUP_DIM)[None, :])
    expand = np.zeros((LANES, ATTN_WIDTH), np.float32)
    for h in range(ATTN_HEADS):
        expand[h, h * ATTN_HEAD_DIM:(h + 1) * ATTN_HEAD_DIM] = 1.0
    expand = jnp.asarray(expand, dtype=BF16)
    return pl.pallas_call(
        partial(_mixer_out_kernel, tm=tm, tiles_per_seq=tps),
        grid=(T // tm,),
        scratch_shapes=[pltpu.VMEM((ATTN_WIDTH // LANES, tm, LANES), F32)] * 2
                       + [pltpu.VMEM((1, tm, LANES), F32)] * 2,
        name="mixer_out",
        in_specs=[pl.BlockSpec((tm, D), row)]
                 + [res_spec(d, ATTN_WIDTH) for d in DILATIONS]
                 + [res_spec(d, LANES) for d in DILATIONS]
                 + [pl.BlockSpec((tm, REST_WIDTH), row),
                    pl.BlockSpec((POOL_HALO, REST_WIDTH), lambda i: (jnp.maximum(i * hb - 1, 0), 0)),
                    pl.BlockSpec(conv_w.shape, const),
                    pl.BlockSpec(pool_w_bd.shape, const),
                    pl.BlockSpec((1, POOL_WIDTH), const),
                    pl.BlockSpec((1, POOL_WIDTH), const),
                    pl.BlockSpec(expand.shape, const),
                    _layer_spec(w_out, layer)],
        out_specs=pl.BlockSpec((tm, D), row),
        out_shape=jax.ShapeDtypeStruct((T, D), F32),
        compiler_params=_cparams("parallel"),
    )(x, *outs, *lses, rest, rest, conv_w, pool_w_bd, pool_scale, pwin, expand, w_out)


def _xattn_kernel(x_ref, g_ref, wq_ref, kv_ref, wo_ref, out_ref):
    x = x_ref[...]
    xn = _rms(x, g_ref[...]).astype(BF16)
    q = jnp.dot(xn, wq_ref[...], preferred_element_type=F32).astype(BF16)
    kv = kv_ref[0]
    scale = XATTN_HEAD_DIM ** -0.5
    heads = []
    for h in range(XATTN_HEADS):
        sl = slice(h * XATTN_HEAD_DIM, (h + 1) * XATTN_HEAD_DIM)
        k = kv[:, sl]
        v = kv[:, XATTN_WIDTH + h * XATTN_HEAD_DIM:XATTN_WIDTH + (h + 1) * XATTN_HEAD_DIM]
        s = lax.dot_general(q[:, sl], k, (((1,), (1,)), ((), ())),
                            preferred_element_type=F32) * scale
        m = jnp.max(s, axis=-1, keepdims=True)
        p = jnp.exp(s - m)
        l = jnp.sum(p, axis=-1, keepdims=True)
        heads.append(jnp.dot(p.astype(BF16), v, preferred_element_type=F32) / l)
    o = jnp.concatenate(heads, axis=1).astype(BF16)
    out_ref[...] = x + jnp.dot(o, wo_ref[...], preferred_element_type=F32)


def _xattn(x, g, wq, kv, wo, layer, S, tm):
    T, D = x.shape
    M = kv.shape[1]
    tiles_per_seq = S // tm
    return pl.pallas_call(
        _xattn_kernel,
        grid=(T // tm,),
        in_specs=[pl.BlockSpec((tm, D), lambda i: (i, 0)),
                  pl.BlockSpec((1, D), lambda i: (0, 0)),
                  _layer_spec(wq, layer),
                  pl.BlockSpec((1, M, 2 * XATTN_WIDTH), lambda i: (i // tiles_per_seq, 0, 0)),
                  _layer_spec(wo, layer)],
        out_specs=pl.BlockSpec((tm, D), lambda i: (i, 0)),
        out_shape=jax.ShapeDtypeStruct((T, D), F32),
        compiler_params=_cparams("parallel"),
        name="xattn",
    )(x, g, wq, kv, wo)


def _swiglu_chunk(hn, wg_ref, wu_ref, w2_ref):
    a = jnp.dot(hn, wg_ref[...], preferred_element_type=F32)
    b = jnp.dot(hn, wu_ref[...], preferred_element_type=F32)
    h = (a * jax.nn.sigmoid(a) * b).astype(BF16)
    return jnp.dot(h, w2_ref[...], preferred_element_type=F32)


def _dense_ffn_kernel(x_ref, g_ref, wg_ref, wu_ref, w2_ref, out_ref, hn_ref, acc_ref):
    j = pl.program_id(1)

    @pl.when(j == 0)
    def _():
        hn_ref[...] = _rms(x_ref[...], g_ref[...]).astype(BF16)
        acc_ref[...] = jnp.zeros_like(acc_ref)

    acc_ref[...] += _swiglu_chunk(hn_ref[...], wg_ref, wu_ref, w2_ref)

    @pl.when(j == pl.num_programs(1) - 1)
    def _():
        out_ref[...] = x_ref[...] + acc_ref[...]


def _dense_ffn(x, g, w13, w2, layer, tm, tf):
    T, D = x.shape
    dff = w2.shape[1]
    nj = dff // tf
    return pl.pallas_call(
        _dense_ffn_kernel,
        grid=(T // tm, nj),
        in_specs=[pl.BlockSpec((tm, D), lambda i, j: (i, 0)),
                  pl.BlockSpec((1, D), lambda i, j: (0, 0)),
                  pl.BlockSpec((None, D, tf), lambda i, j: (layer, 0, j)),
                  pl.BlockSpec((None, D, tf), lambda i, j: (layer, 0, j + nj)),
                  pl.BlockSpec((None, tf, D), lambda i, j: (layer, j, 0))],
        out_specs=pl.BlockSpec((tm, D), lambda i, j: (i, 0)),
        out_shape=jax.ShapeDtypeStruct((T, D), F32),
        scratch_shapes=[pltpu.VMEM((tm, D), BF16), pltpu.VMEM((tm, D), F32)],
        compiler_params=_cparams("parallel", "arbitrary"),
        name="dense_ffn",
    )(x, g, w13, w13, w2)


def _store_slab(ref, v):
    n = v.shape[0]
    for s in range(SLAB_ROWS):
        ref[pl.ds(s, n, stride=SLAB_ROWS), :] = v[:, s * LANES:(s + 1) * LANES]


def _load_slab(ref, first, n):
    return jnp.concatenate(
        [ref[pl.ds(first * SLAB_ROWS + s, n, stride=SLAB_ROWS), :] for s in range(SLAB_ROWS)],
        axis=1)


def _slab(row):
    return pl.ds(pl.multiple_of(row * SLAB_ROWS, SLAB_ROWS), SLAB_ROWS)


def _router_kernel(x_ref, g_ref, wr_ref, hn_ref, route_ref):
    hn = _rms(x_ref[...], g_ref[...])
    _store_slab(hn_ref, hn)
    logits = jnp.dot(hn.astype(BF16), wr_ref[...], preferred_element_type=F32)
    lane = lax.broadcasted_iota(jnp.int32, logits.shape, 1).astype(F32)
    logits = jnp.where(lane < N_EXPERTS, logits, -jnp.inf)
    m1 = jnp.max(logits, axis=-1, keepdims=True)
    i1 = jnp.min(jnp.where(logits == m1, lane, float(LANES)), axis=-1, keepdims=True)
    rest = jnp.where(lane == i1, -jnp.inf, logits)
    m2 = jnp.max(rest, axis=-1, keepdims=True)
    i2 = jnp.min(jnp.where(rest == m2, lane, float(LANES)), axis=-1, keepdims=True)
    e2 = jnp.exp(m2 - m1)
    p1 = 1.0 / (1.0 + e2)
    p2 = e2 * p1
    route_ref[...] = jnp.where(lane == 0, i1, jnp.where(lane == 1, i2,
                     jnp.where(lane == 2, p1, jnp.where(lane == 3, p2, 0.0))))


def _router(x, g, wr, tm):
    T, D = x.shape
    return pl.pallas_call(
        _router_kernel,
        grid=(T // tm,),
        in_specs=[pl.BlockSpec((tm, D), lambda i: (i, 0)),
                  pl.BlockSpec((1, D), lambda i: (0, 0)),
                  pl.BlockSpec((D, LANES), lambda i: (0, 0))],
        out_specs=[pl.BlockSpec((tm * SLAB_ROWS, LANES), lambda i: (i, 0)),
                   pl.BlockSpec((tm, LANES), lambda i: (i, 0))],
        out_shape=[jax.ShapeDtypeStruct((T * SLAB_ROWS, LANES), F32),
                   jax.ShapeDtypeStruct((T, LANES), F32)],
        compiler_params=_cparams("parallel"),
        name="router",
    )(x, g, wr)


def _row_copy(src_ref, src_row, dst_ref, dst_row, sem):
    return pltpu.make_async_copy(src_ref.at[_slab(src_row)], dst_ref.at[_slab(dst_row)], sem)


def _wait_rows(src_ref, dst_ref, n, sem):
    span = pl.ds(0, n * SLAB_ROWS)
    pltpu.make_async_copy(src_ref.at[span], dst_ref.at[span], sem).wait()


def _dispatch_kernel(pos_ref, pad_ref, hn_ref, xs_ref, sem, *, tm):
    i = pl.program_id(0)
    base = i * tm * TOP_K

    def start(r, c):
        for kk in range(TOP_K):
            _row_copy(hn_ref, r, xs_ref, pos_ref[base + r * TOP_K + kk], sem).start()
        return c

    lax.fori_loop(0, tm, start, 0, unroll=DMA_ISSUE_UNROLL)
    _wait_rows(xs_ref, xs_ref, tm * TOP_K, sem)

    @pl.when(i == 0)
    def _():
        npad = pad_ref.shape[0]

        def pstart(r, c):
            _row_copy(hn_ref, 0, xs_ref, pad_ref[r], sem).start()
            return c

        lax.fori_loop(0, npad, pstart, 0, unroll=DMA_ISSUE_UNROLL)
        _wait_rows(xs_ref, xs_ref, npad, sem)


def _dispatch(hn_slab, pos, pad_rows, n_rows, tm):
    T = hn_slab.shape[0] // SLAB_ROWS
    return pl.pallas_call(
        partial(_dispatch_kernel, tm=tm),
        grid_spec=pltpu.PrefetchScalarGridSpec(
            num_scalar_prefetch=2,
            grid=(T // tm,),
            in_specs=[pl.BlockSpec((tm * SLAB_ROWS, LANES), lambda i, *_: (i, 0))],
            out_specs=pl.BlockSpec(memory_space=pl.ANY),
            scratch_shapes=[pltpu.SemaphoreType.DMA(())]),
        out_shape=jax.ShapeDtypeStruct((n_rows * SLAB_ROWS, LANES), F32),
        compiler_params=_cparams("arbitrary"),
        name="dispatch",
    )(pos, pad_rows, hn_slab)


def _expert_ffn_kernel(te_ref, tv_ref, xs_ref, wg_ref, wu_ref, w2_ref, y_ref, hn_ref, acc_ref):
    i, j = pl.program_id(0), pl.program_id(1)
    last = pl.num_programs(1) - 1
    valid = tv_ref[i] > 0

    @pl.when(valid & (j == 0))
    def _():
        hn_ref[...] = _load_slab(xs_ref, 0, hn_ref.shape[0]).astype(BF16)
        acc_ref[...] = jnp.zeros_like(acc_ref)

    @pl.when(valid)
    def _():
        acc_ref[...] += _swiglu_chunk(hn_ref[...], wg_ref, wu_ref, w2_ref)

    @pl.when(valid & (j == last))
    def _():
        _store_slab(y_ref, acc_ref[...])

    @pl.when(jnp.logical_not(valid) & (j == last))
    def _():
        y_ref[...] = jnp.zeros_like(y_ref)


def _expert_ffn(xs, tile_expert, tile_valid, w13, w2, layer, tm, tf):
    n_rows = xs.shape[0] // SLAB_ROWS
    D = D_MODEL
    dff = w2.shape[2]
    nj = dff // tf
    jj = lambda i, j, tv: jnp.where(tv[i] > 0, j, nj - 1)
    return pl.pallas_call(
        _expert_ffn_kernel,
        grid_spec=pltpu.PrefetchScalarGridSpec(
            num_scalar_prefetch=2,
            grid=(n_rows // tm, nj),
            in_specs=[pl.BlockSpec((tm * SLAB_ROWS, LANES), lambda i, j, te, tv: (i, 0)),
                      pl.BlockSpec((None, None, D, tf),
                                   lambda i, j, te, tv: (layer, te[i], 0, jj(i, j, tv))),
                      pl.BlockSpec((None, None, D, tf),
                                   lambda i, j, te, tv: (layer, te[i], 0, jj(i, j, tv) + nj)),
                      pl.BlockSpec((None, None, tf, D),
                                   lambda i, j, te, tv: (layer, te[i], jj(i, j, tv), 0))],
            out_specs=pl.BlockSpec((tm * SLAB_ROWS, LANES), lambda i, j, te, tv: (i, 0)),
            scratch_shapes=[pltpu.VMEM((tm, D), BF16), pltpu.VMEM((tm, D), F32)]),
        out_shape=jax.ShapeDtypeStruct((n_rows * SLAB_ROWS, LANES), F32),
        compiler_params=_cparams("parallel", "arbitrary"),
        name="expert_ffn",
    )(tile_expert, tile_valid, xs, w13, w13, w2)


def _combine_kernel(pos_ref, x_ref, route_ref, y_ref, out_ref, buf, sem, *, tm):
    i = pl.program_id(0)

    def issue(tile, slot):
        base = tile * tm * TOP_K

        def start(r, c):
            for kk in range(TOP_K):
                _row_copy(y_ref, pos_ref[base + r * TOP_K + kk], buf.at[slot], kk * tm + r,
                          sem.at[slot]).start()
            return c

        lax.fori_loop(0, tm, start, 0, unroll=DMA_ISSUE_UNROLL)

    @pl.when(i == 0)
    def _():
        issue(0, 0)

    @pl.when(i + 1 < pl.num_programs(0))
    def _():
        issue(i + 1, (i + 1) % 2)

    slot = i % 2
    _wait_rows(y_ref, buf.at[slot], tm * TOP_K, sem.at[slot])
    route = route_ref[...]
    acc = x_ref[...]
    for kk in range(TOP_K):
        gate = route[:, TOP_K + kk:TOP_K + kk + 1]
        acc = acc + gate * _load_slab(buf.at[slot], kk * tm, tm)
    out_ref[...] = acc


def _combine(x, route, y, pos, tm):
    T, D = x.shape
    return pl.pallas_call(
        partial(_combine_kernel, tm=tm),
        grid_spec=pltpu.PrefetchScalarGridSpec(
            num_scalar_prefetch=1,
            grid=(T // tm,),
            in_specs=[pl.BlockSpec((tm, D), lambda i, *_: (i, 0)),
                      pl.BlockSpec((tm, LANES), lambda i, *_: (i, 0)),
                      pl.BlockSpec(memory_space=pl.ANY)],
            out_specs=pl.BlockSpec((tm, D), lambda i, *_: (i, 0)),
            scratch_shapes=[pltpu.VMEM((2, tm * TOP_K * SLAB_ROWS, LANES), F32),
                            pltpu.SemaphoreType.DMA((2,))]),
        out_shape=jax.ShapeDtypeStruct((T, D), F32),
        compiler_params=_cparams("arbitrary"),
        name="combine",
    )(pos, x, route, y)


def _routing_tables(expert_ids, tm, n_tiles):
    e_flat = expert_ids.reshape(-1)
    onehot = (e_flat[:, None] == jnp.arange(N_EXPERTS, dtype=jnp.int32)[None, :]).astype(jnp.int32)
    csum = jnp.cumsum(onehot, axis=0)
    counts = csum[-1]
    rank = jnp.sum(onehot * (csum - 1), axis=1)
    padded = ((counts + tm - 1) // tm) * tm
    ends = jnp.cumsum(padded)
    starts = ends - padded
    pos = (jnp.sum(onehot * starts[None, :], axis=1) + rank).astype(jnp.int32)

    tile_start = jnp.arange(n_tiles, dtype=jnp.int32) * tm
    tile_valid = (tile_start < ends[-1]).astype(jnp.int32)
    te = jnp.sum((tile_start[:, None] >= ends[None, :]).astype(jnp.int32), axis=1)
    last_e = jnp.sum((ends[-1] - 1 >= ends).astype(jnp.int32))
    tile_expert = jnp.where(tile_valid > 0, te, last_e).astype(jnp.int32)

    npad_e = padded - counts
    pcum = jnp.cumsum(npad_e)
    k = jnp.arange(N_EXPERTS * tm, dtype=jnp.int32)
    ek = jnp.sum((k[:, None] >= pcum[None, :]).astype(jnp.int32), axis=1)
    ohk = (ek[:, None] == jnp.arange(N_EXPERTS, dtype=jnp.int32)[None, :]).astype(jnp.int32)
    in_group = jnp.sum(ohk * (starts + counts - (pcum - npad_e))[None, :], axis=1) + k
    pad = jnp.where(ek < N_EXPERTS, in_group, ends[-1] + k - pcum[-1]).astype(jnp.int32)
    return pos, tile_expert, tile_valid, pad


def _moe_ffn(x, g, wr, w13, w2, layer, tm, tme, tf):
    T, D = x.shape
    n_tiles = (T * TOP_K) // tme + N_EXPERTS
    hn_slab, route = _router(x, g, wr, tm)
    expert_ids = route[:, :TOP_K].astype(jnp.int32)
    pos, tile_expert, tile_valid, pad = _routing_tables(expert_ids, tme, n_tiles)
    xs = _dispatch(hn_slab, pos, pad, n_tiles * tme, tm)
    y = _expert_ffn(xs, tile_expert, tile_valid, w13, w2, layer, tme, tf)
    return _combine(x, route, y, pos, tm)


def _final_norm_kernel(x_ref, g_ref, o_ref):
    o_ref[...] = _rms(x_ref[...], g_ref[...])


def _final_norm(x, g, tm):
    T, D = x.shape
    return pl.pallas_call(
        _final_norm_kernel,
        grid=(T // tm,),
        in_specs=[pl.BlockSpec((tm, D), lambda i: (i, 0)), pl.BlockSpec((1, D), lambda i: (0, 0))],
        out_specs=pl.BlockSpec((tm, D), lambda i: (i, 0)),
        out_shape=jax.ShapeDtypeStruct((T, D), F32),
        compiler_params=_cparams("parallel"),
        name="final_norm",
    )(x, g)


def _block_diag(pw):
    G, C, _ = pw.shape
    eye = jnp.eye(G, dtype=pw.dtype)
    return (eye[:, None, :, None] * pw[:, :, None, :]).reshape(G * C, G * C)


def kernel(x, mem, rel_bias, mem_norm_g, mix_norm_g, w_in, conv_w, pool_w, pool_scale, w_out,
           xattn_norm_g, wq_x, wkv_x, wo_x, ffn_norm_g, w13_dense, w2_dense, router_w, w13_moe,
           w2_moe, final_norm_g):
    B, S, D = x.shape
    depth = w_in.shape[0]
    M = mem.shape[1]
    T = B * S
    tm = min(ROW_TILE, S)
    assert D == D_MODEL and S % (max(DILATIONS) * Q_BLOCK) == 0 and S % tm == 0

    xf = x.reshape(T, D)
    row2 = lambda v: v.reshape(1, -1)
    q_scale = jnp.where(jnp.arange(w_in.shape[-1]) < ATTN_WIDTH,
                        ATTN_HEAD_DIM ** -0.5 * LOG2_E, 1.0)
    w_in_b = (w_in * q_scale).astype(BF16)
    biases = [_band_bias(rel_bias * LOG2_E, d) for d in DILATIONS]
    wr_pad = jnp.pad(router_w, ((0, 0), (0, 0), (0, LANES - N_EXPERTS))).astype(BF16)
    w_out_b, wq_b, wkv_b, wo_b = (w.astype(BF16) for w in (w_out, wq_x, wkv_x, wo_x))
    w13_dense_b, w2_dense_b = w13_dense.astype(BF16), w2_dense.astype(BF16)
    w13_moe_b, w2_moe_b = w13_moe.astype(BF16), w2_moe.astype(BF16)

    mem_f = mem.reshape(B * M, D)
    for layer in range(depth):
        *qkvs, rest = _in_proj(xf, row2(mix_norm_g[layer]), w_in_b, layer, B, S, tm)
        branch = []
        for qkv, bias, d in zip(qkvs, biases, DILATIONS):
            o, lse = _window_attn(qkv.reshape(B * d, S // d, qkv.shape[-1]), bias, ATTN_Q_BLOCKS)
            branch.append((o.reshape(B, d, S // d, -1), lse.reshape(B, d, S // d, -1)))
        xf = _mixer_out(xf, [o for o, _ in branch], [l for _, l in branch], rest,
                        conv_w[layer], _block_diag(pool_w[layer]).astype(BF16),
                        row2(pool_scale[layer]), w_out_b, layer, S, tm)

        kv = _norm_proj(mem_f, row2(mem_norm_g), wkv_b, layer, BF16, min(tm, B * M))
        xf = _xattn(xf, row2(xattn_norm_g[layer]), wq_b, kv.reshape(B, M, 2 * XATTN_WIDTH),
                    wo_b, layer, S, tm)

        g = row2(ffn_norm_g[layer])
        if layer % 2 == 0:
            xf = _dense_ffn(xf, g, w13_dense_b, w2_dense_b, layer // 2, tm, DENSE_FF_TILE)
        else:
            xf = _moe_ffn(xf, g, wr_pad[layer // 2], w13_moe_b, w2_moe_b, layer // 2, tm,
                          EXPERT_ROW_TILE, EXPERT_FF_TILE)
    return _final_norm(xf, row2(final_norm_g), tm).reshape(B, S, D)
```

```python
from functools import partial

import numpy as np
import jax
import jax.numpy as jnp
from jax import lax
from jax.experimental import pallas as pl
from jax.experimental.pallas import tpu as pltpu

F32 = jnp.float32
BF16 = jnp.bfloat16

D_MODEL = 1024
ATTN_HEADS = 8
ATTN_HEAD_DIM = 64
ATTN_WIDTH = ATTN_HEADS * ATTN_HEAD_DIM
POOL_WIDTH = 256
POOL_GROUP_DIM = 64
POOL_WINDOWS = (2, 4, 8, 16)
POOL_HALO = 16
CONV_WIDTH = 256
REST_WIDTH = POOL_WIDTH + 3 * CONV_WIDTH
DILATIONS = (1, 4, 16)
WINDOW_STEPS = 128
Q_BLOCK = 128
NEG_INF = -1e30
REL_BUCKETS = 32
REL_MAX_DIST = 128
XATTN_HEADS = 4
XATTN_HEAD_DIM = 128
XATTN_WIDTH = XATTN_HEADS * XATTN_HEAD_DIM
N_EXPERTS = 8
TOP_K = 2
EPS = 1e-6

LANES = 128
MXU_COLS = 256
SLAB_ROWS = D_MODEL // LANES
PACKED_ROWS = SLAB_ROWS // 2
ROW_TILE = 512
EXPERT_ROW_TILE = 512
EXPERT_FF_TILE = 1792
DENSE_FF_TILE = 1408
LOG2_E = 1.4426950408889634
LN_2 = 0.6931471805599453
ATTN_Q_BLOCKS = 4
DMA_ISSUE_UNROLL = 4
VMEM_LIMIT = 56 * 1024 * 1024


def _cparams(*sem):
    return pltpu.CompilerParams(dimension_semantics=sem, vmem_limit_bytes=VMEM_LIMIT)


def _rms(x, g):
    return x * lax.rsqrt(jnp.mean(x * x, axis=-1, keepdims=True) + EPS) * g


def _norm_proj_kernel(x_ref, g_ref, w_ref, o_ref):
    xn = _rms(x_ref[...], g_ref[...]).astype(BF16)
    o_ref[...] = jnp.dot(xn, w_ref[...], preferred_element_type=F32).astype(o_ref.dtype)


def _layer_spec(w, layer):
    return pl.BlockSpec((None,) + w.shape[1:], lambda *_: (layer, 0, 0))


def _norm_proj(x, g, w, layer, out_dtype, tm):
    T, D = x.shape
    N = w.shape[-1]
    return pl.pallas_call(
        _norm_proj_kernel,
        grid=(T // tm,),
        in_specs=[pl.BlockSpec((tm, D), lambda i: (i, 0)),
                  pl.BlockSpec((1, D), lambda i: (0, 0)),
                  _layer_spec(w, layer)],
        out_specs=pl.BlockSpec((tm, N), lambda i: (i, 0)),
        out_shape=jax.ShapeDtypeStruct((T, N), out_dtype),
        compiler_params=_cparams("parallel"),
        name="norm_proj",
    )(x, g, w)


def _in_proj_kernel(x_ref, g_ref, w_ref, *refs):
    qkv_refs, rest_ref, acc_ref = refs[:-2], refs[-2], refs[-1]
    nchunk, tm, _ = acc_ref.shape
    nq = nchunk * LANES
    xn = _rms(x_ref[...], g_ref[...]).astype(BF16)
    per = MXU_COLS // LANES
    for nb in range(nq // MXU_COLS):
        cols = slice(nb * MXU_COLS, (nb + 1) * MXU_COLS)
        acc = jnp.dot(xn, w_ref[:, cols], preferred_element_type=F32)
        for c in range(per):
            acc_ref[nb * per + c] = acc[:, c * LANES:(c + 1) * LANES]
        for d, qkv_ref in zip(DILATIONS, qkv_refs):
            for r in range(d):
                rows = [acc_ref[nb * per + c, pl.ds(r, tm // d, stride=d), :] for c in range(per)]
                qkv_ref[0, r, :, cols] = jnp.concatenate(rows, axis=1).astype(BF16)
    rest_ref[...] = jnp.dot(xn, w_ref[:, nq:], preferred_element_type=F32)


def _in_proj(x, g, w, layer, B, S, tm):
    T, D = x.shape
    N = w.shape[-1]
    nq = 3 * ATTN_WIDTH
    tps = S // tm
    res_spec = lambda d: pl.BlockSpec((1, d, tm // d, nq), lambda i: (i // tps, 0, i % tps, 0))
    return pl.pallas_call(
        _in_proj_kernel,
        grid=(T // tm,),
        in_specs=[pl.BlockSpec((tm, D), lambda i: (i, 0)),
                  pl.BlockSpec((1, D), lambda i: (0, 0)),
                  _layer_spec(w, layer)],
        out_specs=[res_spec(d) for d in DILATIONS]
                  + [pl.BlockSpec((tm, N - nq), lambda i: (i, 0))],
        out_shape=[jax.ShapeDtypeStruct((B, d, S // d, nq), BF16) for d in DILATIONS]
                  + [jax.ShapeDtypeStruct((T, N - nq), F32)],
        scratch_shapes=[pltpu.VMEM((nq // LANES, tm, LANES), F32)],
        compiler_params=_cparams("parallel"),
        name="in_proj",
    )(x, g, w)


def _t5_causal_bucket(dist):
    max_exact = REL_BUCKETS // 2
    d = np.maximum(dist, 1).astype(np.float32)
    large = max_exact + (np.log(d / max_exact) / np.log(REL_MAX_DIST / max_exact)
                         * (REL_BUCKETS - max_exact)).astype(np.int32)
    large = np.minimum(large, REL_BUCKETS - 1)
    return np.where(dist < max_exact, dist, large).astype(np.int32)


def _band_bias(rel_bias, dil):
    row = np.arange(Q_BLOCK)[:, None]
    col = np.arange(2 * Q_BLOCK)[None, :]
    steps = Q_BLOCK + row - col
    valid = (steps >= 0) & (steps <= WINDOW_STEPS)
    bucket = _t5_causal_bucket(dil * np.clip(steps, 0, WINDOW_STEPS))
    onehot = (bucket[..., None] == np.arange(REL_BUCKETS)).astype(np.float32)
    b = jnp.einsum('qcb,bh->hqc', jnp.asarray(onehot), rel_bias.astype(F32),
                   precision=lax.Precision.HIGHEST)
    masks = np.stack([valid, valid & (col >= Q_BLOCK)])
    return jnp.where(jnp.asarray(masks)[:, None], b[None], NEG_INF)


def _window_attn_kernel(q_ref, kp_ref, k_ref, vp_ref, v_ref, bias_ref, o_ref, lse_ref, *, nq):
    first_step = jnp.where(pl.program_id(1) == 0, 1, 0)
    lane = lax.broadcasted_iota(jnp.int32, (Q_BLOCK, LANES), 1)
    low_half = lane < ATTN_HEAD_DIM
    Q = Q_BLOCK
    for j in range(nq):
        rows = slice(j * Q, (j + 1) * Q)
        if j == 0:
            k2 = jnp.concatenate([kp_ref[0], k_ref[0, :Q, :]], axis=0)
            v2 = jnp.concatenate([vp_ref[0], v_ref[0, :Q, :]], axis=0)
            table = first_step
        else:
            k2 = k_ref[0, (j - 1) * Q:(j + 1) * Q, :]
            v2 = v_ref[0, (j - 1) * Q:(j + 1) * Q, :]
            table = 0
        lse = jnp.zeros((Q, LANES), F32)
        for g in range(ATTN_WIDTH // LANES):
            cols = slice(g * LANES, (g + 1) * LANES)
            qg, kg, vg = q_ref[0, rows, cols], k2[:, cols], v2[:, cols]
            og = None
            for half in range(2):
                h = 2 * g + half
                keep = low_half if half == 0 else jnp.logical_not(low_half)
                qh = jnp.where(keep, qg, jnp.zeros_like(qg))
                s = lax.dot_general(qh, kg, (((1,), (1,)), ((), ())),
                                    preferred_element_type=F32)
                s = s + bias_ref[table, h]
                m = jnp.max(s, axis=-1, keepdims=True)
                p = jnp.exp2(s - m)
                l = jnp.sum(p, axis=-1, keepdims=True)
                o = jnp.dot(p.astype(BF16), vg, preferred_element_type=F32) * (1.0 / l)
                og = o if half == 0 else jnp.where(low_half, og, o)
                lse = jnp.where(lane == h, (m + jnp.log2(l)) * LN_2, lse)
            o_ref[0, rows, cols] = og.astype(o_ref.dtype)
        lse_ref[0, rows, :] = lse


def _window_attn(qkv, bias, nq):
    N, L, _ = qkv.shape
    W = ATTN_WIDTH
    nq = min(nq, L // Q_BLOCK)
    step = nq * Q_BLOCK
    prev = lambda i: jnp.maximum(i * nq - 1, 0)
    return pl.pallas_call(
        partial(_window_attn_kernel, nq=nq),
        grid=(N, L // step),
        in_specs=[pl.BlockSpec((1, step, W), lambda n, i: (n, i, 0)),
                  pl.BlockSpec((1, Q_BLOCK, W), lambda n, i: (n, prev(i), 1)),
                  pl.BlockSpec((1, step, W), lambda n, i: (n, i, 1)),
                  pl.BlockSpec((1, Q_BLOCK, W), lambda n, i: (n, prev(i), 2)),
                  pl.BlockSpec((1, step, W), lambda n, i: (n, i, 2)),
                  pl.BlockSpec(bias.shape, lambda n, i: (0, 0, 0, 0))],
        out_specs=[pl.BlockSpec((1, step, W), lambda n, i: (n, i, 0)),
                   pl.BlockSpec((1, step, LANES), lambda n, i: (n, i, 0))],
        out_shape=[jax.ShapeDtypeStruct((N, L, W), BF16),
                   jax.ShapeDtypeStruct((N, L, LANES), F32)],
        compiler_params=_cparams("parallel", "parallel"),
        name="window_attn",
    )(qkv, qkv, qkv, qkv, qkv, bias)


def _token_order(src_ref, scratch_ref, d):
    if d == 1:
        return src_ref[0, 0].astype(F32)
    nchunk, n, _ = scratch_ref.shape
    for r in range(d):
        part = src_ref[0, r].astype(F32)
        for c in range(nchunk):
            scratch_ref[c, pl.ds(r, n // d, stride=d), :] = part[:, c * LANES:(c + 1) * LANES]
    return jnp.concatenate([scratch_ref[c] for c in range(nchunk)], axis=1)


def _memory_xattn(x, g_ref, wq_ref, kv_ref, wo_ref):
    xn = _rms(x, g_ref[...]).astype(BF16)
    q = jnp.dot(xn, wq_ref[...], preferred_element_type=F32).astype(BF16)
    heads = []
    for h in range(XATTN_HEADS):
        sl = slice(h * XATTN_HEAD_DIM, (h + 1) * XATTN_HEAD_DIM)
        k = kv_ref[0, :, sl]
        v = kv_ref[0, :, XATTN_WIDTH + h * XATTN_HEAD_DIM:XATTN_WIDTH + (h + 1) * XATTN_HEAD_DIM]
        s = lax.dot_general(q[:, sl], k, (((1,), (1,)), ((), ())), preferred_element_type=F32)
        m = jnp.max(s, axis=-1, keepdims=True)
        p = jnp.exp2(s - m)
        l = jnp.sum(p, axis=-1, keepdims=True)
        heads.append(jnp.dot(p.astype(BF16), v, preferred_element_type=F32) * (1.0 / l))
    o = jnp.concatenate(heads, axis=1).astype(BF16)
    return x + jnp.dot(o, wo_ref[...], preferred_element_type=F32)


def _mixer_out_kernel(x_ref, o1_ref, o2_ref, o3_ref, l1_ref, l2_ref, l3_ref, rest_ref, halo_ref,
                      convw_ref, poolw_ref, pscale_ref, pwin_ref, expand_ref, wout_ref,
                      xg_ref, wq_ref, kv_ref, wo_ref, out_ref,
                      os2_ref, os3_ref, ls2_ref, ls3_ref, *, tm, tiles_per_seq):
    it = pl.program_id(0) % tiles_per_seq
    lses = [_token_order(r, s, d)
            for r, s, d in zip((l1_ref, l2_ref, l3_ref), (None, ls2_ref, ls3_ref), DILATIONS)]
    outs = [_token_order(r, s, d)
            for r, s, d in zip((o1_ref, o2_ref, o3_ref), (None, os2_ref, os3_ref), DILATIONS)]
    lm = jnp.maximum(jnp.maximum(lses[0], lses[1]), lses[2])
    es = [jnp.exp(l - lm) for l in lses]
    inv = 1.0 / (es[0] + es[1] + es[2])
    y_attn = jnp.zeros((tm, ATTN_WIDTH), F32)
    for e, o in zip(es, outs):
        w = e * inv
        w_hi = w.astype(BF16)
        w_lo = (w - w_hi.astype(F32)).astype(BF16)
        wide = (jnp.dot(w_hi, expand_ref[...], preferred_element_type=F32)
                + jnp.dot(w_lo, expand_ref[...], preferred_element_type=F32))
        y_attn = y_attn + wide * o
    y_attn = y_attn.astype(BF16)

    rest = rest_ref[...]
    halo = jnp.where(it == 0, 0.0, halo_ref[...])
    ext = jnp.concatenate([halo, rest], axis=0)

    u = ext[:, :POOL_WIDTH]
    s2 = u + pltpu.roll(u, 1, 0)
    s4 = s2 + pltpu.roll(s2, 2, 0)
    s8 = s4 + pltpu.roll(s4, 4, 0)
    s16 = s8 + pltpu.roll(s8, 8, 0)
    col = lax.broadcasted_iota(jnp.int32, u.shape, 1)
    g = POOL_GROUP_DIM
    wsum = jnp.where(col < g, s2, jnp.where(col < 2 * g, s4, jnp.where(col < 3 * g, s8, s16)))
    pos = it * tm + lax.broadcasted_iota(jnp.int32, (tm, POOL_WIDTH), 0)
    count = jnp.minimum((pos + 1).astype(F32), pwin_ref[...])
    pooled = wsum[POOL_HALO:] / count - u[POOL_HALO:]
    y_pool = jnp.dot(pooled.astype(BF16), poolw_ref[...], preferred_element_type=F32)
    y_pool = (y_pool * pscale_ref[...]).astype(BF16)

    b_gate = rest[:, POOL_WIDTH:POOL_WIDTH + CONV_WIDTH]
    cu = ext[:, POOL_WIDTH + CONV_WIDTH:POOL_WIDTH + 2 * CONV_WIDTH] * ext[:, POOL_WIDTH + 2 * CONV_WIDTH:]
    y = (convw_ref[0:1, :] * cu + convw_ref[1:2, :] * pltpu.roll(cu, 1, 0)
         + convw_ref[2:3, :] * pltpu.roll(cu, 2, 0))
    y_conv = (b_gate * y[POOL_HALO:]).astype(BF16)

    a0, a1 = ATTN_WIDTH, ATTN_WIDTH + POOL_WIDTH
    acc = jnp.dot(y_attn, wout_ref[:a0, :], preferred_element_type=F32)
    acc += jnp.dot(y_pool, wout_ref[a0:a1, :], preferred_element_type=F32)
    acc += jnp.dot(y_conv, wout_ref[a1:, :], preferred_element_type=F32)
    out_ref[...] = _memory_xattn(x_ref[...] + acc, xg_ref, wq_ref, kv_ref, wo_ref)


def _mixer_out(x, outs, lses, rest, conv_w, pool_w_bd, pool_scale, w_out, xg, wq, kv, wo, layer,
               S, tm):
    T, D = x.shape
    M = kv.shape[1]
    hb = tm // POOL_HALO
    tps = S // tm
    row = lambda i: (i, 0)
    const = lambda i: (0, 0)
    res_spec = lambda d, c: pl.BlockSpec((1, d, tm // d, c), lambda i: (i // tps, 0, i % tps, 0))
    pwin = jnp.asarray(np.repeat(np.array(POOL_WINDOWS, np.float32), POOL_GROUP_DIM)[None, :])
    expand = np.zeros((LANES, ATTN_WIDTH), np.float32)
    for h in range(ATTN_HEADS):
        expand[h, h * ATTN_HEAD_DIM:(h + 1) * ATTN_HEAD_DIM] = 1.0
    expand = jnp.asarray(expand, dtype=BF16)
    return pl.pallas_call(
        partial(_mixer_out_kernel, tm=tm, tiles_per_seq=tps),
        grid=(T // tm,),
        scratch_shapes=[pltpu.VMEM((ATTN_WIDTH // LANES, tm, LANES), F32)] * 2
                       + [pltpu.VMEM((1, tm, LANES), F32)] * 2,
        name="mixer_out",
        in_specs=[pl.BlockSpec((tm, D), row)]
                 + [res_spec(d, ATTN_WIDTH) for d in DILATIONS]
                 + [res_spec(d, LANES) for d in DILATIONS]
                 + [pl.BlockSpec((tm, REST_WIDTH), row),
                    pl.BlockSpec((POOL_HALO, REST_WIDTH), lambda i: (jnp.maximum(i * hb - 1, 0), 0)),
                    pl.BlockSpec(conv_w.shape, const),
                    pl.BlockSpec(pool_w_bd.shape, const),
                    pl.BlockSpec((1, POOL_WIDTH), const),
                    pl.BlockSpec((1, POOL_WIDTH), const),
                    pl.BlockSpec(expand.shape, const),
                    _layer_spec(w_out, layer),
                    pl.BlockSpec((1, D), const),
                    _layer_spec(wq, layer),
                    pl.BlockSpec((1, M, 2 * XATTN_WIDTH), lambda i: (i // tps, 0, 0)),
                    _layer_spec(wo, layer)],
        out_specs=pl.BlockSpec((tm, D), row),
        out_shape=jax.ShapeDtypeStruct((T, D), F32),
        compiler_params=_cparams("parallel"),
    )(x, *outs, *lses, rest, rest, conv_w, pool_w_bd, pool_scale, pwin, expand, w_out,
      xg, wq, kv, wo)


def _swiglu_chunk(hn, wg_ref, wu_ref, w2_ref):
    a = jnp.dot(hn, wg_ref[...], preferred_element_type=F32)
    b = jnp.dot(hn, wu_ref[...], preferred_element_type=F32)
    h = (a * jax.nn.sigmoid(a) * b).astype(BF16)
    return jnp.dot(h, w2_ref[...], preferred_element_type=F32)


def _dense_ffn_kernel(x_ref, g_ref, wg_ref, wu_ref, w2_ref, out_ref, hn_ref, acc_ref):
    j = pl.program_id(1)

    @pl.when(j == 0)
    def _():
        hn_ref[...] = _rms(x_ref[...], g_ref[...]).astype(BF16)
        acc_ref[...] = jnp.zeros_like(acc_ref)

    acc_ref[...] += _swiglu_chunk(hn_ref[...], wg_ref, wu_ref, w2_ref)

    @pl.when(j == pl.num_programs(1) - 1)
    def _():
        out_ref[...] = x_ref[...] + acc_ref[...]


def _dense_ffn(x, g, w13, w2, layer, tm, tf):
    T, D = x.shape
    dff = w2.shape[1]
    nj = dff // tf
    return pl.pallas_call(
        _dense_ffn_kernel,
        grid=(T // tm, nj),
        in_specs=[pl.BlockSpec((tm, D), lambda i, j: (i, 0)),
                  pl.BlockSpec((1, D), lambda i, j: (0, 0)),
                  pl.BlockSpec((None, D, tf), lambda i, j: (layer, 0, j)),
                  pl.BlockSpec((None, D, tf), lambda i, j: (layer, 0, j + nj)),
                  pl.BlockSpec((None, tf, D), lambda i, j: (layer, j, 0))],
        out_specs=pl.BlockSpec((tm, D), lambda i, j: (i, 0)),
        out_shape=jax.ShapeDtypeStruct((T, D), F32),
        scratch_shapes=[pltpu.VMEM((tm, D), BF16), pltpu.VMEM((tm, D), F32)],
        compiler_params=_cparams("parallel", "arbitrary"),
        name="dense_ffn",
    )(x, g, w13, w13, w2)


def _store_slab(ref, v):
    n, rows = v.shape[0], v.shape[1] // LANES
    for s in range(rows):
        ref[pl.ds(s, n, stride=rows), :] = v[:, s * LANES:(s + 1) * LANES]


def _load_slab(ref, first, n, rows):
    return jnp.concatenate(
        [ref[pl.ds(first * rows + s, n, stride=rows), :] for s in range(rows)], axis=1)


def _slab(row, rows):
    return pl.ds(pl.multiple_of(row * rows, rows), rows)


def _pack_bf16_pairs(v):
    half = v.shape[1] // 2
    bits = pltpu.bitcast(v.astype(BF16).astype(F32), jnp.uint32)
    return (bits[:, half:] & jnp.uint32(0xFFFF0000)) | (bits[:, :half] >> 16)


def _unpack_bf16_pairs(u):
    low = pltpu.bitcast(u << 16, F32)
    high = pltpu.bitcast(u & jnp.uint32(0xFFFF0000), F32)
    return jnp.concatenate([low, high], axis=1).astype(BF16)


def _router_kernel(x_ref, g_ref, wr_ref, hn_ref, route_ref):
    hn = _rms(x_ref[...], g_ref[...])
    _store_slab(hn_ref, _pack_bf16_pairs(hn))
    logits = jnp.dot(hn.astype(BF16), wr_ref[...], preferred_element_type=F32)
    lane = lax.broadcasted_iota(jnp.int32, logits.shape, 1).astype(F32)
    logits = jnp.where(lane < N_EXPERTS, logits, -jnp.inf)
    m1 = jnp.max(logits, axis=-1, keepdims=True)
    i1 = jnp.min(jnp.where(logits == m1, lane, float(LANES)), axis=-1, keepdims=True)
    rest = jnp.where(lane == i1, -jnp.inf, logits)
    m2 = jnp.max(rest, axis=-1, keepdims=True)
    i2 = jnp.min(jnp.where(rest == m2, lane, float(LANES)), axis=-1, keepdims=True)
    e2 = jnp.exp(m2 - m1)
    p1 = 1.0 / (1.0 + e2)
    p2 = e2 * p1
    route_ref[...] = jnp.where(lane == 0, i1, jnp.where(lane == 1, i2,
                     jnp.where(lane == 2, p1, jnp.where(lane == 3, p2, 0.0))))


def _router(x, g, wr, tm):
    T, D = x.shape
    return pl.pallas_call(
        _router_kernel,
        grid=(T // tm,),
        in_specs=[pl.BlockSpec((tm, D), lambda i: (i, 0)),
                  pl.BlockSpec((1, D), lambda i: (0, 0)),
                  pl.BlockSpec((D, LANES), lambda i: (0, 0))],
        out_specs=[pl.BlockSpec((tm * PACKED_ROWS, LANES), lambda i: (i, 0)),
                   pl.BlockSpec((tm, LANES), lambda i: (i, 0))],
        out_shape=[jax.ShapeDtypeStruct((T * PACKED_ROWS, LANES), jnp.uint32),
                   jax.ShapeDtypeStruct((T, LANES), F32)],
        compiler_params=_cparams("parallel"),
        name="router",
    )(x, g, wr)


def _row_copy(src_ref, src_row, dst_ref, dst_row, sem, rows):
    return pltpu.make_async_copy(src_ref.at[_slab(src_row, rows)], dst_ref.at[_slab(dst_row, rows)],
                                 sem)


def _wait_rows(src_ref, dst_ref, n, sem, rows):
    span = pl.ds(0, n * rows)
    pltpu.make_async_copy(src_ref.at[span], dst_ref.at[span], sem).wait()


def _dispatch_kernel(pos_ref, pad_ref, hn_ref, xs_ref, sem, *, tm):
    i = pl.program_id(0)
    base = i * tm * TOP_K

    def start(r, c):
        for kk in range(TOP_K):
            _row_copy(hn_ref, r, xs_ref, pos_ref[base + r * TOP_K + kk], sem, PACKED_ROWS).start()
        return c

    lax.fori_loop(0, tm, start, 0, unroll=DMA_ISSUE_UNROLL)
    _wait_rows(xs_ref, xs_ref, tm * TOP_K, sem, PACKED_ROWS)

    @pl.when(i == 0)
    def _():
        npad = pad_ref.shape[0]

        def pstart(r, c):
            _row_copy(hn_ref, 0, xs_ref, pad_ref[r], sem, PACKED_ROWS).start()
            return c

        lax.fori_loop(0, npad, pstart, 0, unroll=DMA_ISSUE_UNROLL)
        _wait_rows(xs_ref, xs_ref, npad, sem, PACKED_ROWS)


def _dispatch(hn_slab, pos, pad_rows, n_rows, tm):
    T = hn_slab.shape[0] // PACKED_ROWS
    return pl.pallas_call(
        partial(_dispatch_kernel, tm=tm),
        grid_spec=pltpu.PrefetchScalarGridSpec(
            num_scalar_prefetch=2,
            grid=(T // tm,),
            in_specs=[pl.BlockSpec((tm * PACKED_ROWS, LANES), lambda i, *_: (i, 0))],
            out_specs=pl.BlockSpec(memory_space=pl.ANY),
            scratch_shapes=[pltpu.SemaphoreType.DMA(())]),
        out_shape=jax.ShapeDtypeStruct((n_rows * PACKED_ROWS, LANES), jnp.uint32),
        compiler_params=_cparams("arbitrary"),
        name="dispatch",
    )(pos, pad_rows, hn_slab)


def _expert_ffn_kernel(te_ref, tv_ref, xs_ref, wg_ref, wu_ref, w2_ref, y_ref, hn_ref, acc_ref):
    i, j = pl.program_id(0), pl.program_id(1)
    last = pl.num_programs(1) - 1
    valid = tv_ref[i] > 0

    @pl.when(valid & (j == 0))
    def _():
        hn_ref[...] = _unpack_bf16_pairs(_load_slab(xs_ref, 0, hn_ref.shape[0], PACKED_ROWS))
        acc_ref[...] = jnp.zeros_like(acc_ref)

    @pl.when(valid)
    def _():
        acc_ref[...] += _swiglu_chunk(hn_ref[...], wg_ref, wu_ref, w2_ref)

    @pl.when(valid & (j == last))
    def _():
        _store_slab(y_ref, acc_ref[...])

    @pl.when(jnp.logical_not(valid) & (j == last))
    def _():
        y_ref[...] = jnp.zeros_like(y_ref)


def _expert_ffn(xs, tile_expert, tile_valid, w13, w2, layer, tm, tf):
    n_rows = xs.shape[0] // PACKED_ROWS
    D = D_MODEL
    dff = w2.shape[2]
    nj = dff // tf
    jj = lambda i, j, tv: jnp.where(tv[i] > 0, j, nj - 1)
    return pl.pallas_call(
        _expert_ffn_kernel,
        grid_spec=pltpu.PrefetchScalarGridSpec(
            num_scalar_prefetch=2,
            grid=(n_rows // tm, nj),
            in_specs=[pl.BlockSpec((tm * PACKED_ROWS, LANES), lambda i, j, te, tv: (i, 0)),
                      pl.BlockSpec((None, None, D, tf),
                                   lambda i, j, te, tv: (layer, te[i], 0, jj(i, j, tv))),
                      pl.BlockSpec((None, None, D, tf),
                                   lambda i, j, te, tv: (layer, te[i], 0, jj(i, j, tv) + nj)),
                      pl.BlockSpec((None, None, tf, D),
                                   lambda i, j, te, tv: (layer, te[i], jj(i, j, tv), 0))],
            out_specs=pl.BlockSpec((tm * SLAB_ROWS, LANES), lambda i, j, te, tv: (i, 0)),
            scratch_shapes=[pltpu.VMEM((tm, D), BF16), pltpu.VMEM((tm, D), F32)]),
        out_shape=jax.ShapeDtypeStruct((n_rows * SLAB_ROWS, LANES), F32),
        compiler_params=_cparams("parallel", "arbitrary"),
        name="expert_ffn",
    )(tile_expert, tile_valid, xs, w13, w13, w2)


def _combine_kernel(pos_ref, x_ref, route_ref, g_ref, y_ref, out_ref, buf, sem, *, tm,
                    final_norm):
    i = pl.program_id(0)

    def issue(tile, slot):
        base = tile * tm * TOP_K

        def start(r, c):
            for kk in range(TOP_K):
                _row_copy(y_ref, pos_ref[base + r * TOP_K + kk], buf.at[slot], kk * tm + r,
                          sem.at[slot], SLAB_ROWS).start()
            return c

        lax.fori_loop(0, tm, start, 0, unroll=DMA_ISSUE_UNROLL)

    @pl.when(i == 0)
    def _():
        issue(0, 0)

    @pl.when(i + 1 < pl.num_programs(0))
    def _():
        issue(i + 1, (i + 1) % 2)

    slot = i % 2
    _wait_rows(y_ref, buf.at[slot], tm * TOP_K, sem.at[slot], SLAB_ROWS)
    route = route_ref[...]
    acc = x_ref[...]
    for kk in range(TOP_K):
        gate = route[:, TOP_K + kk:TOP_K + kk + 1]
        acc = acc + gate * _load_slab(buf.at[slot], kk * tm, tm, SLAB_ROWS)
    out_ref[...] = _rms(acc, g_ref[...]) if final_norm else acc


def _combine(x, route, y, pos, final_g, final_norm, tm):
    T, D = x.shape
    return pl.pallas_call(
        partial(_combine_kernel, tm=tm, final_norm=final_norm),
        grid_spec=pltpu.PrefetchScalarGridSpec(
            num_scalar_prefetch=1,
            grid=(T // tm,),
            in_specs=[pl.BlockSpec((tm, D), lambda i, *_: (i, 0)),
                      pl.BlockSpec((tm, LANES), lambda i, *_: (i, 0)),
                      pl.BlockSpec((1, D), lambda i, *_: (0, 0)),
                      pl.BlockSpec(memory_space=pl.ANY)],
            out_specs=pl.BlockSpec((tm, D), lambda i, *_: (i, 0)),
            scratch_shapes=[pltpu.VMEM((2, tm * TOP_K * SLAB_ROWS, LANES), F32),
                            pltpu.SemaphoreType.DMA((2,))]),
        out_shape=jax.ShapeDtypeStruct((T, D), F32),
        compiler_params=_cparams("arbitrary"),
        name="combine",
    )(pos, x, route, final_g, y)


def _routing_tables(expert_ids, tm, n_tiles):
    e_flat = expert_ids.reshape(-1)
    onehot = (e_flat[:, None] == jnp.arange(N_EXPERTS, dtype=jnp.int32)[None, :]).astype(jnp.int32)
    csum = jnp.cumsum(onehot, axis=0)
    counts = csum[-1]
    rank = jnp.sum(onehot * (csum - 1), axis=1)
    padded = ((counts + tm - 1) // tm) * tm
    ends = jnp.cumsum(padded)
    starts = ends - padded
    pos = (jnp.sum(onehot * starts[None, :], axis=1) + rank).astype(jnp.int32)

    tile_start = jnp.arange(n_tiles, dtype=jnp.int32) * tm
    tile_valid = (tile_start < ends[-1]).astype(jnp.int32)
    te = jnp.sum((tile_start[:, None] >= ends[None, :]).astype(jnp.int32), axis=1)
    last_e = jnp.sum((ends[-1] - 1 >= ends).astype(jnp.int32))
    tile_expert = jnp.where(tile_valid > 0, te, last_e).astype(jnp.int32)

    npad_e = padded - counts
    pcum = jnp.cumsum(npad_e)
    k = jnp.arange(N_EXPERTS * tm, dtype=jnp.int32)
    ek = jnp.sum((k[:, None] >= pcum[None, :]).astype(jnp.int32), axis=1)
    ohk = (ek[:, None] == jnp.arange(N_EXPERTS, dtype=jnp.int32)[None, :]).astype(jnp.int32)
    in_group = jnp.sum(ohk * (starts + counts - (pcum - npad_e))[None, :], axis=1) + k
    pad = jnp.where(ek < N_EXPERTS, in_group, ends[-1] + k - pcum[-1]).astype(jnp.int32)
    return pos, tile_expert, tile_valid, pad


def _moe_ffn(x, g, wr, w13, w2, layer, final_g, final_norm, tm, tme, tf):
    T, D = x.shape
    n_tiles = (T * TOP_K) // tme + N_EXPERTS
    hn_slab, route = _router(x, g, wr, tm)
    expert_ids = route[:, :TOP_K].astype(jnp.int32)
    pos, tile_expert, tile_valid, pad = _routing_tables(expert_ids, tme, n_tiles)
    xs = _dispatch(hn_slab, pos, pad, n_tiles * tme, tm)
    y = _expert_ffn(xs, tile_expert, tile_valid, w13, w2, layer, tme, tf)
    return _combine(x, route, y, pos, final_g, final_norm, tm)


def _final_norm_kernel(x_ref, g_ref, o_ref):
    o_ref[...] = _rms(x_ref[...], g_ref[...])


def _final_norm(x, g, tm):
    T, D = x.shape
    return pl.pallas_call(
        _final_norm_kernel,
        grid=(T // tm,),
        in_specs=[pl.BlockSpec((tm, D), lambda i: (i, 0)), pl.BlockSpec((1, D), lambda i: (0, 0))],
        out_specs=pl.BlockSpec((tm, D), lambda i: (i, 0)),
        out_shape=jax.ShapeDtypeStruct((T, D), F32),
        compiler_params=_cparams("parallel"),
        name="final_norm",
    )(x, g)


def _block_diag(pw):
    G, C, _ = pw.shape
    eye = jnp.eye(G, dtype=pw.dtype)
    return (eye[:, None, :, None] * pw[:, :, None, :]).reshape(G * C, G * C)


def kernel(x, mem, rel_bias, mem_norm_g, mix_norm_g, w_in, conv_w, pool_w, pool_scale, w_out,
           xattn_norm_g, wq_x, wkv_x, wo_x, ffn_norm_g, w13_dense, w2_dense, router_w, w13_moe,
           w2_moe, final_norm_g):
    B, S, D = x.shape
    depth = w_in.shape[0]
    M = mem.shape[1]
    T = B * S
    tm = min(ROW_TILE, S)
    assert D == D_MODEL and S % (max(DILATIONS) * Q_BLOCK) == 0 and S % tm == 0

    xf = x.reshape(T, D)
    row2 = lambda v: v.reshape(1, -1)
    q_scale = jnp.where(jnp.arange(w_in.shape[-1]) < ATTN_WIDTH,
                        ATTN_HEAD_DIM ** -0.5 * LOG2_E, 1.0)
    w_in_b = (w_in * q_scale).astype(BF16)
    biases = [_band_bias(rel_bias * LOG2_E, d) for d in DILATIONS]
    wr_pad = jnp.pad(router_w, ((0, 0), (0, 0), (0, LANES - N_EXPERTS))).astype(BF16)
    w_out_b, wkv_b, wo_b = (w.astype(BF16) for w in (w_out, wkv_x, wo_x))
    wq_b = (wq_x * (XATTN_HEAD_DIM ** -0.5 * LOG2_E)).astype(BF16)
    w13_dense_b, w2_dense_b = w13_dense.astype(BF16), w2_dense.astype(BF16)
    w13_moe_b, w2_moe_b = w13_moe.astype(BF16), w2_moe.astype(BF16)

    mem_f = mem.reshape(B * M, D)
    for layer in range(depth):
        *qkvs, rest = _in_proj(xf, row2(mix_norm_g[layer]), w_in_b, layer, B, S, tm)
        branch = []
        for qkv, bias, d in zip(qkvs, biases, DILATIONS):
            o, lse = _window_attn(qkv.reshape(B * d, S // d, qkv.shape[-1]), bias, ATTN_Q_BLOCKS)
            branch.append((o.reshape(B, d, S // d, -1), lse.reshape(B, d, S // d, -1)))
        kv = _norm_proj(mem_f, row2(mem_norm_g), wkv_b, layer, BF16, min(tm, B * M))
        xf = _mixer_out(xf, [o for o, _ in branch], [l for _, l in branch], rest,
                        conv_w[layer], _block_diag(pool_w[layer]).astype(BF16),
                        row2(pool_scale[layer]), w_out_b, row2(xattn_norm_g[layer]), wq_b,
                        kv.reshape(B, M, 2 * XATTN_WIDTH), wo_b, layer, S, tm)

        g = row2(ffn_norm_g[layer])
        last = layer == depth - 1
        if layer % 2 == 0:
            xf = _dense_ffn(xf, g, w13_dense_b, w2_dense_b, layer // 2, tm, DENSE_FF_TILE)
            if last:
                xf = _final_norm(xf, row2(final_norm_g), tm)
        else:
            xf = _moe_ffn(xf, g, wr_pad[layer // 2], w13_moe_b, w2_moe_b, layer // 2,
                          row2(final_norm_g), last, tm, EXPERT_ROW_TILE, EXPERT_FF_TILE)
    return xf.reshape(B, S, D)
```

```python
from functools import partial

import numpy as np
import jax
import jax.numpy as jnp
from jax import lax
from jax.experimental import pallas as pl
from jax.experimental.pallas import tpu as pltpu

F32 = jnp.float32
BF16 = jnp.bfloat16

D_MODEL = 1024
ATTN_HEADS = 8
ATTN_HEAD_DIM = 64
ATTN_WIDTH = ATTN_HEADS * ATTN_HEAD_DIM
POOL_WIDTH = 256
POOL_GROUP_DIM = 64
POOL_WINDOWS = (2, 4, 8, 16)
POOL_HALO = 16
CONV_WIDTH = 256
REST_WIDTH = POOL_WIDTH + 3 * CONV_WIDTH
DILATIONS = (1, 4, 16)
WINDOW_STEPS = 128
Q_BLOCK = 128
NEG_INF = -1e30
REL_BUCKETS = 32
REL_MAX_DIST = 128
XATTN_HEADS = 4
XATTN_HEAD_DIM = 128
XATTN_WIDTH = XATTN_HEADS * XATTN_HEAD_DIM
N_EXPERTS = 8
TOP_K = 2
EPS = 1e-6

LANES = 128
MXU_COLS = 256
SLAB_ROWS = D_MODEL // LANES
PACKED_ROWS = SLAB_ROWS // 2
ROW_TILE = 512
EXPERT_ROW_TILE = 512
EXPERT_FF_TILE = 1792
DENSE_FF_TILE = 1408
LOG2_E = 1.4426950408889634
LN_2 = 0.6931471805599453
ATTN_Q_BLOCKS = 4
DMA_ISSUE_UNROLL = 4
VMEM_LIMIT = 56 * 1024 * 1024


def _cparams(*sem):
    return pltpu.CompilerParams(dimension_semantics=sem, vmem_limit_bytes=VMEM_LIMIT)


def _rms(x, g):
    return x * lax.rsqrt(jnp.mean(x * x, axis=-1, keepdims=True) + EPS) * g


def _norm_proj_kernel(x_ref, g_ref, w_ref, o_ref):
    xn = _rms(x_ref[...], g_ref[...]).astype(BF16)
    o_ref[...] = jnp.dot(xn, w_ref[...], preferred_element_type=F32).astype(o_ref.dtype)


def _layer_spec(w, layer):
    return pl.BlockSpec((None,) + w.shape[1:], lambda *_: (layer, 0, 0))


def _norm_proj(x, g, w, layer, out_dtype, tm):
    T, D = x.shape
    N = w.shape[-1]
    return pl.pallas_call(
        _norm_proj_kernel,
        grid=(T // tm,),
        in_specs=[pl.BlockSpec((tm, D), lambda i: (i, 0)),
                  pl.BlockSpec((1, D), lambda i: (0, 0)),
                  _layer_spec(w, layer)],
        out_specs=pl.BlockSpec((tm, N), lambda i: (i, 0)),
        out_shape=jax.ShapeDtypeStruct((T, N), out_dtype),
        compiler_params=_cparams("parallel"),
        name="norm_proj",
    )(x, g, w)


def _in_proj_kernel(x_ref, g_ref, w_ref, *refs):
    nd = len(DILATIONS)
    qkv_refs, rest_ref, acc_refs = refs[:nd], refs[nd], refs[nd + 1:]
    per, tm, _ = acc_refs[0].shape
    nq = len(acc_refs) * MXU_COLS
    xn = _rms(x_ref[...], g_ref[...]).astype(BF16)
    for nb, acc_ref in enumerate(acc_refs):
        cols = slice(nb * MXU_COLS, (nb + 1) * MXU_COLS)
        acc = jnp.dot(xn, w_ref[:, cols], preferred_element_type=F32)
        for c in range(per):
            acc_ref[c] = acc[:, c * LANES:(c + 1) * LANES]
        for d, qkv_ref in zip(DILATIONS, qkv_refs):
            for r in range(d):
                rows = [acc_ref[c, pl.ds(r, tm // d, stride=d), :] for c in range(per)]
                qkv_ref[0, r, :, cols] = jnp.concatenate(rows, axis=1).astype(BF16)
    rest_ref[...] = jnp.dot(xn, w_ref[:, nq:], preferred_element_type=F32)


def _in_proj(x, g, w, layer, B, S, tm):
    T, D = x.shape
    N = w.shape[-1]
    nq = 3 * ATTN_WIDTH
    tps = S // tm
    res_spec = lambda d: pl.BlockSpec((1, d, tm // d, nq), lambda i: (i // tps, 0, i % tps, 0))
    return pl.pallas_call(
        _in_proj_kernel,
        grid=(T // tm,),
        in_specs=[pl.BlockSpec((tm, D), lambda i: (i, 0)),
                  pl.BlockSpec((1, D), lambda i: (0, 0)),
                  _layer_spec(w, layer)],
        out_specs=[res_spec(d) for d in DILATIONS]
                  + [pl.BlockSpec((tm, N - nq), lambda i: (i, 0))],
        out_shape=[jax.ShapeDtypeStruct((B, d, S // d, nq), BF16) for d in DILATIONS]
                  + [jax.ShapeDtypeStruct((T, N - nq), F32)],
        scratch_shapes=[pltpu.VMEM((MXU_COLS // LANES, tm, LANES), F32)] * (nq // MXU_COLS),
        compiler_params=_cparams("parallel"),
        name="in_proj",
    )(x, g, w)


def _t5_causal_bucket(dist):
    max_exact = REL_BUCKETS // 2
    d = np.maximum(dist, 1).astype(np.float32)
    large = max_exact + (np.log(d / max_exact) / np.log(REL_MAX_DIST / max_exact)
                         * (REL_BUCKETS - max_exact)).astype(np.int32)
    large = np.minimum(large, REL_BUCKETS - 1)
    return np.where(dist < max_exact, dist, large).astype(np.int32)


def _band_bias(rel_bias, dil):
    row = np.arange(Q_BLOCK)[:, None]
    col = np.arange(2 * Q_BLOCK)[None, :]
    steps = Q_BLOCK + row - col
    valid = (steps >= 0) & (steps <= WINDOW_STEPS)
    bucket = _t5_causal_bucket(dil * np.clip(steps, 0, WINDOW_STEPS))
    onehot = (bucket[..., None] == np.arange(REL_BUCKETS)).astype(np.float32)
    b = jnp.einsum('qcb,bh->hqc', jnp.asarray(onehot), rel_bias.astype(F32),
                   precision=lax.Precision.HIGHEST)
    masks = np.stack([valid, valid & (col >= Q_BLOCK)])
    return jnp.where(jnp.asarray(masks)[:, None], b[None], NEG_INF)


def _window_attn_kernel(q_ref, kp_ref, k_ref, vp_ref, v_ref, bias_ref, o_ref, lse_ref, *, nq):
    first_step = jnp.where(pl.program_id(1) == 0, 1, 0)
    lane = lax.broadcasted_iota(jnp.int32, (Q_BLOCK, LANES), 1)
    low_half = lane < ATTN_HEAD_DIM
    Q = Q_BLOCK
    for j in range(nq):
        rows = slice(j * Q, (j + 1) * Q)
        if j == 0:
            k2 = jnp.concatenate([kp_ref[0], k_ref[0, :Q, :]], axis=0)
            v2 = jnp.concatenate([vp_ref[0], v_ref[0, :Q, :]], axis=0)
            table = first_step
        else:
            k2 = k_ref[0, (j - 1) * Q:(j + 1) * Q, :]
            v2 = v_ref[0, (j - 1) * Q:(j + 1) * Q, :]
            table = 0
        lse = jnp.zeros((Q, LANES), F32)
        for g in range(ATTN_WIDTH // LANES):
            cols = slice(g * LANES, (g + 1) * LANES)
            qg, kg, vg = q_ref[0, rows, cols], k2[:, cols], v2[:, cols]
            og = None
            for half in range(2):
                h = 2 * g + half
                keep = low_half if half == 0 else jnp.logical_not(low_half)
                qh = jnp.where(keep, qg, jnp.zeros_like(qg))
                s = lax.dot_general(qh, kg, (((1,), (1,)), ((), ())),
                                    preferred_element_type=F32)
                s = s + bias_ref[table, h]
                m = jnp.max(s, axis=-1, keepdims=True)
                p = jnp.exp2(s - m)
                l = jnp.sum(p, axis=-1, keepdims=True)
                o = jnp.dot(p.astype(BF16), vg, preferred_element_type=F32) * (1.0 / l)
                og = o if half == 0 else jnp.where(low_half, og, o)
                lse = jnp.where(lane == h, (m + jnp.log2(l)) * LN_2, lse)
            o_ref[0, rows, cols] = og.astype(o_ref.dtype)
        lse_ref[0, rows, :] = lse


def _window_attn(qkv, bias, nq):
    N, L, _ = qkv.shape
    W = ATTN_WIDTH
    nq = min(nq, L // Q_BLOCK)
    step = nq * Q_BLOCK
    prev = lambda i: jnp.maximum(i * nq - 1, 0)
    return pl.pallas_call(
        partial(_window_attn_kernel, nq=nq),
        grid=(N, L // step),
        in_specs=[pl.BlockSpec((1, step, W), lambda n, i: (n, i, 0)),
                  pl.BlockSpec((1, Q_BLOCK, W), lambda n, i: (n, prev(i), 1)),
                  pl.BlockSpec((1, step, W), lambda n, i: (n, i, 1)),
                  pl.BlockSpec((1, Q_BLOCK, W), lambda n, i: (n, prev(i), 2)),
                  pl.BlockSpec((1, step, W), lambda n, i: (n, i, 2)),
                  pl.BlockSpec(bias.shape, lambda n, i: (0, 0, 0, 0))],
        out_specs=[pl.BlockSpec((1, step, W), lambda n, i: (n, i, 0)),
                   pl.BlockSpec((1, step, LANES), lambda n, i: (n, i, 0))],
        out_shape=[jax.ShapeDtypeStruct((N, L, W), BF16),
                   jax.ShapeDtypeStruct((N, L, LANES), F32)],
        compiler_params=_cparams("parallel", "parallel"),
        name="window_attn",
    )(qkv, qkv, qkv, qkv, qkv, bias)


def _token_order(src_ref, scratch_ref, d):
    if d == 1:
        return src_ref[0, 0].astype(F32)
    nchunk, n, _ = scratch_ref.shape
    for r in range(d):
        part = src_ref[0, r].astype(F32)
        for c in range(nchunk):
            scratch_ref[c, pl.ds(r, n // d, stride=d), :] = part[:, c * LANES:(c + 1) * LANES]
    return jnp.concatenate([scratch_ref[c] for c in range(nchunk)], axis=1)


def _memory_xattn(x, g_ref, wq_ref, kv_ref, wo_ref):
    xn = _rms(x, g_ref[...]).astype(BF16)
    q = jnp.dot(xn, wq_ref[...], preferred_element_type=F32).astype(BF16)
    heads = []
    for h in range(XATTN_HEADS):
        sl = slice(h * XATTN_HEAD_DIM, (h + 1) * XATTN_HEAD_DIM)
        k = kv_ref[0, :, sl]
        v = kv_ref[0, :, XATTN_WIDTH + h * XATTN_HEAD_DIM:XATTN_WIDTH + (h + 1) * XATTN_HEAD_DIM]
        s = lax.dot_general(q[:, sl], k, (((1,), (1,)), ((), ())), preferred_element_type=F32)
        m = jnp.max(s, axis=-1, keepdims=True)
        p = jnp.exp2(s - m)
        l = jnp.sum(p, axis=-1, keepdims=True)
        heads.append(jnp.dot(p.astype(BF16), v, preferred_element_type=F32) * (1.0 / l))
    o = jnp.concatenate(heads, axis=1).astype(BF16)
    return x + jnp.dot(o, wo_ref[...], preferred_element_type=F32)


def _mixer_out_kernel(x_ref, o1_ref, o2_ref, o3_ref, l1_ref, l2_ref, l3_ref, rest_ref, halo_ref,
                      convw_ref, poolw_ref, pscale_ref, pwin_ref, expand_ref, wout_ref,
                      xg_ref, wq_ref, kv_ref, wo_ref, out_ref,
                      os2_ref, os3_ref, ls2_ref, ls3_ref, *, tm, tiles_per_seq):
    it = pl.program_id(0) % tiles_per_seq
    lses = [_token_order(r, s, d)
            for r, s, d in zip((l1_ref, l2_ref, l3_ref), (None, ls2_ref, ls3_ref), DILATIONS)]
    outs = [_token_order(r, s, d)
            for r, s, d in zip((o1_ref, o2_ref, o3_ref), (None, os2_ref, os3_ref), DILATIONS)]
    lm = jnp.maximum(jnp.maximum(lses[0], lses[1]), lses[2])
    es = [jnp.exp(l - lm) for l in lses]
    inv = 1.0 / (es[0] + es[1] + es[2])
    y_attn = jnp.zeros((tm, ATTN_WIDTH), F32)
    for e, o in zip(es, outs):
        w = e * inv
        w_hi = w.astype(BF16)
        w_lo = (w - w_hi.astype(F32)).astype(BF16)
        wide = jnp.dot(jnp.concatenate([w_hi, w_lo], axis=1), expand_ref[...],
                       preferred_element_type=F32)
        y_attn = y_attn + wide * o
    y_attn = y_attn.astype(BF16)

    rest = rest_ref[...]
    halo = jnp.where(it == 0, 0.0, halo_ref[...])
    ext = jnp.concatenate([halo, rest], axis=0)

    u = ext[:, :POOL_WIDTH]
    s2 = u + pltpu.roll(u, 1, 0)
    s4 = s2 + pltpu.roll(s2, 2, 0)
    s8 = s4 + pltpu.roll(s4, 4, 0)
    s16 = s8 + pltpu.roll(s8, 8, 0)
    col = lax.broadcasted_iota(jnp.int32, u.shape, 1)
    g = POOL_GROUP_DIM
    wsum = jnp.where(col < g, s2, jnp.where(col < 2 * g, s4, jnp.where(col < 3 * g, s8, s16)))
    pos = it * tm + lax.broadcasted_iota(jnp.int32, (tm, POOL_WIDTH), 0)
    count = jnp.minimum((pos + 1).astype(F32), pwin_ref[...])
    pooled = wsum[POOL_HALO:] / count - u[POOL_HALO:]
    y_pool = jnp.dot(pooled.astype(BF16), poolw_ref[...], preferred_element_type=F32)
    y_pool = (y_pool * pscale_ref[...]).astype(BF16)

    b_gate = rest[:, POOL_WIDTH:POOL_WIDTH + CONV_WIDTH]
    cu = ext[:, POOL_WIDTH + CONV_WIDTH:POOL_WIDTH + 2 * CONV_WIDTH] * ext[:, POOL_WIDTH + 2 * CONV_WIDTH:]
    y = (convw_ref[0:1, :] * cu + convw_ref[1:2, :] * pltpu.roll(cu, 1, 0)
         + convw_ref[2:3, :] * pltpu.roll(cu, 2, 0))
    y_conv = (b_gate * y[POOL_HALO:]).astype(BF16)

    a0, a1 = ATTN_WIDTH, ATTN_WIDTH + POOL_WIDTH
    acc = jnp.dot(y_attn, wout_ref[:a0, :], preferred_element_type=F32)
    acc += jnp.dot(y_pool, wout_ref[a0:a1, :], preferred_element_type=F32)
    acc += jnp.dot(y_conv, wout_ref[a1:, :], preferred_element_type=F32)
    out_ref[...] = _memory_xattn(x_ref[...] + acc, xg_ref, wq_ref, kv_ref, wo_ref)


def _mixer_out(x, outs, lses, rest, conv_w, pool_w_bd, pool_scale, w_out, xg, wq, kv, wo, layer,
               S, tm):
    T, D = x.shape
    M = kv.shape[1]
    hb = tm // POOL_HALO
    tps = S // tm
    row = lambda i: (i, 0)
    const = lambda i: (0, 0)
    res_spec = lambda d, c: pl.BlockSpec((1, d, tm // d, c), lambda i: (i // tps, 0, i % tps, 0))
    pwin = jnp.asarray(np.repeat(np.array(POOL_WINDOWS, np.float32), POOL_GROUP_DIM)[None, :])
    expand = np.zeros((2, LANES, ATTN_WIDTH), np.float32)
    for h in range(ATTN_HEADS):
        expand[:, h, h * ATTN_HEAD_DIM:(h + 1) * ATTN_HEAD_DIM] = 1.0
    expand = jnp.asarray(expand.reshape(2 * LANES, ATTN_WIDTH), dtype=BF16)
    return pl.pallas_call(
        partial(_mixer_out_kernel, tm=tm, tiles_per_seq=tps),
        grid=(T // tm,),
        scratch_shapes=[pltpu.VMEM((ATTN_WIDTH // LANES, tm, LANES), F32)] * 2
                       + [pltpu.VMEM((1, tm, LANES), F32)] * 2,
        name="mixer_out",
        in_specs=[pl.BlockSpec((tm, D), row)]
                 + [res_spec(d, ATTN_WIDTH) for d in DILATIONS]
                 + [res_spec(d, LANES) for d in DILATIONS]
                 + [pl.BlockSpec((tm, REST_WIDTH), row),
                    pl.BlockSpec((POOL_HALO, REST_WIDTH), lambda i: (jnp.maximum(i * hb - 1, 0), 0)),
                    pl.BlockSpec(conv_w.shape, const),
                    pl.BlockSpec(pool_w_bd.shape, const),
                    pl.BlockSpec((1, POOL_WIDTH), const),
                    pl.BlockSpec((1, POOL_WIDTH), const),
                    pl.BlockSpec(expand.shape, const),
                    _layer_spec(w_out, layer),
                    pl.BlockSpec((1, D), const),
                    _layer_spec(wq, layer),
                    pl.BlockSpec((1, M, 2 * XATTN_WIDTH), lambda i: (i // tps, 0, 0)),
                    _layer_spec(wo, layer)],
        out_specs=pl.BlockSpec((tm, D), row),
        out_shape=jax.ShapeDtypeStruct((T, D), F32),
        compiler_params=_cparams("parallel"),
    )(x, *outs, *lses, rest, rest, conv_w, pool_w_bd, pool_scale, pwin, expand, w_out,
      xg, wq, kv, wo)


def _swiglu_chunk(hn, wg_ref, wu_ref, w2_ref):
    a = jnp.dot(hn, wg_ref[...], preferred_element_type=F32)
    b = jnp.dot(hn, wu_ref[...], preferred_element_type=F32)
    h = (a * jax.nn.sigmoid(a) * b).astype(BF16)
    return jnp.dot(h, w2_ref[...], preferred_element_type=F32)


def _dense_ffn_kernel(x_ref, g_ref, wg_ref, wu_ref, w2_ref, out_ref, hn_ref, acc_ref):
    j = pl.program_id(1)

    @pl.when(j == 0)
    def _():
        hn_ref[...] = _rms(x_ref[...], g_ref[...]).astype(BF16)
        acc_ref[...] = jnp.zeros_like(acc_ref)

    acc_ref[...] += _swiglu_chunk(hn_ref[...], wg_ref, wu_ref, w2_ref)

    @pl.when(j == pl.num_programs(1) - 1)
    def _():
        out_ref[...] = x_ref[...] + acc_ref[...]


def _dense_ffn(x, g, w13, w2, layer, tm, tf):
    T, D = x.shape
    dff = w2.shape[1]
    nj = dff // tf
    return pl.pallas_call(
        _dense_ffn_kernel,
        grid=(T // tm, nj),
        in_specs=[pl.BlockSpec((tm, D), lambda i, j: (i, 0)),
                  pl.BlockSpec((1, D), lambda i, j: (0, 0)),
                  pl.BlockSpec((None, D, tf), lambda i, j: (layer, 0, j)),
                  pl.BlockSpec((None, D, tf), lambda i, j: (layer, 0, j + nj)),
                  pl.BlockSpec((None, tf, D), lambda i, j: (layer, j, 0))],
        out_specs=pl.BlockSpec((tm, D), lambda i, j: (i, 0)),
        out_shape=jax.ShapeDtypeStruct((T, D), F32),
        scratch_shapes=[pltpu.VMEM((tm, D), BF16), pltpu.VMEM((tm, D), F32)],
        compiler_params=_cparams("parallel", "arbitrary"),
        name="dense_ffn",
    )(x, g, w13, w13, w2)


def _store_slab(ref, v):
    n, rows = v.shape[0], v.shape[1] // LANES
    for s in range(rows):
        ref[pl.ds(s, n, stride=rows), :] = v[:, s * LANES:(s + 1) * LANES]


def _load_slab(ref, first, n, rows):
    return jnp.concatenate(
        [ref[pl.ds(first * rows + s, n, stride=rows), :] for s in range(rows)], axis=1)


def _slab(row, rows):
    return pl.ds(pl.multiple_of(row * rows, rows), rows)


def _pack_bf16_pairs(v):
    half = v.shape[1] // 2
    bits = pltpu.bitcast(v.astype(BF16).astype(F32), jnp.uint32)
    return (bits[:, half:] & jnp.uint32(0xFFFF0000)) | (bits[:, :half] >> 16)


def _unpack_bf16_pairs(u):
    low = pltpu.bitcast(u << 16, F32)
    high = pltpu.bitcast(u & jnp.uint32(0xFFFF0000), F32)
    return jnp.concatenate([low, high], axis=1).astype(BF16)


def _router_kernel(x_ref, g_ref, wr_ref, hn_ref, route_ref):
    hn = _rms(x_ref[...], g_ref[...])
    _store_slab(hn_ref, _pack_bf16_pairs(hn))
    logits = jnp.dot(hn.astype(BF16), wr_ref[...], preferred_element_type=F32)
    lane = lax.broadcasted_iota(jnp.int32, logits.shape, 1).astype(F32)
    logits = jnp.where(lane < N_EXPERTS, logits, -jnp.inf)
    m1 = jnp.max(logits, axis=-1, keepdims=True)
    i1 = jnp.min(jnp.where(logits == m1, lane, float(LANES)), axis=-1, keepdims=True)
    rest = jnp.where(lane == i1, -jnp.inf, logits)
    m2 = jnp.max(rest, axis=-1, keepdims=True)
    i2 = jnp.min(jnp.where(rest == m2, lane, float(LANES)), axis=-1, keepdims=True)
    e2 = jnp.exp(m2 - m1)
    p1 = 1.0 / (1.0 + e2)
    p2 = e2 * p1
    route_ref[...] = jnp.where(lane == 0, i1, jnp.where(lane == 1, i2,
                     jnp.where(lane == 2, p1, jnp.where(lane == 3, p2, 0.0))))


def _router(x, g, wr, tm):
    T, D = x.shape
    return pl.pallas_call(
        _router_kernel,
        grid=(T // tm,),
        in_specs=[pl.BlockSpec((tm, D), lambda i: (i, 0)),
                  pl.BlockSpec((1, D), lambda i: (0, 0)),
                  pl.BlockSpec((D, LANES), lambda i: (0, 0))],
        out_specs=[pl.BlockSpec((tm * PACKED_ROWS, LANES), lambda i: (i, 0)),
                   pl.BlockSpec((tm, LANES), lambda i: (i, 0))],
        out_shape=[jax.ShapeDtypeStruct((T * PACKED_ROWS, LANES), jnp.uint32),
                   jax.ShapeDtypeStruct((T, LANES), F32)],
        compiler_params=_cparams("parallel"),
        name="router",
    )(x, g, wr)


def _row_copy(src_ref, src_row, dst_ref, dst_row, sem, rows):
    return pltpu.make_async_copy(src_ref.at[_slab(src_row, rows)], dst_ref.at[_slab(dst_row, rows)],
                                 sem)


def _wait_rows(src_ref, dst_ref, n, sem, rows):
    span = pl.ds(0, n * rows)
    pltpu.make_async_copy(src_ref.at[span], dst_ref.at[span], sem).wait()


def _expert_ffn_kernel(src_ref, te_ref, tv_ref, hn_hbm, wg_ref, wu_ref, w2_ref, y_ref,
                       xbuf, hn_ref, acc_ref, sem, *, tm, nj):
    i, j = pl.program_id(0), pl.program_id(1)
    nt = pl.num_programs(0)
    valid = tv_ref[i] > 0
    slot, next_slot = i % 2, (i + 1) % 2
    share = tm // nj

    def gather(tile, into, r):
        _row_copy(hn_hbm, src_ref[tile * tm + r], xbuf.at[into], r, sem.at[into],
                  PACKED_ROWS).start()

    def gather_share_of_next_tile():
        tile = jnp.minimum(i + 1, nt - 1)
        for r in range(share):
            gather(tile, next_slot, j * share + r)

    @pl.when((i == 0) & (j == 0))
    def _():
        def body(r, c):
            gather(0, 0, r)
            return c

        lax.fori_loop(0, tm, body, 0, unroll=DMA_ISSUE_UNROLL)

    @pl.when(j == 0)
    def _():
        _wait_rows(hn_hbm, xbuf.at[slot], tm, sem.at[slot], PACKED_ROWS)

    @pl.when(valid & (j == 0))
    def _():
        hn_ref[...] = _unpack_bf16_pairs(_load_slab(xbuf.at[slot], 0, tm, PACKED_ROWS))
        acc_ref[...] = jnp.zeros_like(acc_ref)

    @pl.when(valid)
    def _():
        gather_share_of_next_tile()
        acc_ref[...] += _swiglu_chunk(hn_ref[...], wg_ref, wu_ref, w2_ref)

    @pl.when(jnp.logical_not(valid))
    def _():
        gather_share_of_next_tile()

    @pl.when(valid & (j == nj - 1))
    def _():
        _store_slab(y_ref, acc_ref[...])

    @pl.when(jnp.logical_not(valid) & (j == nj - 1))
    def _():
        y_ref[...] = jnp.zeros_like(y_ref)

    @pl.when((i == nt - 1) & (j == nj - 1))
    def _():
        _wait_rows(hn_hbm, xbuf.at[next_slot], tm, sem.at[next_slot], PACKED_ROWS)


def _expert_ffn(hn_packed, src_token, tile_expert, tile_valid, w13, w2, layer, tm, tf):
    n_rows = src_token.shape[0]
    D = D_MODEL
    dff = w2.shape[2]
    nj = dff // tf
    jj = lambda i, j, tv: jnp.where(tv[i] > 0, j, nj - 1)
    return pl.pallas_call(
        partial(_expert_ffn_kernel, tm=tm, nj=nj),
        grid_spec=pltpu.PrefetchScalarGridSpec(
            num_scalar_prefetch=3,
            grid=(n_rows // tm, nj),
            in_specs=[pl.BlockSpec(memory_space=pl.ANY),
                      pl.BlockSpec((None, None, D, tf),
                                   lambda i, j, src, te, tv: (layer, te[i], 0, jj(i, j, tv))),
                      pl.BlockSpec((None, None, D, tf),
                                   lambda i, j, src, te, tv: (layer, te[i], 0, jj(i, j, tv) + nj)),
                      pl.BlockSpec((None, None, tf, D),
                                   lambda i, j, src, te, tv: (layer, te[i], jj(i, j, tv), 0))],
            out_specs=pl.BlockSpec((tm * SLAB_ROWS, LANES), lambda i, j, src, te, tv: (i, 0)),
            scratch_shapes=[pltpu.VMEM((2, tm * PACKED_ROWS, LANES), jnp.uint32),
                            pltpu.VMEM((tm, D), BF16), pltpu.VMEM((tm, D), F32),
                            pltpu.SemaphoreType.DMA((2,))]),
        out_shape=jax.ShapeDtypeStruct((n_rows * SLAB_ROWS, LANES), F32),
        compiler_params=_cparams("arbitrary", "arbitrary"),
        name="expert_ffn",
    )(src_token, tile_expert, tile_valid, hn_packed, w13, w13, w2)


def _combine_kernel(pos_ref, x_ref, route_ref, g_ref, y_ref, out_ref, buf, sem, *, tm,
                    final_norm):
    i = pl.program_id(0)

    def issue(tile, slot):
        base = tile * tm * TOP_K

        def start(r, c):
            for kk in range(TOP_K):
                _row_copy(y_ref, pos_ref[base + r * TOP_K + kk], buf.at[slot], kk * tm + r,
                          sem.at[slot], SLAB_ROWS).start()
            return c

        lax.fori_loop(0, tm, start, 0, unroll=DMA_ISSUE_UNROLL)

    @pl.when(i == 0)
    def _():
        issue(0, 0)

    @pl.when(i + 1 < pl.num_programs(0))
    def _():
        issue(i + 1, (i + 1) % 2)

    slot = i % 2
    _wait_rows(y_ref, buf.at[slot], tm * TOP_K, sem.at[slot], SLAB_ROWS)
    route = route_ref[...]
    acc = x_ref[...]
    for kk in range(TOP_K):
        gate = route[:, TOP_K + kk:TOP_K + kk + 1]
        acc = acc + gate * _load_slab(buf.at[slot], kk * tm, tm, SLAB_ROWS)
    out_ref[...] = _rms(acc, g_ref[...]) if final_norm else acc


def _combine(x, route, y, pos, final_g, final_norm, tm):
    T, D = x.shape
    return pl.pallas_call(
        partial(_combine_kernel, tm=tm, final_norm=final_norm),
        grid_spec=pltpu.PrefetchScalarGridSpec(
            num_scalar_prefetch=1,
            grid=(T // tm,),
            in_specs=[pl.BlockSpec((tm, D), lambda i, *_: (i, 0)),
                      pl.BlockSpec((tm, LANES), lambda i, *_: (i, 0)),
                      pl.BlockSpec((1, D), lambda i, *_: (0, 0)),
                      pl.BlockSpec(memory_space=pl.ANY)],
            out_specs=pl.BlockSpec((tm, D), lambda i, *_: (i, 0)),
            scratch_shapes=[pltpu.VMEM((2, tm * TOP_K * SLAB_ROWS, LANES), F32),
                            pltpu.SemaphoreType.DMA((2,))]),
        out_shape=jax.ShapeDtypeStruct((T, D), F32),
        compiler_params=_cparams("arbitrary"),
        name="combine",
    )(pos, x, route, final_g, y)


def _routing_tables(expert_ids, tm, n_tiles):
    e_flat = expert_ids.reshape(-1)
    onehot = (e_flat[:, None] == jnp.arange(N_EXPERTS, dtype=jnp.int32)[None, :]).astype(jnp.int32)
    csum = jnp.cumsum(onehot, axis=0)
    counts = csum[-1]
    rank = jnp.sum(onehot * (csum - 1), axis=1)
    padded = ((counts + tm - 1) // tm) * tm
    ends = jnp.cumsum(padded)
    starts = ends - padded
    pos = (jnp.sum(onehot * starts[None, :], axis=1) + rank).astype(jnp.int32)

    tile_start = jnp.arange(n_tiles, dtype=jnp.int32) * tm
    tile_valid = (tile_start < ends[-1]).astype(jnp.int32)
    te = jnp.sum((tile_start[:, None] >= ends[None, :]).astype(jnp.int32), axis=1)
    last_e = jnp.sum((ends[-1] - 1 >= ends).astype(jnp.int32))
    tile_expert = jnp.where(tile_valid > 0, te, last_e).astype(jnp.int32)

    token = jnp.arange(e_flat.shape[0], dtype=jnp.int32) // TOP_K
    src_token = jnp.zeros((n_tiles * tm,), jnp.int32).at[pos].set(token, unique_indices=True)
    return pos, src_token, tile_expert, tile_valid


def _moe_ffn(x, g, wr, w13, w2, layer, final_g, final_norm, tm, tme, tf):
    T, D = x.shape
    n_tiles = (T * TOP_K) // tme + N_EXPERTS
    hn_packed, route = _router(x, g, wr, tm)
    expert_ids = route[:, :TOP_K].astype(jnp.int32)
    pos, src_token, tile_expert, tile_valid = _routing_tables(expert_ids, tme, n_tiles)
    y = _expert_ffn(hn_packed, src_token, tile_expert, tile_valid, w13, w2, layer, tme, tf)
    return _combine(x, route, y, pos, final_g, final_norm, tm)


def _final_norm_kernel(x_ref, g_ref, o_ref):
    o_ref[...] = _rms(x_ref[...], g_ref[...])


def _final_norm(x, g, tm):
    T, D = x.shape
    return pl.pallas_call(
        _final_norm_kernel,
        grid=(T // tm,),
        in_specs=[pl.BlockSpec((tm, D), lambda i: (i, 0)), pl.BlockSpec((1, D), lambda i: (0, 0))],
        out_specs=pl.BlockSpec((tm, D), lambda i: (i, 0)),
        out_shape=jax.ShapeDtypeStruct((T, D), F32),
        compiler_params=_cparams("parallel"),
        name="final_norm",
    )(x, g)


def _block_diag(pw):
    G, C, _ = pw.shape
    eye = jnp.eye(G, dtype=pw.dtype)
    return (eye[:, None, :, None] * pw[:, :, None, :]).reshape(G * C, G * C)


def kernel(x, mem, rel_bias, mem_norm_g, mix_norm_g, w_in, conv_w, pool_w, pool_scale, w_out,
           xattn_norm_g, wq_x, wkv_x, wo_x, ffn_norm_g, w13_dense, w2_dense, router_w, w13_moe,
           w2_moe, final_norm_g):
    B, S, D = x.shape
    depth = w_in.shape[0]
    M = mem.shape[1]
    T = B * S
    tm = min(ROW_TILE, S)
    assert D == D_MODEL and S % (max(DILATIONS) * Q_BLOCK) == 0 and S % tm == 0

    xf = x.reshape(T, D)
    row2 = lambda v: v.reshape(1, -1)
    q_scale = jnp.where(jnp.arange(w_in.shape[-1]) < ATTN_WIDTH,
                        ATTN_HEAD_DIM ** -0.5 * LOG2_E, 1.0)
    w_in_b = (w_in * q_scale).astype(BF16)
    biases = [_band_bias(rel_bias * LOG2_E, d) for d in DILATIONS]
    wr_pad = jnp.pad(router_w, ((0, 0), (0, 0), (0, LANES - N_EXPERTS))).astype(BF16)
    w_out_b, wkv_b, wo_b = (w.astype(BF16) for w in (w_out, wkv_x, wo_x))
    wq_b = (wq_x * (XATTN_HEAD_DIM ** -0.5 * LOG2_E)).astype(BF16)
    w13_dense_b, w2_dense_b = w13_dense.astype(BF16), w2_dense.astype(BF16)
    w13_moe_b, w2_moe_b = w13_moe.astype(BF16), w2_moe.astype(BF16)

    mem_f = mem.reshape(B * M, D)
    for layer in range(depth):
        *qkvs, rest = _in_proj(xf, row2(mix_norm_g[layer]), w_in_b, layer, B, S, tm)
        branch = []
        for qkv, bias, d in zip(qkvs, biases, DILATIONS):
            o, lse = _window_attn(qkv.reshape(B * d, S // d, qkv.shape[-1]), bias, ATTN_Q_BLOCKS)
            branch.append((o.reshape(B, d, S // d, -1), lse.reshape(B, d, S // d, -1)))
        kv = _norm_proj(mem_f, row2(mem_norm_g), wkv_b, layer, BF16, min(tm, B * M))
        xf = _mixer_out(xf, [o for o, _ in branch], [l for _, l in branch], rest,
                        conv_w[layer], _block_diag(pool_w[layer]).astype(BF16),
                        row2(pool_scale[layer]), w_out_b, row2(xattn_norm_g[layer]), wq_b,
                        kv.reshape(B, M, 2 * XATTN_WIDTH), wo_b, layer, S, tm)

        g = row2(ffn_norm_g[layer])
        last = layer == depth - 1
        if layer % 2 == 0:
            xf = _dense_ffn(xf, g, w13_dense_b, w2_dense_b, layer // 2, tm, DENSE_FF_TILE)
            if last:
                xf = _final_norm(xf, row2(final_norm_g), tm)
        else:
            xf = _moe_ffn(xf, g, wr_pad[layer // 2], w13_moe_b, w2_moe_b, layer // 2,
                          row2(final_norm_g), last, tm, EXPERT_ROW_TILE, EXPERT_FF_TILE)
    return xf.reshape(B, S, D)
```

```python
from functools import partial

import numpy as np
import jax
import jax.numpy as jnp
from jax import lax
from jax.experimental import pallas as pl
from jax.experimental.pallas import tpu as pltpu

F32 = jnp.float32
BF16 = jnp.bfloat16

D_MODEL = 1024
ATTN_HEADS = 8
ATTN_HEAD_DIM = 64
ATTN_WIDTH = ATTN_HEADS * ATTN_HEAD_DIM
POOL_WIDTH = 256
POOL_GROUP_DIM = 64
POOL_WINDOWS = (2, 4, 8, 16)
POOL_HALO = 16
CONV_WIDTH = 256
REST_WIDTH = POOL_WIDTH + 3 * CONV_WIDTH
DILATIONS = (1, 4, 16)
WINDOW_STEPS = 128
Q_BLOCK = 128
NEG_INF = -1e30
REL_BUCKETS = 32
REL_MAX_DIST = 128
XATTN_HEADS = 4
XATTN_HEAD_DIM = 128
XATTN_WIDTH = XATTN_HEADS * XATTN_HEAD_DIM
N_EXPERTS = 8
TOP_K = 2
EPS = 1e-6

LANES = 128
MXU_COLS = 256
SLAB_ROWS = D_MODEL // LANES
PACKED_ROWS = SLAB_ROWS // 2
ROW_TILE = 512
EXPERT_ROW_TILE = 512
LOG2_E = 1.4426950408889634
LN_2 = 0.6931471805599453
ATTN_Q_BLOCKS = 4
DMA_ISSUE_UNROLL = 4
VMEM_LIMIT = 56 * 1024 * 1024


def _cparams(*sem):
    return pltpu.CompilerParams(dimension_semantics=sem, vmem_limit_bytes=VMEM_LIMIT)


def _rms(x, g):
    return x * lax.rsqrt(jnp.mean(x * x, axis=-1, keepdims=True) + EPS) * g


def _norm_proj_kernel(x_ref, g_ref, w_ref, o_ref):
    xn = _rms(x_ref[...], g_ref[...]).astype(BF16)
    o_ref[...] = jnp.dot(xn, w_ref[...], preferred_element_type=F32).astype(o_ref.dtype)


def _layer_spec(w, layer):
    return pl.BlockSpec((None,) + w.shape[1:], lambda *_: (layer, 0, 0))


def _norm_proj(x, g, w, layer, out_dtype, tm):
    T, D = x.shape
    N = w.shape[-1]
    return pl.pallas_call(
        _norm_proj_kernel,
        grid=(T // tm,),
        in_specs=[pl.BlockSpec((tm, D), lambda i: (i, 0)),
                  pl.BlockSpec((1, D), lambda i: (0, 0)),
                  _layer_spec(w, layer)],
        out_specs=pl.BlockSpec((tm, N), lambda i: (i, 0)),
        out_shape=jax.ShapeDtypeStruct((T, N), out_dtype),
        compiler_params=_cparams("parallel"),
        name="norm_proj",
    )(x, g, w)


def _in_proj_kernel(x_ref, g_ref, w_ref, *refs):
    nd = len(DILATIONS)
    qkv_refs, rest_ref, acc_refs = refs[:nd], refs[nd], refs[nd + 1:]
    per, tm, _ = acc_refs[0].shape
    nq = len(acc_refs) * MXU_COLS
    xn = _rms(x_ref[...], g_ref[...]).astype(BF16)
    for nb, acc_ref in enumerate(acc_refs):
        cols = slice(nb * MXU_COLS, (nb + 1) * MXU_COLS)
        acc = jnp.dot(xn, w_ref[:, cols], preferred_element_type=F32)
        for c in range(per):
            acc_ref[c] = acc[:, c * LANES:(c + 1) * LANES]
        for d, qkv_ref in zip(DILATIONS, qkv_refs):
            for r in range(d):
                rows = [acc_ref[c, pl.ds(r, tm // d, stride=d), :] for c in range(per)]
                qkv_ref[0, r, :, cols] = jnp.concatenate(rows, axis=1).astype(BF16)
    rest_ref[...] = jnp.dot(xn, w_ref[:, nq:], preferred_element_type=F32)


def _in_proj(x, g, w, layer, B, S, tm):
    T, D = x.shape
    N = w.shape[-1]
    nq = 3 * ATTN_WIDTH
    tps = S // tm
    res_spec = lambda d: pl.BlockSpec((1, d, tm // d, nq), lambda i: (i // tps, 0, i % tps, 0))
    return pl.pallas_call(
        _in_proj_kernel,
        grid=(T // tm,),
        in_specs=[pl.BlockSpec((tm, D), lambda i: (i, 0)),
                  pl.BlockSpec((1, D), lambda i: (0, 0)),
                  _layer_spec(w, layer)],
        out_specs=[res_spec(d) for d in DILATIONS]
                  + [pl.BlockSpec((tm, N - nq), lambda i: (i, 0))],
        out_shape=[jax.ShapeDtypeStruct((B, d, S // d, nq), BF16) for d in DILATIONS]
                  + [jax.ShapeDtypeStruct((T, N - nq), F32)],
        scratch_shapes=[pltpu.VMEM((MXU_COLS // LANES, tm, LANES), F32)] * (nq // MXU_COLS),
        compiler_params=_cparams("parallel"),
        name="in_proj",
    )(x, g, w)


def _t5_causal_bucket(dist):
    max_exact = REL_BUCKETS // 2
    d = np.maximum(dist, 1).astype(np.float32)
    large = max_exact + (np.log(d / max_exact) / np.log(REL_MAX_DIST / max_exact)
                         * (REL_BUCKETS - max_exact)).astype(np.int32)
    large = np.minimum(large, REL_BUCKETS - 1)
    return np.where(dist < max_exact, dist, large).astype(np.int32)


def _band_bias(rel_bias, dil):
    row = np.arange(Q_BLOCK)[:, None]
    col = np.arange(2 * Q_BLOCK)[None, :]
    steps = Q_BLOCK + row - col
    valid = (steps >= 0) & (steps <= WINDOW_STEPS)
    bucket = _t5_causal_bucket(dil * np.clip(steps, 0, WINDOW_STEPS))
    onehot = (bucket[..., None] == np.arange(REL_BUCKETS)).astype(np.float32)
    b = jnp.einsum('qcb,bh->hqc', jnp.asarray(onehot), rel_bias.astype(F32),
                   precision=lax.Precision.HIGHEST)
    masks = np.stack([valid, valid & (col >= Q_BLOCK)])
    return jnp.where(jnp.asarray(masks)[:, None], b[None], NEG_INF)


def _window_attn_kernel(q_ref, kp_ref, k_ref, vp_ref, v_ref, bias_ref, o_ref, lse_ref, *, nq):
    first_step = jnp.where(pl.program_id(1) == 0, 1, 0)
    lane = lax.broadcasted_iota(jnp.int32, (Q_BLOCK, LANES), 1)
    low_half = lane < ATTN_HEAD_DIM
    Q = Q_BLOCK
    for j in range(nq):
        rows = slice(j * Q, (j + 1) * Q)
        if j == 0:
            k2 = jnp.concatenate([kp_ref[0], k_ref[0, :Q, :]], axis=0)
            v2 = jnp.concatenate([vp_ref[0], v_ref[0, :Q, :]], axis=0)
            table = first_step
        else:
            k2 = k_ref[0, (j - 1) * Q:(j + 1) * Q, :]
            v2 = v_ref[0, (j - 1) * Q:(j + 1) * Q, :]
            table = 0
        lse = jnp.zeros((Q, LANES), F32)
        for g in range(ATTN_WIDTH // LANES):
            cols = slice(g * LANES, (g + 1) * LANES)
            qg, kg, vg = q_ref[0, rows, cols], k2[:, cols], v2[:, cols]
            og = None
            for half in range(2):
                h = 2 * g + half
                keep = low_half if half == 0 else jnp.logical_not(low_half)
                qh = jnp.where(keep, qg, jnp.zeros_like(qg))
                s = lax.dot_general(qh, kg, (((1,), (1,)), ((), ())),
                                    preferred_element_type=F32)
                s = s + bias_ref[table, h]
                m = jnp.max(s, axis=-1, keepdims=True)
                p = jnp.exp2(s - m)
                l = jnp.sum(p, axis=-1, keepdims=True)
                o = jnp.dot(p.astype(BF16), vg, preferred_element_type=F32) * (1.0 / l)
                og = o if half == 0 else jnp.where(low_half, og, o)
                lse = jnp.where(lane == h, (m + jnp.log2(l)) * LN_2, lse)
            o_ref[0, rows, cols] = og.astype(o_ref.dtype)
        lse_ref[0, rows, :] = lse


def _window_attn(qkv, bias, nq):
    N, L, _ = qkv.shape
    W = ATTN_WIDTH
    nq = min(nq, L // Q_BLOCK)
    step = nq * Q_BLOCK
    prev = lambda i: jnp.maximum(i * nq - 1, 0)
    return pl.pallas_call(
        partial(_window_attn_kernel, nq=nq),
        grid=(N, L // step),
        in_specs=[pl.BlockSpec((1, step, W), lambda n, i: (n, i, 0)),
                  pl.BlockSpec((1, Q_BLOCK, W), lambda n, i: (n, prev(i), 1)),
                  pl.BlockSpec((1, step, W), lambda n, i: (n, i, 1)),
                  pl.BlockSpec((1, Q_BLOCK, W), lambda n, i: (n, prev(i), 2)),
                  pl.BlockSpec((1, step, W), lambda n, i: (n, i, 2)),
                  pl.BlockSpec(bias.shape, lambda n, i: (0, 0, 0, 0))],
        out_specs=[pl.BlockSpec((1, step, W), lambda n, i: (n, i, 0)),
                   pl.BlockSpec((1, step, LANES), lambda n, i: (n, i, 0))],
        out_shape=[jax.ShapeDtypeStruct((N, L, W), BF16),
                   jax.ShapeDtypeStruct((N, L, LANES), F32)],
        compiler_params=_cparams("parallel", "parallel"),
        name="window_attn",
    )(qkv, qkv, qkv, qkv, qkv, bias)


def _token_order(src_ref, scratch_ref, d):
    if d == 1:
        return src_ref[0, 0].astype(F32)
    nchunk, n, _ = scratch_ref.shape
    for r in range(d):
        part = src_ref[0, r].astype(F32)
        for c in range(nchunk):
            scratch_ref[c, pl.ds(r, n // d, stride=d), :] = part[:, c * LANES:(c + 1) * LANES]
    return jnp.concatenate([scratch_ref[c] for c in range(nchunk)], axis=1)


def _memory_xattn(x, g_ref, wq_ref, kv_ref, wo_ref):
    xn = _rms(x, g_ref[...]).astype(BF16)
    q = jnp.dot(xn, wq_ref[...], preferred_element_type=F32).astype(BF16)
    heads = []
    for h in range(XATTN_HEADS):
        sl = slice(h * XATTN_HEAD_DIM, (h + 1) * XATTN_HEAD_DIM)
        k = kv_ref[0, :, sl]
        v = kv_ref[0, :, XATTN_WIDTH + h * XATTN_HEAD_DIM:XATTN_WIDTH + (h + 1) * XATTN_HEAD_DIM]
        s = lax.dot_general(q[:, sl], k, (((1,), (1,)), ((), ())), preferred_element_type=F32)
        m = jnp.max(s, axis=-1, keepdims=True)
        p = jnp.exp2(s - m)
        l = jnp.sum(p, axis=-1, keepdims=True)
        heads.append(jnp.dot(p.astype(BF16), v, preferred_element_type=F32) * (1.0 / l))
    o = jnp.concatenate(heads, axis=1).astype(BF16)
    return x + jnp.dot(o, wo_ref[...], preferred_element_type=F32)


def _mixer_out_kernel(x_ref, o1_ref, o2_ref, o3_ref, l1_ref, l2_ref, l3_ref, rest_ref, halo_ref,
                      convw_ref, poolw_ref, pscale_ref, pwin_ref, expand_ref, wout_ref,
                      xg_ref, wq_ref, kv_ref, wo_ref, out_ref,
                      os2_ref, os3_ref, ls2_ref, ls3_ref, *, tm, tiles_per_seq):
    it = pl.program_id(0) % tiles_per_seq
    lses = [_token_order(r, s, d)
            for r, s, d in zip((l1_ref, l2_ref, l3_ref), (None, ls2_ref, ls3_ref), DILATIONS)]
    outs = [_token_order(r, s, d)
            for r, s, d in zip((o1_ref, o2_ref, o3_ref), (None, os2_ref, os3_ref), DILATIONS)]
    lm = jnp.maximum(jnp.maximum(lses[0], lses[1]), lses[2])
    es = [jnp.exp(l - lm) for l in lses]
    inv = 1.0 / (es[0] + es[1] + es[2])
    y_attn = jnp.zeros((tm, ATTN_WIDTH), F32)
    for e, o in zip(es, outs):
        w = e * inv
        w_hi = w.astype(BF16)
        w_lo = (w - w_hi.astype(F32)).astype(BF16)
        wide = jnp.dot(jnp.concatenate([w_hi, w_lo], axis=1), expand_ref[...],
                       preferred_element_type=F32)
        y_attn = y_attn + wide * o
    y_attn = y_attn.astype(BF16)

    rest = rest_ref[...]
    halo = jnp.where(it == 0, 0.0, halo_ref[...])
    ext = jnp.concatenate([halo, rest], axis=0)

    u = ext[:, :POOL_WIDTH]
    s2 = u + pltpu.roll(u, 1, 0)
    s4 = s2 + pltpu.roll(s2, 2, 0)
    s8 = s4 + pltpu.roll(s4, 4, 0)
    s16 = s8 + pltpu.roll(s8, 8, 0)
    col = lax.broadcasted_iota(jnp.int32, u.shape, 1)
    g = POOL_GROUP_DIM
    wsum = jnp.where(col < g, s2, jnp.where(col < 2 * g, s4, jnp.where(col < 3 * g, s8, s16)))
    pos = it * tm + lax.broadcasted_iota(jnp.int32, (tm, POOL_WIDTH), 0)
    count = jnp.minimum((pos + 1).astype(F32), pwin_ref[...])
    pooled = wsum[POOL_HALO:] / count - u[POOL_HALO:]
    y_pool = jnp.dot(pooled.astype(BF16), poolw_ref[...], preferred_element_type=F32)
    y_pool = (y_pool * pscale_ref[...]).astype(BF16)

    b_gate = rest[:, POOL_WIDTH:POOL_WIDTH + CONV_WIDTH]
    cu = ext[:, POOL_WIDTH + CONV_WIDTH:POOL_WIDTH + 2 * CONV_WIDTH] * ext[:, POOL_WIDTH + 2 * CONV_WIDTH:]
    y = (convw_ref[0:1, :] * cu + convw_ref[1:2, :] * pltpu.roll(cu, 1, 0)
         + convw_ref[2:3, :] * pltpu.roll(cu, 2, 0))
    y_conv = (b_gate * y[POOL_HALO:]).astype(BF16)

    a0, a1 = ATTN_WIDTH, ATTN_WIDTH + POOL_WIDTH
    acc = jnp.dot(y_attn, wout_ref[:a0, :], preferred_element_type=F32)
    acc += jnp.dot(y_pool, wout_ref[a0:a1, :], preferred_element_type=F32)
    acc += jnp.dot(y_conv, wout_ref[a1:, :], preferred_element_type=F32)
    out_ref[...] = _memory_xattn(x_ref[...] + acc, xg_ref, wq_ref, kv_ref, wo_ref)


def _mixer_out(x, outs, lses, rest, conv_w, pool_w_bd, pool_scale, w_out, xg, wq, kv, wo, layer,
               S, tm):
    T, D = x.shape
    M = kv.shape[1]
    hb = tm // POOL_HALO
    tps = S // tm
    row = lambda i: (i, 0)
    const = lambda i: (0, 0)
    res_spec = lambda d, c: pl.BlockSpec((1, d, tm // d, c), lambda i: (i // tps, 0, i % tps, 0))
    pwin = jnp.asarray(np.repeat(np.array(POOL_WINDOWS, np.float32), POOL_GROUP_DIM)[None, :])
    expand = np.zeros((2, LANES, ATTN_WIDTH), np.float32)
    for h in range(ATTN_HEADS):
        expand[:, h, h * ATTN_HEAD_DIM:(h + 1) * ATTN_HEAD_DIM] = 1.0
    expand = jnp.asarray(expand.reshape(2 * LANES, ATTN_WIDTH), dtype=BF16)
    return pl.pallas_call(
        partial(_mixer_out_kernel, tm=tm, tiles_per_seq=tps),
        grid=(T // tm,),
        scratch_shapes=[pltpu.VMEM((ATTN_WIDTH // LANES, tm, LANES), F32)] * 2
                       + [pltpu.VMEM((1, tm, LANES), F32)] * 2,
        name="mixer_out",
        in_specs=[pl.BlockSpec((tm, D), row)]
                 + [res_spec(d, ATTN_WIDTH) for d in DILATIONS]
                 + [res_spec(d, LANES) for d in DILATIONS]
                 + [pl.BlockSpec((tm, REST_WIDTH), row),
                    pl.BlockSpec((POOL_HALO, REST_WIDTH), lambda i: (jnp.maximum(i * hb - 1, 0), 0)),
                    pl.BlockSpec(conv_w.shape, const),
                    pl.BlockSpec(pool_w_bd.shape, const),
                    pl.BlockSpec((1, POOL_WIDTH), const),
                    pl.BlockSpec((1, POOL_WIDTH), const),
                    pl.BlockSpec(expand.shape, const),
                    _layer_spec(w_out, layer),
                    pl.BlockSpec((1, D), const),
                    _layer_spec(wq, layer),
                    pl.BlockSpec((1, M, 2 * XATTN_WIDTH), lambda i: (i // tps, 0, 0)),
                    _layer_spec(wo, layer)],
        out_specs=pl.BlockSpec((tm, D), row),
        out_shape=jax.ShapeDtypeStruct((T, D), F32),
        compiler_params=_cparams("parallel"),
    )(x, *outs, *lses, rest, rest, conv_w, pool_w_bd, pool_scale, pwin, expand, w_out,
      xg, wq, kv, wo)


def _swiglu(hn, w13_ref, w2_ref):
    dff = w2_ref.shape[0]
    a = jnp.dot(hn, w13_ref[:, :dff], preferred_element_type=F32)
    b = jnp.dot(hn, w13_ref[:, dff:], preferred_element_type=F32)
    h = (a * jax.nn.sigmoid(a) * b).astype(BF16)
    return jnp.dot(h, w2_ref[...], preferred_element_type=F32)


def _dense_ffn_kernel(x_ref, g_ref, w13_ref, w2_ref, out_ref):
    x = x_ref[...]
    out_ref[...] = x + _swiglu(_rms(x, g_ref[...]).astype(BF16), w13_ref, w2_ref)


def _resident_spec(w, layer):
    return pl.BlockSpec((None,) + w.shape[1:], lambda *_: (layer, 0, 0),
                        pipeline_mode=pl.Buffered(1))


def _dense_ffn(x, g, w13, w2, layer, tm):
    T, D = x.shape
    return pl.pallas_call(
        _dense_ffn_kernel,
        grid=(T // tm,),
        in_specs=[pl.BlockSpec((tm, D), lambda i: (i, 0)),
                  pl.BlockSpec((1, D), lambda i: (0, 0)),
                  _resident_spec(w13, layer),
                  _resident_spec(w2, layer)],
        out_specs=pl.BlockSpec((tm, D), lambda i: (i, 0)),
        out_shape=jax.ShapeDtypeStruct((T, D), F32),
        compiler_params=_cparams("parallel"),
        name="dense_ffn",
    )(x, g, w13, w2)


def _store_slab(ref, v):
    n, rows = v.shape[0], v.shape[1] // LANES
    for s in range(rows):
        ref[pl.ds(s, n, stride=rows), :] = v[:, s * LANES:(s + 1) * LANES]


def _load_slab(ref, first, n, rows):
    return jnp.concatenate(
        [ref[pl.ds(first * rows + s, n, stride=rows), :] for s in range(rows)], axis=1)


def _slab(row, rows):
    return pl.ds(pl.multiple_of(row * rows, rows), rows)


def _pack_bf16_pairs(v):
    half = v.shape[1] // 2
    bits = pltpu.bitcast(v.astype(BF16).astype(F32), jnp.uint32)
    return (bits[:, half:] & jnp.uint32(0xFFFF0000)) | (bits[:, :half] >> 16)


def _unpack_bf16_pairs(u):
    low = pltpu.bitcast(u << 16, F32)
    high = pltpu.bitcast(u & jnp.uint32(0xFFFF0000), F32)
    return jnp.concatenate([low, high], axis=1).astype(BF16)


def _router_kernel(x_ref, g_ref, wr_ref, hn_ref, route_ref):
    hn = _rms(x_ref[...], g_ref[...])
    _store_slab(hn_ref, _pack_bf16_pairs(hn))
    logits = jnp.dot(hn.astype(BF16), wr_ref[...], preferred_element_type=F32)
    lane = lax.broadcasted_iota(jnp.int32, logits.shape, 1).astype(F32)
    logits = jnp.where(lane < N_EXPERTS, logits, -jnp.inf)
    m1 = jnp.max(logits, axis=-1, keepdims=True)
    i1 = jnp.min(jnp.where(logits == m1, lane, float(LANES)), axis=-1, keepdims=True)
    rest = jnp.where(lane == i1, -jnp.inf, logits)
    m2 = jnp.max(rest, axis=-1, keepdims=True)
    i2 = jnp.min(jnp.where(rest == m2, lane, float(LANES)), axis=-1, keepdims=True)
    e2 = jnp.exp(m2 - m1)
    p1 = 1.0 / (1.0 + e2)
    p2 = e2 * p1
    route_ref[...] = jnp.where(lane == 0, i1, jnp.where(lane == 1, i2,
                     jnp.where(lane == 2, p1, jnp.where(lane == 3, p2, 0.0))))


def _router(x, g, wr, tm):
    T, D = x.shape
    return pl.pallas_call(
        _router_kernel,
        grid=(T // tm,),
        in_specs=[pl.BlockSpec((tm, D), lambda i: (i, 0)),
                  pl.BlockSpec((1, D), lambda i: (0, 0)),
                  pl.BlockSpec((D, LANES), lambda i: (0, 0))],
        out_specs=[pl.BlockSpec((tm * PACKED_ROWS, LANES), lambda i: (i, 0)),
                   pl.BlockSpec((tm, LANES), lambda i: (i, 0))],
        out_shape=[jax.ShapeDtypeStruct((T * PACKED_ROWS, LANES), jnp.uint32),
                   jax.ShapeDtypeStruct((T, LANES), F32)],
        compiler_params=_cparams("parallel"),
        name="router",
    )(x, g, wr)


def _row_copy(src_ref, src_row, dst_ref, dst_row, sem, rows):
    return pltpu.make_async_copy(src_ref.at[_slab(src_row, rows)], dst_ref.at[_slab(dst_row, rows)],
                                 sem)


def _wait_rows(src_ref, dst_ref, n, sem, rows):
    span = pl.ds(0, n * rows)
    pltpu.make_async_copy(src_ref.at[span], dst_ref.at[span], sem).wait()


def _dispatch_kernel(pos_ref, pad_ref, hn_ref, xs_ref, sem, *, tm):
    i = pl.program_id(0)
    base = i * tm * TOP_K

    def start(r, c):
        for kk in range(TOP_K):
            _row_copy(hn_ref, r, xs_ref, pos_ref[base + r * TOP_K + kk], sem, PACKED_ROWS).start()
        return c

    lax.fori_loop(0, tm, start, 0, unroll=DMA_ISSUE_UNROLL)
    _wait_rows(xs_ref, xs_ref, tm * TOP_K, sem, PACKED_ROWS)

    @pl.when(i == 0)
    def _():
        npad = pad_ref.shape[0]

        def pstart(r, c):
            _row_copy(hn_ref, 0, xs_ref, pad_ref[r], sem, PACKED_ROWS).start()
            return c

        lax.fori_loop(0, npad, pstart, 0, unroll=DMA_ISSUE_UNROLL)
        _wait_rows(xs_ref, xs_ref, npad, sem, PACKED_ROWS)


def _dispatch(hn_slab, pos, pad_rows, n_rows, tm):
    T = hn_slab.shape[0] // PACKED_ROWS
    return pl.pallas_call(
        partial(_dispatch_kernel, tm=tm),
        grid_spec=pltpu.PrefetchScalarGridSpec(
            num_scalar_prefetch=2,
            grid=(T // tm,),
            in_specs=[pl.BlockSpec((tm * PACKED_ROWS, LANES), lambda i, *_: (i, 0))],
            out_specs=pl.BlockSpec(memory_space=pl.ANY),
            scratch_shapes=[pltpu.SemaphoreType.DMA(())]),
        out_shape=jax.ShapeDtypeStruct((n_rows * PACKED_ROWS, LANES), jnp.uint32),
        compiler_params=_cparams("arbitrary"),
        name="dispatch",
    )(pos, pad_rows, hn_slab)


def _expert_ffn_kernel(te_ref, tv_ref, xs_ref, w13_ref, w2_ref, y_ref, *, tm):
    valid = tv_ref[pl.program_id(0)] > 0

    @pl.when(valid)
    def _():
        hn = _unpack_bf16_pairs(_load_slab(xs_ref, 0, tm, PACKED_ROWS))
        _store_slab(y_ref, _swiglu(hn, w13_ref, w2_ref))

    @pl.when(jnp.logical_not(valid))
    def _():
        y_ref[...] = jnp.zeros_like(y_ref)


def _expert_ffn(xs, tile_expert, tile_valid, w13, w2, layer, tm):
    n_rows = xs.shape[0] // PACKED_ROWS
    expert_spec = lambda w: pl.BlockSpec((None, None) + w.shape[2:],
                                         lambda i, te, tv: (layer, te[i], 0, 0),
                                         pipeline_mode=pl.Buffered(1))
    return pl.pallas_call(
        partial(_expert_ffn_kernel, tm=tm),
        grid_spec=pltpu.PrefetchScalarGridSpec(
            num_scalar_prefetch=2,
            grid=(n_rows // tm,),
            in_specs=[pl.BlockSpec((tm * PACKED_ROWS, LANES), lambda i, te, tv: (i, 0)),
                      expert_spec(w13), expert_spec(w2)],
            out_specs=pl.BlockSpec((tm * SLAB_ROWS, LANES), lambda i, te, tv: (i, 0))),
        out_shape=jax.ShapeDtypeStruct((n_rows * SLAB_ROWS, LANES), F32),
        compiler_params=_cparams("parallel"),
        name="expert_ffn",
    )(tile_expert, tile_valid, xs, w13, w2)


def _combine_kernel(pos_ref, x_ref, route_ref, g_ref, y_ref, out_ref, buf, sem, *, tm,
                    final_norm):
    i = pl.program_id(0)

    def issue(tile, slot):
        base = tile * tm * TOP_K

        def start(r, c):
            for kk in range(TOP_K):
                _row_copy(y_ref, pos_ref[base + r * TOP_K + kk], buf.at[slot], kk * tm + r,
                          sem.at[slot], SLAB_ROWS).start()
            return c

        lax.fori_loop(0, tm, start, 0, unroll=DMA_ISSUE_UNROLL)

    @pl.when(i == 0)
    def _():
        issue(0, 0)

    @pl.when(i + 1 < pl.num_programs(0))
    def _():
        issue(i + 1, (i + 1) % 2)

    slot = i % 2
    _wait_rows(y_ref, buf.at[slot], tm * TOP_K, sem.at[slot], SLAB_ROWS)
    route = route_ref[...]
    acc = x_ref[...]
    for kk in range(TOP_K):
        gate = route[:, TOP_K + kk:TOP_K + kk + 1]
        acc = acc + gate * _load_slab(buf.at[slot], kk * tm, tm, SLAB_ROWS)
    out_ref[...] = _rms(acc, g_ref[...]) if final_norm else acc


def _combine(x, route, y, pos, final_g, final_norm, tm):
    T, D = x.shape
    return pl.pallas_call(
        partial(_combine_kernel, tm=tm, final_norm=final_norm),
        grid_spec=pltpu.PrefetchScalarGridSpec(
            num_scalar_prefetch=1,
            grid=(T // tm,),
            in_specs=[pl.BlockSpec((tm, D), lambda i, *_: (i, 0)),
                      pl.BlockSpec((tm, LANES), lambda i, *_: (i, 0)),
                      pl.BlockSpec((1, D), lambda i, *_: (0, 0)),
                      pl.BlockSpec(memory_space=pl.ANY)],
            out_specs=pl.BlockSpec((tm, D), lambda i, *_: (i, 0)),
            scratch_shapes=[pltpu.VMEM((2, tm * TOP_K * SLAB_ROWS, LANES), F32),
                            pltpu.SemaphoreType.DMA((2,))]),
        out_shape=jax.ShapeDtypeStruct((T, D), F32),
        compiler_params=_cparams("arbitrary"),
        name="combine",
    )(pos, x, route, final_g, y)


def _routing_tables(expert_ids, tm, n_tiles):
    e_flat = expert_ids.reshape(-1)
    onehot = (e_flat[:, None] == jnp.arange(N_EXPERTS, dtype=jnp.int32)[None, :]).astype(jnp.int32)
    csum = jnp.cumsum(onehot, axis=0)
    counts = csum[-1]
    rank = jnp.sum(onehot * (csum - 1), axis=1)
    padded = ((counts + tm - 1) // tm) * tm
    ends = jnp.cumsum(padded)
    starts = ends - padded
    pos = (jnp.sum(onehot * starts[None, :], axis=1) + rank).astype(jnp.int32)

    tile_start = jnp.arange(n_tiles, dtype=jnp.int32) * tm
    tile_valid = (tile_start < ends[-1]).astype(jnp.int32)
    te = jnp.sum((tile_start[:, None] >= ends[None, :]).astype(jnp.int32), axis=1)
    last_e = jnp.sum((ends[-1] - 1 >= ends).astype(jnp.int32))
    tile_expert = jnp.where(tile_valid > 0, te, last_e).astype(jnp.int32)

    npad_e = padded - counts
    pcum = jnp.cumsum(npad_e)
    k = jnp.arange(N_EXPERTS * tm, dtype=jnp.int32)
    ek = jnp.sum((k[:, None] >= pcum[None, :]).astype(jnp.int32), axis=1)
    ohk = (ek[:, None] == jnp.arange(N_EXPERTS, dtype=jnp.int32)[None, :]).astype(jnp.int32)
    in_group = jnp.sum(ohk * (starts + counts - (pcum - npad_e))[None, :], axis=1) + k
    pad = jnp.where(ek < N_EXPERTS, in_group, ends[-1] + k - pcum[-1]).astype(jnp.int32)
    return pos, tile_expert, tile_valid, pad


def _moe_ffn(x, g, wr, w13, w2, layer, final_g, final_norm, tm, tme):
    T, D = x.shape
    n_tiles = (T * TOP_K) // tme + N_EXPERTS
    hn_slab, route = _router(x, g, wr, tm)
    expert_ids = route[:, :TOP_K].astype(jnp.int32)
    pos, tile_expert, tile_valid, pad = _routing_tables(expert_ids, tme, n_tiles)
    xs = _dispatch(hn_slab, pos, pad, n_tiles * tme, tm)
    y = _expert_ffn(xs, tile_expert, tile_valid, w13, w2, layer, tme)
    return _combine(x, route, y, pos, final_g, final_norm, tm)


def _final_norm_kernel(x_ref, g_ref, o_ref):
    o_ref[...] = _rms(x_ref[...], g_ref[...])


def _final_norm(x, g, tm):
    T, D = x.shape
    return pl.pallas_call(
        _final_norm_kernel,
        grid=(T // tm,),
        in_specs=[pl.BlockSpec((tm, D), lambda i: (i, 0)), pl.BlockSpec((1, D), lambda i: (0, 0))],
        out_specs=pl.BlockSpec((tm, D), lambda i: (i, 0)),
        out_shape=jax.ShapeDtypeStruct((T, D), F32),
        compiler_params=_cparams("parallel"),
        name="final_norm",
    )(x, g)


def _block_diag(pw):
    G, C, _ = pw.shape
    eye = jnp.eye(G, dtype=pw.dtype)
    return (eye[:, None, :, None] * pw[:, :, None, :]).reshape(G * C, G * C)


def kernel(x, mem, rel_bias, mem_norm_g, mix_norm_g, w_in, conv_w, pool_w, pool_scale, w_out,
           xattn_norm_g, wq_x, wkv_x, wo_x, ffn_norm_g, w13_dense, w2_dense, router_w, w13_moe,
           w2_moe, final_norm_g):
    B, S, D = x.shape
    depth = w_in.shape[0]
    M = mem.shape[1]
    T = B * S
    tm = min(ROW_TILE, S)
    assert D == D_MODEL and S % (max(DILATIONS) * Q_BLOCK) == 0 and S % tm == 0

    xf = x.reshape(T, D)
    row2 = lambda v: v.reshape(1, -1)
    q_scale = jnp.where(jnp.arange(w_in.shape[-1]) < ATTN_WIDTH,
                        ATTN_HEAD_DIM ** -0.5 * LOG2_E, 1.0)
    w_in_b = (w_in * q_scale).astype(BF16)
    biases = [_band_bias(rel_bias * LOG2_E, d) for d in DILATIONS]
    wr_pad = jnp.pad(router_w, ((0, 0), (0, 0), (0, LANES - N_EXPERTS))).astype(BF16)
    w_out_b, wkv_b, wo_b = (w.astype(BF16) for w in (w_out, wkv_x, wo_x))
    wq_b = (wq_x * (XATTN_HEAD_DIM ** -0.5 * LOG2_E)).astype(BF16)
    w13_dense_b, w2_dense_b = w13_dense.astype(BF16), w2_dense.astype(BF16)
    w13_moe_b, w2_moe_b = w13_moe.astype(BF16), w2_moe.astype(BF16)

    mem_f = mem.reshape(B * M, D)
    for layer in range(depth):
        *qkvs, rest = _in_proj(xf, row2(mix_norm_g[layer]), w_in_b, layer, B, S, tm)
        branch = []
        for qkv, bias, d in zip(qkvs, biases, DILATIONS):
            o, lse = _window_attn(qkv.reshape(B * d, S // d, qkv.shape[-1]), bias, ATTN_Q_BLOCKS)
            branch.append((o.reshape(B, d, S // d, -1), lse.reshape(B, d, S // d, -1)))
        kv = _norm_proj(mem_f, row2(mem_norm_g), wkv_b, layer, BF16, min(tm, B * M))
        xf = _mixer_out(xf, [o for o, _ in branch], [l for _, l in branch], rest,
                        conv_w[layer], _block_diag(pool_w[layer]).astype(BF16),
                        row2(pool_scale[layer]), w_out_b, row2(xattn_norm_g[layer]), wq_b,
                        kv.reshape(B, M, 2 * XATTN_WIDTH), wo_b, layer, S, tm)

        g = row2(ffn_norm_g[layer])
        last = layer == depth - 1
        if layer % 2 == 0:
            xf = _dense_ffn(xf, g, w13_dense_b, w2_dense_b, layer // 2, tm)
            if last:
                xf = _final_norm(xf, row2(final_norm_g), tm)
        else:
            xf = _moe_ffn(xf, g, wr_pad[layer // 2], w13_moe_b, w2_moe_b, layer // 2,
                          row2(final_norm_g), last, tm, EXPERT_ROW_TILE)
    return xf.reshape(B, S, D)
```

```python
from functools import partial

import numpy as np
import jax
import jax.numpy as jnp
from jax import lax
from jax.experimental import pallas as pl
from jax.experimental.pallas import tpu as pltpu

F32 = jnp.float32
BF16 = jnp.bfloat16

D_MODEL = 1024
ATTN_HEADS = 8
ATTN_HEAD_DIM = 64
ATTN_WIDTH = ATTN_HEADS * ATTN_HEAD_DIM
POOL_WIDTH = 256
POOL_GROUP_DIM = 64
POOL_WINDOWS = (2, 4, 8, 16)
POOL_HALO = 16
CONV_WIDTH = 256
REST_WIDTH = POOL_WIDTH + 3 * CONV_WIDTH
DILATIONS = (1, 4, 16)
WINDOW_STEPS = 128
Q_BLOCK = 128
NEG_INF = -1e30
REL_BUCKETS = 32
REL_MAX_DIST = 128
XATTN_HEADS = 4
XATTN_HEAD_DIM = 128
XATTN_WIDTH = XATTN_HEADS * XATTN_HEAD_DIM
N_EXPERTS = 8
TOP_K = 2
EPS = 1e-6

LANES = 128
MXU_COLS = 256
SLAB_ROWS = D_MODEL // LANES
PACKED_ROWS = SLAB_ROWS // 2
ROW_TILE = 1024
FFN_ROW_TILE = 512
LOG2_E = 1.4426950408889634
LN_2 = 0.6931471805599453
ATTN_Q_BLOCKS = 4
DMA_ISSUE_UNROLL = 4
VMEM_LIMIT = 56 * 1024 * 1024


def _cparams(*sem):
    return pltpu.CompilerParams(dimension_semantics=sem, vmem_limit_bytes=VMEM_LIMIT)


def _rms(x, g):
    return x * lax.rsqrt(jnp.mean(x * x, axis=-1, keepdims=True) + EPS) * g


def _norm_proj_kernel(x_ref, g_ref, w_ref, o_ref):
    xn = _rms(x_ref[...], g_ref[...]).astype(BF16)
    o_ref[...] = jnp.dot(xn, w_ref[...], preferred_element_type=F32).astype(o_ref.dtype)


def _layer_spec(w, layer):
    return pl.BlockSpec((None,) + w.shape[1:], lambda *_: (layer, 0, 0),
                        pipeline_mode=pl.Buffered(1))


def _norm_proj(x, g, w, layer, out_dtype, tm):
    T, D = x.shape
    N = w.shape[-1]
    return pl.pallas_call(
        _norm_proj_kernel,
        grid=(T // tm,),
        in_specs=[pl.BlockSpec((tm, D), lambda i: (i, 0)),
                  pl.BlockSpec((1, D), lambda i: (0, 0)),
                  _layer_spec(w, layer)],
        out_specs=pl.BlockSpec((tm, N), lambda i: (i, 0)),
        out_shape=jax.ShapeDtypeStruct((T, N), out_dtype),
        compiler_params=_cparams("parallel"),
        name="norm_proj",
    )(x, g, w)


def _in_proj_kernel(x_ref, g_ref, w_ref, *refs):
    nd = len(DILATIONS)
    qkv_refs, rest_ref, acc_refs = refs[:nd], refs[nd], refs[nd + 1:]
    per, tm, _ = acc_refs[0].shape
    nq = len(acc_refs) * MXU_COLS
    xn = _rms(x_ref[...], g_ref[...]).astype(BF16)
    for nb, acc_ref in enumerate(acc_refs):
        cols = slice(nb * MXU_COLS, (nb + 1) * MXU_COLS)
        acc = jnp.dot(xn, w_ref[:, cols], preferred_element_type=F32)
        for c in range(per):
            acc_ref[c] = acc[:, c * LANES:(c + 1) * LANES]
        for d, qkv_ref in zip(DILATIONS, qkv_refs):
            for r in range(d):
                rows = [acc_ref[c, pl.ds(r, tm // d, stride=d), :] for c in range(per)]
                qkv_ref[0, r, :, cols] = jnp.concatenate(rows, axis=1).astype(BF16)
    rest_ref[...] = jnp.dot(xn, w_ref[:, nq:], preferred_element_type=F32)


def _in_proj(x, g, w, layer, B, S, tm):
    T, D = x.shape
    N = w.shape[-1]
    nq = 3 * ATTN_WIDTH
    tps = S // tm
    res_spec = lambda d: pl.BlockSpec((1, d, tm // d, nq), lambda i: (i // tps, 0, i % tps, 0))
    return pl.pallas_call(
        _in_proj_kernel,
        grid=(T // tm,),
        in_specs=[pl.BlockSpec((tm, D), lambda i: (i, 0)),
                  pl.BlockSpec((1, D), lambda i: (0, 0)),
                  _layer_spec(w, layer)],
        out_specs=[res_spec(d) for d in DILATIONS]
                  + [pl.BlockSpec((tm, N - nq), lambda i: (i, 0))],
        out_shape=[jax.ShapeDtypeStruct((B, d, S // d, nq), BF16) for d in DILATIONS]
                  + [jax.ShapeDtypeStruct((T, N - nq), F32)],
        scratch_shapes=[pltpu.VMEM((MXU_COLS // LANES, tm, LANES), F32)] * (nq // MXU_COLS),
        compiler_params=_cparams("parallel"),
        name="in_proj",
    )(x, g, w)


def _t5_causal_bucket(dist):
    max_exact = REL_BUCKETS // 2
    d = np.maximum(dist, 1).astype(np.float32)
    large = max_exact + (np.log(d / max_exact) / np.log(REL_MAX_DIST / max_exact)
                         * (REL_BUCKETS - max_exact)).astype(np.int32)
    large = np.minimum(large, REL_BUCKETS - 1)
    return np.where(dist < max_exact, dist, large).astype(np.int32)


def _band_bias(rel_bias, dil):
    row = np.arange(Q_BLOCK)[:, None]
    col = np.arange(2 * Q_BLOCK)[None, :]
    steps = Q_BLOCK + row - col
    valid = (steps >= 0) & (steps <= WINDOW_STEPS)
    bucket = _t5_causal_bucket(dil * np.clip(steps, 0, WINDOW_STEPS))
    onehot = (bucket[..., None] == np.arange(REL_BUCKETS)).astype(np.float32)
    b = jnp.einsum('qcb,bh->hqc', jnp.asarray(onehot), rel_bias.astype(F32),
                   precision=lax.Precision.HIGHEST)
    masks = np.stack([valid, valid & (col >= Q_BLOCK)])
    return jnp.where(jnp.asarray(masks)[:, None], b[None], NEG_INF)


def _window_attn_kernel(q_ref, kp_ref, k_ref, vp_ref, v_ref, bias_ref, o_ref, lse_ref, *, nq):
    first_step = jnp.where(pl.program_id(1) == 0, 1, 0)
    lane = lax.broadcasted_iota(jnp.int32, (Q_BLOCK, LANES), 1)
    low_half = lane < ATTN_HEAD_DIM
    Q = Q_BLOCK
    for j in range(nq):
        rows = slice(j * Q, (j + 1) * Q)
        if j == 0:
            k2 = jnp.concatenate([kp_ref[0], k_ref[0, :Q, :]], axis=0)
            v2 = jnp.concatenate([vp_ref[0], v_ref[0, :Q, :]], axis=0)
            table = first_step
        else:
            k2 = k_ref[0, (j - 1) * Q:(j + 1) * Q, :]
            v2 = v_ref[0, (j - 1) * Q:(j + 1) * Q, :]
            table = 0
        lse = jnp.zeros((Q, LANES), F32)
        for g in range(ATTN_WIDTH // LANES):
            cols = slice(g * LANES, (g + 1) * LANES)
            qg, kg, vg = q_ref[0, rows, cols], k2[:, cols], v2[:, cols]
            og = None
            for half in range(2):
                h = 2 * g + half
                keep = low_half if half == 0 else jnp.logical_not(low_half)
                qh = jnp.where(keep, qg, jnp.zeros_like(qg))
                s = lax.dot_general(qh, kg, (((1,), (1,)), ((), ())),
                                    preferred_element_type=F32)
                s = s + bias_ref[table, h]
                m = jnp.max(s, axis=-1, keepdims=True)
                p = jnp.exp2(s - m)
                l = jnp.sum(p, axis=-1, keepdims=True)
                o = jnp.dot(p.astype(BF16), vg, preferred_element_type=F32) * (1.0 / l)
                og = o if half == 0 else jnp.where(low_half, og, o)
                lse = jnp.where(lane == h, (m + jnp.log2(l)) * LN_2, lse)
            o_ref[0, rows, cols] = og.astype(o_ref.dtype)
        lse_ref[0, rows, :] = lse


def _window_attn(qkv, bias, nq):
    N, L, _ = qkv.shape
    W = ATTN_WIDTH
    nq = min(nq, L // Q_BLOCK)
    step = nq * Q_BLOCK
    prev = lambda i: jnp.maximum(i * nq - 1, 0)
    return pl.pallas_call(
        partial(_window_attn_kernel, nq=nq),
        grid=(N, L // step),
        in_specs=[pl.BlockSpec((1, step, W), lambda n, i: (n, i, 0)),
                  pl.BlockSpec((1, Q_BLOCK, W), lambda n, i: (n, prev(i), 1)),
                  pl.BlockSpec((1, step, W), lambda n, i: (n, i, 1)),
                  pl.BlockSpec((1, Q_BLOCK, W), lambda n, i: (n, prev(i), 2)),
                  pl.BlockSpec((1, step, W), lambda n, i: (n, i, 2)),
                  pl.BlockSpec(bias.shape, lambda n, i: (0, 0, 0, 0))],
        out_specs=[pl.BlockSpec((1, step, W), lambda n, i: (n, i, 0)),
                   pl.BlockSpec((1, step, LANES), lambda n, i: (n, i, 0))],
        out_shape=[jax.ShapeDtypeStruct((N, L, W), BF16),
                   jax.ShapeDtypeStruct((N, L, LANES), F32)],
        compiler_params=_cparams("parallel", "parallel"),
        name="window_attn",
    )(qkv, qkv, qkv, qkv, qkv, bias)


def _token_order(src_ref, scratch_ref, d):
    if d == 1:
        return src_ref[0, 0].astype(F32)
    nchunk, n, _ = scratch_ref.shape
    for r in range(d):
        part = src_ref[0, r].astype(F32)
        for c in range(nchunk):
            scratch_ref[c, pl.ds(r, n // d, stride=d), :] = part[:, c * LANES:(c + 1) * LANES]
    return jnp.concatenate([scratch_ref[c] for c in range(nchunk)], axis=1)


def _memory_xattn(x, g_ref, wq_ref, kv_ref, wo_ref):
    xn = _rms(x, g_ref[...]).astype(BF16)
    q = jnp.dot(xn, wq_ref[...], preferred_element_type=F32).astype(BF16)
    heads = []
    for h in range(XATTN_HEADS):
        sl = slice(h * XATTN_HEAD_DIM, (h + 1) * XATTN_HEAD_DIM)
        k = kv_ref[0, :, sl]
        v = kv_ref[0, :, XATTN_WIDTH + h * XATTN_HEAD_DIM:XATTN_WIDTH + (h + 1) * XATTN_HEAD_DIM]
        s = lax.dot_general(q[:, sl], k, (((1,), (1,)), ((), ())), preferred_element_type=F32)
        m = jnp.max(s, axis=-1, keepdims=True)
        p = jnp.exp2(s - m)
        l = jnp.sum(p, axis=-1, keepdims=True)
        heads.append(jnp.dot(p.astype(BF16), v, preferred_element_type=F32) * (1.0 / l))
    o = jnp.concatenate(heads, axis=1).astype(BF16)
    return x + jnp.dot(o, wo_ref[...], preferred_element_type=F32)


def _mixer_out_kernel(x_ref, o1_ref, o2_ref, o3_ref, l1_ref, l2_ref, l3_ref, rest_ref, halo_ref,
                      convw_ref, poolw_ref, pscale_ref, pwin_ref, expand_ref, wout_ref,
                      xg_ref, wq_ref, kv_ref, wo_ref, out_ref,
                      os2_ref, os3_ref, ls2_ref, ls3_ref, *, tm, tiles_per_seq):
    it = pl.program_id(0) % tiles_per_seq
    lses = [_token_order(r, s, d)
            for r, s, d in zip((l1_ref, l2_ref, l3_ref), (None, ls2_ref, ls3_ref), DILATIONS)]
    outs = [_token_order(r, s, d)
            for r, s, d in zip((o1_ref, o2_ref, o3_ref), (None, os2_ref, os3_ref), DILATIONS)]
    lm = jnp.maximum(jnp.maximum(lses[0], lses[1]), lses[2])
    es = [jnp.exp(l - lm) for l in lses]
    inv = 1.0 / (es[0] + es[1] + es[2])
    y_attn = jnp.zeros((tm, ATTN_WIDTH), F32)
    for e, o in zip(es, outs):
        w = e * inv
        w_hi = w.astype(BF16)
        w_lo = (w - w_hi.astype(F32)).astype(BF16)
        wide = jnp.dot(jnp.concatenate([w_hi, w_lo], axis=1), expand_ref[...],
                       preferred_element_type=F32)
        y_attn = y_attn + wide * o
    y_attn = y_attn.astype(BF16)

    rest = rest_ref[...]
    halo = jnp.where(it == 0, 0.0, halo_ref[...])
    ext = jnp.concatenate([halo, rest], axis=0)

    u = ext[:, :POOL_WIDTH]
    s2 = u + pltpu.roll(u, 1, 0)
    s4 = s2 + pltpu.roll(s2, 2, 0)
    s8 = s4 + pltpu.roll(s4, 4, 0)
    s16 = s8 + pltpu.roll(s8, 8, 0)
    col = lax.broadcasted_iota(jnp.int32, u.shape, 1)
    g = POOL_GROUP_DIM
    wsum = jnp.where(col < g, s2, jnp.where(col < 2 * g, s4, jnp.where(col < 3 * g, s8, s16)))
    pos = it * tm + lax.broadcasted_iota(jnp.int32, (tm, POOL_WIDTH), 0)
    count = jnp.minimum((pos + 1).astype(F32), pwin_ref[...])
    pooled = wsum[POOL_HALO:] / count - u[POOL_HALO:]
    y_pool = jnp.dot(pooled.astype(BF16), poolw_ref[...], preferred_element_type=F32)
    y_pool = (y_pool * pscale_ref[...]).astype(BF16)

    b_gate = rest[:, POOL_WIDTH:POOL_WIDTH + CONV_WIDTH]
    cu = ext[:, POOL_WIDTH + CONV_WIDTH:POOL_WIDTH + 2 * CONV_WIDTH] * ext[:, POOL_WIDTH + 2 * CONV_WIDTH:]
    y = (convw_ref[0:1, :] * cu + convw_ref[1:2, :] * pltpu.roll(cu, 1, 0)
         + convw_ref[2:3, :] * pltpu.roll(cu, 2, 0))
    y_conv = (b_gate * y[POOL_HALO:]).astype(BF16)

    a0, a1 = ATTN_WIDTH, ATTN_WIDTH + POOL_WIDTH
    acc = jnp.dot(y_attn, wout_ref[:a0, :], preferred_element_type=F32)
    acc += jnp.dot(y_pool, wout_ref[a0:a1, :], preferred_element_type=F32)
    acc += jnp.dot(y_conv, wout_ref[a1:, :], preferred_element_type=F32)
    out_ref[...] = _memory_xattn(x_ref[...] + acc, xg_ref, wq_ref, kv_ref, wo_ref)


def _mixer_out(x, outs, lses, rest, conv_w, pool_w_bd, pool_scale, w_out, xg, wq, kv, wo, layer,
               S, tm):
    T, D = x.shape
    M = kv.shape[1]
    hb = tm // POOL_HALO
    tps = S // tm
    row = lambda i: (i, 0)
    const = lambda i: (0, 0)
    res_spec = lambda d, c: pl.BlockSpec((1, d, tm // d, c), lambda i: (i // tps, 0, i % tps, 0))
    pwin = jnp.asarray(np.repeat(np.array(POOL_WINDOWS, np.float32), POOL_GROUP_DIM)[None, :])
    expand = np.zeros((2, LANES, ATTN_WIDTH), np.float32)
    for h in range(ATTN_HEADS):
        expand[:, h, h * ATTN_HEAD_DIM:(h + 1) * ATTN_HEAD_DIM] = 1.0
    expand = jnp.asarray(expand.reshape(2 * LANES, ATTN_WIDTH), dtype=BF16)
    return pl.pallas_call(
        partial(_mixer_out_kernel, tm=tm, tiles_per_seq=tps),
        grid=(T // tm,),
        scratch_shapes=[pltpu.VMEM((ATTN_WIDTH // LANES, tm, LANES), F32)] * 2
                       + [pltpu.VMEM((1, tm, LANES), F32)] * 2,
        name="mixer_out",
        in_specs=[pl.BlockSpec((tm, D), row)]
                 + [res_spec(d, ATTN_WIDTH) for d in DILATIONS]
                 + [res_spec(d, LANES) for d in DILATIONS]
                 + [pl.BlockSpec((tm, REST_WIDTH), row),
                    pl.BlockSpec((POOL_HALO, REST_WIDTH), lambda i: (jnp.maximum(i * hb - 1, 0), 0)),
                    pl.BlockSpec(conv_w.shape, const),
                    pl.BlockSpec(pool_w_bd.shape, const),
                    pl.BlockSpec((1, POOL_WIDTH), const),
                    pl.BlockSpec((1, POOL_WIDTH), const),
                    pl.BlockSpec(expand.shape, const),
                    _layer_spec(w_out, layer),
                    pl.BlockSpec((1, D), const),
                    _layer_spec(wq, layer),
                    pl.BlockSpec((1, M, 2 * XATTN_WIDTH), lambda i: (i // tps, 0, 0)),
                    _layer_spec(wo, layer)],
        out_specs=pl.BlockSpec((tm, D), row),
        out_shape=jax.ShapeDtypeStruct((T, D), F32),
        compiler_params=_cparams("parallel"),
    )(x, *outs, *lses, rest, rest, conv_w, pool_w_bd, pool_scale, pwin, expand, w_out,
      xg, wq, kv, wo)


def _swiglu(hn, w13_ref, w2_ref):
    dff = w2_ref.shape[0]
    a = jnp.dot(hn, w13_ref[:, :dff], preferred_element_type=F32)
    b = jnp.dot(hn, w13_ref[:, dff:], preferred_element_type=F32)
    h = (a * jax.nn.sigmoid(a) * b).astype(BF16)
    return jnp.dot(h, w2_ref[...], preferred_element_type=F32)


def _dense_ffn_kernel(x_ref, g_ref, w13_ref, w2_ref, out_ref):
    x = x_ref[...]
    out_ref[...] = x + _swiglu(_rms(x, g_ref[...]).astype(BF16), w13_ref, w2_ref)


def _dense_ffn(x, g, w13, w2, layer, tm):
    T, D = x.shape
    return pl.pallas_call(
        _dense_ffn_kernel,
        grid=(T // tm,),
        in_specs=[pl.BlockSpec((tm, D), lambda i: (i, 0)),
                  pl.BlockSpec((1, D), lambda i: (0, 0)),
                  _layer_spec(w13, layer),
                  _layer_spec(w2, layer)],
        out_specs=pl.BlockSpec((tm, D), lambda i: (i, 0)),
        out_shape=jax.ShapeDtypeStruct((T, D), F32),
        compiler_params=_cparams("parallel"),
        name="dense_ffn",
    )(x, g, w13, w2)


def _store_slab(ref, v):
    n, rows = v.shape[0], v.shape[1] // LANES
    for s in range(rows):
        ref[pl.ds(s, n, stride=rows), :] = v[:, s * LANES:(s + 1) * LANES]


def _load_slab(ref, first, n, rows):
    return jnp.concatenate(
        [ref[pl.ds(first * rows + s, n, stride=rows), :] for s in range(rows)], axis=1)


def _slab(row, rows):
    return pl.ds(pl.multiple_of(row * rows, rows), rows)


def _pack_bf16_pairs(v):
    half = v.shape[1] // 2
    bits = pltpu.bitcast(v.astype(BF16).astype(F32), jnp.uint32)
    return (bits[:, half:] & jnp.uint32(0xFFFF0000)) | (bits[:, :half] >> 16)


def _unpack_bf16_pairs(u):
    low = pltpu.bitcast(u << 16, F32)
    high = pltpu.bitcast(u & jnp.uint32(0xFFFF0000), F32)
    return jnp.concatenate([low, high], axis=1).astype(BF16)


def _router_kernel(x_ref, g_ref, wr_ref, hn_ref, route_ref):
    hn = _rms(x_ref[...], g_ref[...])
    _store_slab(hn_ref, _pack_bf16_pairs(hn))
    logits = jnp.dot(hn.astype(BF16), wr_ref[...], preferred_element_type=F32)
    lane = lax.broadcasted_iota(jnp.int32, logits.shape, 1).astype(F32)
    logits = jnp.where(lane < N_EXPERTS, logits, -jnp.inf)
    m1 = jnp.max(logits, axis=-1, keepdims=True)
    i1 = jnp.min(jnp.where(logits == m1, lane, float(LANES)), axis=-1, keepdims=True)
    rest = jnp.where(lane == i1, -jnp.inf, logits)
    m2 = jnp.max(rest, axis=-1, keepdims=True)
    i2 = jnp.min(jnp.where(rest == m2, lane, float(LANES)), axis=-1, keepdims=True)
    e2 = jnp.exp(m2 - m1)
    p1 = 1.0 / (1.0 + e2)
    p2 = e2 * p1
    route_ref[...] = jnp.where(lane == 0, i1, jnp.where(lane == 1, i2,
                     jnp.where(lane == 2, p1, jnp.where(lane == 3, p2, 0.0))))


def _router(x, g, wr, tm):
    T, D = x.shape
    return pl.pallas_call(
        _router_kernel,
        grid=(T // tm,),
        in_specs=[pl.BlockSpec((tm, D), lambda i: (i, 0)),
                  pl.BlockSpec((1, D), lambda i: (0, 0)),
                  pl.BlockSpec((D, LANES), lambda i: (0, 0))],
        out_specs=[pl.BlockSpec((tm * PACKED_ROWS, LANES), lambda i: (i, 0)),
                   pl.BlockSpec((tm, LANES), lambda i: (i, 0))],
        out_shape=[jax.ShapeDtypeStruct((T * PACKED_ROWS, LANES), jnp.uint32),
                   jax.ShapeDtypeStruct((T, LANES), F32)],
        compiler_params=_cparams("parallel"),
        name="router",
    )(x, g, wr)


def _row_copy(src_ref, src_row, dst_ref, dst_row, sem, rows):
    return pltpu.make_async_copy(src_ref.at[_slab(src_row, rows)], dst_ref.at[_slab(dst_row, rows)],
                                 sem)


def _wait_rows(src_ref, dst_ref, n, sem, rows):
    span = pl.ds(0, n * rows)
    pltpu.make_async_copy(src_ref.at[span], dst_ref.at[span], sem).wait()


def _dispatch_kernel(pos_ref, pad_ref, hn_ref, xs_ref, sem, *, tm):
    i = pl.program_id(0)
    base = i * tm * TOP_K

    def start(r, c):
        for kk in range(TOP_K):
            _row_copy(hn_ref, r, xs_ref, pos_ref[base + r * TOP_K + kk], sem, PACKED_ROWS).start()
        return c

    lax.fori_loop(0, tm, start, 0, unroll=DMA_ISSUE_UNROLL)
    _wait_rows(xs_ref, xs_ref, tm * TOP_K, sem, PACKED_ROWS)

    @pl.when(i == 0)
    def _():
        npad = pad_ref.shape[0]

        def pstart(r, c):
            _row_copy(hn_ref, 0, xs_ref, pad_ref[r], sem, PACKED_ROWS).start()
            return c

        lax.fori_loop(0, npad, pstart, 0, unroll=DMA_ISSUE_UNROLL)
        _wait_rows(xs_ref, xs_ref, npad, sem, PACKED_ROWS)


def _dispatch(hn_slab, pos, pad_rows, n_rows, tm):
    T = hn_slab.shape[0] // PACKED_ROWS
    return pl.pallas_call(
        partial(_dispatch_kernel, tm=tm),
        grid_spec=pltpu.PrefetchScalarGridSpec(
            num_scalar_prefetch=2,
            grid=(T // tm,),
            in_specs=[pl.BlockSpec((tm * PACKED_ROWS, LANES), lambda i, *_: (i, 0))],
            out_specs=pl.BlockSpec(memory_space=pl.ANY),
            scratch_shapes=[pltpu.SemaphoreType.DMA(())]),
        out_shape=jax.ShapeDtypeStruct((n_rows * PACKED_ROWS, LANES), jnp.uint32),
        compiler_params=_cparams("arbitrary"),
        name="dispatch",
    )(pos, pad_rows, hn_slab)


def _expert_ffn_kernel(te_ref, tv_ref, xs_ref, w13_ref, w2_ref, y_ref, *, tm):
    valid = tv_ref[pl.program_id(0)] > 0

    @pl.when(valid)
    def _():
        hn = _unpack_bf16_pairs(_load_slab(xs_ref, 0, tm, PACKED_ROWS))
        _store_slab(y_ref, _swiglu(hn, w13_ref, w2_ref))

    @pl.when(jnp.logical_not(valid))
    def _():
        y_ref[...] = jnp.zeros_like(y_ref)


def _expert_ffn(xs, tile_expert, tile_valid, w13, w2, layer, tm):
    n_rows = xs.shape[0] // PACKED_ROWS
    expert_spec = lambda w: pl.BlockSpec((None, None) + w.shape[2:],
                                         lambda i, te, tv: (layer, te[i], 0, 0),
                                         pipeline_mode=pl.Buffered(1))
    return pl.pallas_call(
        partial(_expert_ffn_kernel, tm=tm),
        grid_spec=pltpu.PrefetchScalarGridSpec(
            num_scalar_prefetch=2,
            grid=(n_rows // tm,),
            in_specs=[pl.BlockSpec((tm * PACKED_ROWS, LANES), lambda i, te, tv: (i, 0)),
                      expert_spec(w13), expert_spec(w2)],
            out_specs=pl.BlockSpec((tm * SLAB_ROWS, LANES), lambda i, te, tv: (i, 0))),
        out_shape=jax.ShapeDtypeStruct((n_rows * SLAB_ROWS, LANES), F32),
        compiler_params=_cparams("parallel"),
        name="expert_ffn",
    )(tile_expert, tile_valid, xs, w13, w2)


def _combine_kernel(pos_ref, x_ref, route_ref, g_ref, y_ref, out_ref, buf, sem, *, tm,
                    final_norm):
    i = pl.program_id(0)

    def issue(tile, slot):
        base = tile * tm * TOP_K

        def start(r, c):
            for kk in range(TOP_K):
                _row_copy(y_ref, pos_ref[base + r * TOP_K + kk], buf.at[slot], kk * tm + r,
                          sem.at[slot], SLAB_ROWS).start()
            return c

        lax.fori_loop(0, tm, start, 0, unroll=DMA_ISSUE_UNROLL)

    @pl.when(i == 0)
    def _():
        issue(0, 0)

    @pl.when(i + 1 < pl.num_programs(0))
    def _():
        issue(i + 1, (i + 1) % 2)

    slot = i % 2
    _wait_rows(y_ref, buf.at[slot], tm * TOP_K, sem.at[slot], SLAB_ROWS)
    route = route_ref[...]
    acc = x_ref[...]
    for kk in range(TOP_K):
        gate = route[:, TOP_K + kk:TOP_K + kk + 1]
        acc = acc + gate * _load_slab(buf.at[slot], kk * tm, tm, SLAB_ROWS)
    out_ref[...] = _rms(acc, g_ref[...]) if final_norm else acc


def _combine(x, route, y, pos, final_g, final_norm, tm):
    T, D = x.shape
    return pl.pallas_call(
        partial(_combine_kernel, tm=tm, final_norm=final_norm),
        grid_spec=pltpu.PrefetchScalarGridSpec(
            num_scalar_prefetch=1,
            grid=(T // tm,),
            in_specs=[pl.BlockSpec((tm, D), lambda i, *_: (i, 0)),
                      pl.BlockSpec((tm, LANES), lambda i, *_: (i, 0)),
                      pl.BlockSpec((1, D), lambda i, *_: (0, 0)),
                      pl.BlockSpec(memory_space=pl.ANY)],
            out_specs=pl.BlockSpec((tm, D), lambda i, *_: (i, 0)),
            scratch_shapes=[pltpu.VMEM((2, tm * TOP_K * SLAB_ROWS, LANES), F32),
                            pltpu.SemaphoreType.DMA((2,))]),
        out_shape=jax.ShapeDtypeStruct((T, D), F32),
        compiler_params=_cparams("arbitrary"),
        name="combine",
    )(pos, x, route, final_g, y)


def _routing_tables(expert_ids, tm, n_tiles):
    e_flat = expert_ids.reshape(-1)
    onehot = (e_flat[:, None] == jnp.arange(N_EXPERTS, dtype=jnp.int32)[None, :]).astype(jnp.int32)
    csum = jnp.cumsum(onehot, axis=0)
    counts = csum[-1]
    rank = jnp.sum(onehot * (csum - 1), axis=1)
    padded = ((counts + tm - 1) // tm) * tm
    ends = jnp.cumsum(padded)
    starts = ends - padded
    pos = (jnp.sum(onehot * starts[None, :], axis=1) + rank).astype(jnp.int32)

    tile_start = jnp.arange(n_tiles, dtype=jnp.int32) * tm
    tile_valid = (tile_start < ends[-1]).astype(jnp.int32)
    te = jnp.sum((tile_start[:, None] >= ends[None, :]).astype(jnp.int32), axis=1)
    last_e = jnp.sum((ends[-1] - 1 >= ends).astype(jnp.int32))
    tile_expert = jnp.where(tile_valid > 0, te, last_e).astype(jnp.int32)

    npad_e = padded - counts
    pcum = jnp.cumsum(npad_e)
    k = jnp.arange(N_EXPERTS * tm, dtype=jnp.int32)
    ek = jnp.sum((k[:, None] >= pcum[None, :]).astype(jnp.int32), axis=1)
    ohk = (ek[:, None] == jnp.arange(N_EXPERTS, dtype=jnp.int32)[None, :]).astype(jnp.int32)
    in_group = jnp.sum(ohk * (starts + counts - (pcum - npad_e))[None, :], axis=1) + k
    pad = jnp.where(ek < N_EXPERTS, in_group, ends[-1] + k - pcum[-1]).astype(jnp.int32)
    return pos, tile_expert, tile_valid, pad


def _moe_ffn(x, g, wr, w13, w2, layer, final_g, final_norm, tm, tme):
    T, D = x.shape
    n_tiles = (T * TOP_K) // tme + N_EXPERTS
    hn_slab, route = _router(x, g, wr, tm)
    expert_ids = route[:, :TOP_K].astype(jnp.int32)
    pos, tile_expert, tile_valid, pad = _routing_tables(expert_ids, tme, n_tiles)
    xs = _dispatch(hn_slab, pos, pad, n_tiles * tme, tm)
    y = _expert_ffn(xs, tile_expert, tile_valid, w13, w2, layer, tme)
    return _combine(x, route, y, pos, final_g, final_norm, tm)


def _final_norm_kernel(x_ref, g_ref, o_ref):
    o_ref[...] = _rms(x_ref[...], g_ref[...])


def _final_norm(x, g, tm):
    T, D = x.shape
    return pl.pallas_call(
        _final_norm_kernel,
        grid=(T // tm,),
        in_specs=[pl.BlockSpec((tm, D), lambda i: (i, 0)), pl.BlockSpec((1, D), lambda i: (0, 0))],
        out_specs=pl.BlockSpec((tm, D), lambda i: (i, 0)),
        out_shape=jax.ShapeDtypeStruct((T, D), F32),
        compiler_params=_cparams("parallel"),
        name="final_norm",
    )(x, g)


def _block_diag(pw):
    G, C, _ = pw.shape
    eye = jnp.eye(G, dtype=pw.dtype)
    return (eye[:, None, :, None] * pw[:, :, None, :]).reshape(G * C, G * C)


def kernel(x, mem, rel_bias, mem_norm_g, mix_norm_g, w_in, conv_w, pool_w, pool_scale, w_out,
           xattn_norm_g, wq_x, wkv_x, wo_x, ffn_norm_g, w13_dense, w2_dense, router_w, w13_moe,
           w2_moe, final_norm_g):
    B, S, D = x.shape
    depth = w_in.shape[0]
    M = mem.shape[1]
    T = B * S
    tm = min(ROW_TILE, S)
    assert D == D_MODEL and S % (max(DILATIONS) * Q_BLOCK) == 0 and S % tm == 0

    xf = x.reshape(T, D)
    row2 = lambda v: v.reshape(1, -1)
    q_scale = jnp.where(jnp.arange(w_in.shape[-1]) < ATTN_WIDTH,
                        ATTN_HEAD_DIM ** -0.5 * LOG2_E, 1.0)
    w_in_b = (w_in * q_scale).astype(BF16)
    biases = [_band_bias(rel_bias * LOG2_E, d) for d in DILATIONS]
    wr_pad = jnp.pad(router_w, ((0, 0), (0, 0), (0, LANES - N_EXPERTS))).astype(BF16)
    w_out_b, wkv_b, wo_b = (w.astype(BF16) for w in (w_out, wkv_x, wo_x))
    wq_b = (wq_x * (XATTN_HEAD_DIM ** -0.5 * LOG2_E)).astype(BF16)
    w13_dense_b, w2_dense_b = w13_dense.astype(BF16), w2_dense.astype(BF16)
    w13_moe_b, w2_moe_b = w13_moe.astype(BF16), w2_moe.astype(BF16)

    mem_f = mem.reshape(B * M, D)
    for layer in range(depth):
        *qkvs, rest = _in_proj(xf, row2(mix_norm_g[layer]), w_in_b, layer, B, S, tm)
        branch = []
        for qkv, bias, d in zip(qkvs, biases, DILATIONS):
            o, lse = _window_attn(qkv.reshape(B * d, S // d, qkv.shape[-1]), bias, ATTN_Q_BLOCKS)
            branch.append((o.reshape(B, d, S // d, -1), lse.reshape(B, d, S // d, -1)))
        kv = _norm_proj(mem_f, row2(mem_norm_g), wkv_b, layer, BF16, min(tm, B * M))
        xf = _mixer_out(xf, [o for o, _ in branch], [l for _, l in branch], rest,
                        conv_w[layer], _block_diag(pool_w[layer]).astype(BF16),
                        row2(pool_scale[layer]), w_out_b, row2(xattn_norm_g[layer]), wq_b,
                        kv.reshape(B, M, 2 * XATTN_WIDTH), wo_b, layer, S, tm)

        g = row2(ffn_norm_g[layer])
        last = layer == depth - 1
        if layer % 2 == 0:
            xf = _dense_ffn(xf, g, w13_dense_b, w2_dense_b, layer // 2, min(FFN_ROW_TILE, tm))
            if last:
                xf = _final_norm(xf, row2(final_norm_g), tm)
        else:
            xf = _moe_ffn(xf, g, wr_pad[layer // 2], w13_moe_b, w2_moe_b, layer // 2,
                          row2(final_norm_g), last, tm, FFN_ROW_TILE)
    return xf.reshape(B, S, D)
```

```python
from functools import partial

import numpy as np
import jax
import jax.numpy as jnp
from jax import lax
from jax.experimental import pallas as pl
from jax.experimental.pallas import tpu as pltpu

F32 = jnp.float32
BF16 = jnp.bfloat16

D_MODEL = 1024
ATTN_HEADS = 8
ATTN_HEAD_DIM = 64
ATTN_WIDTH = ATTN_HEADS * ATTN_HEAD_DIM
POOL_WIDTH = 256
POOL_GROUP_DIM = 64
POOL_WINDOWS = (2, 4, 8, 16)
POOL_HALO = 16
CONV_WIDTH = 256
REST_WIDTH = POOL_WIDTH + 3 * CONV_WIDTH
DILATIONS = (1, 4, 16)
WINDOW_STEPS = 128
Q_BLOCK = 128
NEG_INF = -1e30
REL_BUCKETS = 32
REL_MAX_DIST = 128
XATTN_HEADS = 4
XATTN_HEAD_DIM = 128
XATTN_WIDTH = XATTN_HEADS * XATTN_HEAD_DIM
N_EXPERTS = 8
TOP_K = 2
EPS = 1e-6

LANES = 128
MXU_COLS = 256
SLAB_ROWS = D_MODEL // LANES
PACKED_ROWS = SLAB_ROWS // 2
ROW_TILE = 1024
FFN_ROW_TILE = 512
COMBINE_ROW_TILE = 512
MIXER_ROW_TILE = 512
LOG2_E = 1.4426950408889634
LN_2 = 0.6931471805599453
ATTN_Q_BLOCKS = 4
DMA_ISSUE_UNROLL = 4
DMA_QUEUES = 2
VMEM_LIMIT = 56 * 1024 * 1024


def _cparams(*sem):
    return pltpu.CompilerParams(dimension_semantics=sem, vmem_limit_bytes=VMEM_LIMIT)


def _rms(x, g):
    return x * lax.rsqrt(jnp.mean(x * x, axis=-1, keepdims=True) + EPS) * g


def _norm_proj_kernel(x_ref, g_ref, w_ref, o_ref):
    xn = _rms(x_ref[...], g_ref[...]).astype(BF16)
    o_ref[...] = jnp.dot(xn, w_ref[...], preferred_element_type=F32).astype(o_ref.dtype)


def _layer_spec(w, layer):
    return pl.BlockSpec((None,) + w.shape[1:], lambda *_: (layer, 0, 0),
                        pipeline_mode=pl.Buffered(1))


def _norm_proj(x, g, w, layer, out_dtype, tm):
    T, D = x.shape
    N = w.shape[-1]
    return pl.pallas_call(
        _norm_proj_kernel,
        grid=(T // tm,),
        in_specs=[pl.BlockSpec((tm, D), lambda i: (i, 0)),
                  pl.BlockSpec((1, D), lambda i: (0, 0)),
                  _layer_spec(w, layer)],
        out_specs=pl.BlockSpec((tm, N), lambda i: (i, 0)),
        out_shape=jax.ShapeDtypeStruct((T, N), out_dtype),
        compiler_params=_cparams("parallel"),
        name="norm_proj",
    )(x, g, w)


def _in_proj_kernel(x_ref, g_ref, w_ref, *refs):
    nd = len(DILATIONS)
    qkv_refs, rest_ref, acc_refs = refs[:nd], refs[nd], refs[nd + 1:]
    per, tm, _ = acc_refs[0].shape
    nq = len(acc_refs) * MXU_COLS
    xn = _rms(x_ref[...], g_ref[...]).astype(BF16)
    for nb, acc_ref in enumerate(acc_refs):
        cols = slice(nb * MXU_COLS, (nb + 1) * MXU_COLS)
        acc = jnp.dot(xn, w_ref[:, cols], preferred_element_type=F32)
        for c in range(per):
            acc_ref[c] = acc[:, c * LANES:(c + 1) * LANES]
        for d, qkv_ref in zip(DILATIONS, qkv_refs):
            for r in range(d):
                rows = [acc_ref[c, pl.ds(r, tm // d, stride=d), :] for c in range(per)]
                qkv_ref[0, r, :, cols] = jnp.concatenate(rows, axis=1).astype(BF16)
    rest_ref[...] = jnp.dot(xn, w_ref[:, nq:], preferred_element_type=F32)


def _in_proj(x, g, w, layer, B, S, tm):
    T, D = x.shape
    N = w.shape[-1]
    nq = 3 * ATTN_WIDTH
    tps = S // tm
    res_spec = lambda d: pl.BlockSpec((1, d, tm // d, nq), lambda i: (i // tps, 0, i % tps, 0))
    return pl.pallas_call(
        _in_proj_kernel,
        grid=(T // tm,),
        in_specs=[pl.BlockSpec((tm, D), lambda i: (i, 0)),
                  pl.BlockSpec((1, D), lambda i: (0, 0)),
                  _layer_spec(w, layer)],
        out_specs=[res_spec(d) for d in DILATIONS]
                  + [pl.BlockSpec((tm, N - nq), lambda i: (i, 0))],
        out_shape=[jax.ShapeDtypeStruct((B, d, S // d, nq), BF16) for d in DILATIONS]
                  + [jax.ShapeDtypeStruct((T, N - nq), F32)],
        scratch_shapes=[pltpu.VMEM((MXU_COLS // LANES, tm, LANES), F32)] * (nq // MXU_COLS),
        compiler_params=_cparams("parallel"),
        name="in_proj",
    )(x, g, w)


def _t5_causal_bucket(dist):
    max_exact = REL_BUCKETS // 2
    d = np.maximum(dist, 1).astype(np.float32)
    large = max_exact + (np.log(d / max_exact) / np.log(REL_MAX_DIST / max_exact)
                         * (REL_BUCKETS - max_exact)).astype(np.int32)
    large = np.minimum(large, REL_BUCKETS - 1)
    return np.where(dist < max_exact, dist, large).astype(np.int32)


def _band_bias(rel_bias, dil):
    row = np.arange(Q_BLOCK)[:, None]
    col = np.arange(2 * Q_BLOCK)[None, :]
    steps = Q_BLOCK + row - col
    valid = (steps >= 0) & (steps <= WINDOW_STEPS)
    bucket = _t5_causal_bucket(dil * np.clip(steps, 0, WINDOW_STEPS))
    onehot = (bucket[..., None] == np.arange(REL_BUCKETS)).astype(np.float32)
    b = jnp.einsum('qcb,bh->hqc', jnp.asarray(onehot), rel_bias.astype(F32),
                   precision=lax.Precision.HIGHEST)
    masks = np.stack([valid, valid & (col >= Q_BLOCK)])
    return jnp.where(jnp.asarray(masks)[:, None], b[None], NEG_INF)


def _window_attn_kernel(q_ref, kp_ref, k_ref, vp_ref, v_ref, bias_ref, o_ref, lse_ref, *, nq):
    first_step = jnp.where(pl.program_id(1) == 0, 1, 0)
    lane = lax.broadcasted_iota(jnp.int32, (Q_BLOCK, LANES), 1)
    low_half = lane < ATTN_HEAD_DIM
    Q = Q_BLOCK
    for j in range(nq):
        rows = slice(j * Q, (j + 1) * Q)
        if j == 0:
            k2 = jnp.concatenate([kp_ref[0], k_ref[0, :Q, :]], axis=0)
            v2 = jnp.concatenate([vp_ref[0], v_ref[0, :Q, :]], axis=0)
            table = first_step
        else:
            k2 = k_ref[0, (j - 1) * Q:(j + 1) * Q, :]
            v2 = v_ref[0, (j - 1) * Q:(j + 1) * Q, :]
            table = 0
        lse = jnp.zeros((Q, LANES), F32)
        for g in range(ATTN_WIDTH // LANES):
            cols = slice(g * LANES, (g + 1) * LANES)
            qg, kg, vg = q_ref[0, rows, cols], k2[:, cols], v2[:, cols]
            og = None
            for half in range(2):
                h = 2 * g + half
                keep = low_half if half == 0 else jnp.logical_not(low_half)
                qh = jnp.where(keep, qg, jnp.zeros_like(qg))
                s = lax.dot_general(qh, kg, (((1,), (1,)), ((), ())),
                                    preferred_element_type=F32)
                s = s + bias_ref[table, h]
                m = jnp.max(s, axis=-1, keepdims=True)
                p = jnp.exp2(s - m)
                l = jnp.sum(p, axis=-1, keepdims=True)
                o = jnp.dot(p.astype(BF16), vg, preferred_element_type=F32) * (1.0 / l)
                og = o if half == 0 else jnp.where(low_half, og, o)
                lse = jnp.where(lane == h, (m + jnp.log2(l)) * LN_2, lse)
            o_ref[0, rows, cols] = og.astype(o_ref.dtype)
        lse_ref[0, rows, :] = lse


def _window_attn(qkv, bias, nq):
    N, L, _ = qkv.shape
    W = ATTN_WIDTH
    nq = min(nq, L // Q_BLOCK)
    step = nq * Q_BLOCK
    prev = lambda i: jnp.maximum(i * nq - 1, 0)
    return pl.pallas_call(
        partial(_window_attn_kernel, nq=nq),
        grid=(N, L // step),
        in_specs=[pl.BlockSpec((1, step, W), lambda n, i: (n, i, 0)),
                  pl.BlockSpec((1, Q_BLOCK, W), lambda n, i: (n, prev(i), 1)),
                  pl.BlockSpec((1, step, W), lambda n, i: (n, i, 1)),
                  pl.BlockSpec((1, Q_BLOCK, W), lambda n, i: (n, prev(i), 2)),
                  pl.BlockSpec((1, step, W), lambda n, i: (n, i, 2)),
                  pl.BlockSpec(bias.shape, lambda n, i: (0, 0, 0, 0))],
        out_specs=[pl.BlockSpec((1, step, W), lambda n, i: (n, i, 0)),
                   pl.BlockSpec((1, step, LANES), lambda n, i: (n, i, 0))],
        out_shape=[jax.ShapeDtypeStruct((N, L, W), BF16),
                   jax.ShapeDtypeStruct((N, L, LANES), F32)],
        compiler_params=_cparams("parallel", "parallel"),
        name="window_attn",
    )(qkv, qkv, qkv, qkv, qkv, bias)


def _token_order(src_ref, scratch_ref, d):
    if d == 1:
        return src_ref[0, 0].astype(F32)
    nchunk, n, _ = scratch_ref.shape
    for r in range(d):
        part = src_ref[0, r].astype(F32)
        for c in range(nchunk):
            scratch_ref[c, pl.ds(r, n // d, stride=d), :] = part[:, c * LANES:(c + 1) * LANES]
    return jnp.concatenate([scratch_ref[c] for c in range(nchunk)], axis=1)


def _memory_xattn(x, g_ref, wq_ref, kv_ref, wo_ref):
    xn = _rms(x, g_ref[...]).astype(BF16)
    q = jnp.dot(xn, wq_ref[...], preferred_element_type=F32).astype(BF16)
    heads = []
    for h in range(XATTN_HEADS):
        sl = slice(h * XATTN_HEAD_DIM, (h + 1) * XATTN_HEAD_DIM)
        k = kv_ref[0, :, sl]
        v = kv_ref[0, :, XATTN_WIDTH + h * XATTN_HEAD_DIM:XATTN_WIDTH + (h + 1) * XATTN_HEAD_DIM]
        s = lax.dot_general(q[:, sl], k, (((1,), (1,)), ((), ())), preferred_element_type=F32)
        m = jnp.max(s, axis=-1, keepdims=True)
        p = jnp.exp2(s - m)
        l = jnp.sum(p, axis=-1, keepdims=True)
        heads.append(jnp.dot(p.astype(BF16), v, preferred_element_type=F32) * (1.0 / l))
    o = jnp.concatenate(heads, axis=1).astype(BF16)
    return x + jnp.dot(o, wo_ref[...], preferred_element_type=F32)


def _mixer_out_kernel(x_ref, o1_ref, o2_ref, o3_ref, l1_ref, l2_ref, l3_ref, rest_ref, halo_ref,
                      convw_ref, poolw_ref, pscale_ref, pwin_ref, expand_ref, wout_ref,
                      xg_ref, wq_ref, kv_ref, wo_ref, fg_ref, wr_ref, out_ref, *refs,
                      tm, tiles_per_seq, routed):
    os2_ref, os3_ref, ls2_ref, ls3_ref = refs[-4:]
    it = pl.program_id(0) % tiles_per_seq
    lses = [_token_order(r, s, d)
            for r, s, d in zip((l1_ref, l2_ref, l3_ref), (None, ls2_ref, ls3_ref), DILATIONS)]
    outs = [_token_order(r, s, d)
            for r, s, d in zip((o1_ref, o2_ref, o3_ref), (None, os2_ref, os3_ref), DILATIONS)]
    lm = jnp.maximum(jnp.maximum(lses[0], lses[1]), lses[2])
    es = [jnp.exp(l - lm) for l in lses]
    inv = 1.0 / (es[0] + es[1] + es[2])
    y_attn = jnp.zeros((tm, ATTN_WIDTH), F32)
    for e, o in zip(es, outs):
        w = e * inv
        w_hi = w.astype(BF16)
        w_lo = (w - w_hi.astype(F32)).astype(BF16)
        wide = jnp.dot(jnp.concatenate([w_hi, w_lo], axis=1), expand_ref[...],
                       preferred_element_type=F32)
        y_attn = y_attn + wide * o
    y_attn = y_attn.astype(BF16)

    rest = rest_ref[...]
    halo = jnp.where(it == 0, 0.0, halo_ref[...])
    ext = jnp.concatenate([halo, rest], axis=0)

    u = ext[:, :POOL_WIDTH]
    s2 = u + pltpu.roll(u, 1, 0)
    s4 = s2 + pltpu.roll(s2, 2, 0)
    s8 = s4 + pltpu.roll(s4, 4, 0)
    s16 = s8 + pltpu.roll(s8, 8, 0)
    col = lax.broadcasted_iota(jnp.int32, u.shape, 1)
    g = POOL_GROUP_DIM
    wsum = jnp.where(col < g, s2, jnp.where(col < 2 * g, s4, jnp.where(col < 3 * g, s8, s16)))
    pos = it * tm + lax.broadcasted_iota(jnp.int32, (tm, POOL_WIDTH), 0)
    count = jnp.minimum((pos + 1).astype(F32), pwin_ref[...])
    pooled = wsum[POOL_HALO:] / count - u[POOL_HALO:]
    y_pool = jnp.dot(pooled.astype(BF16), poolw_ref[...], preferred_element_type=F32)
    y_pool = (y_pool * pscale_ref[...]).astype(BF16)

    b_gate = rest[:, POOL_WIDTH:POOL_WIDTH + CONV_WIDTH]
    cu = ext[:, POOL_WIDTH + CONV_WIDTH:POOL_WIDTH + 2 * CONV_WIDTH] * ext[:, POOL_WIDTH + 2 * CONV_WIDTH:]
    y = (convw_ref[0:1, :] * cu + convw_ref[1:2, :] * pltpu.roll(cu, 1, 0)
         + convw_ref[2:3, :] * pltpu.roll(cu, 2, 0))
    y_conv = (b_gate * y[POOL_HALO:]).astype(BF16)

    a0, a1 = ATTN_WIDTH, ATTN_WIDTH + POOL_WIDTH
    acc = jnp.dot(y_attn, wout_ref[:a0, :], preferred_element_type=F32)
    acc += jnp.dot(y_pool, wout_ref[a0:a1, :], preferred_element_type=F32)
    acc += jnp.dot(y_conv, wout_ref[a1:, :], preferred_element_type=F32)
    x_new = _memory_xattn(x_ref[...] + acc, xg_ref, wq_ref, kv_ref, wo_ref)
    out_ref[...] = x_new
    if routed:
        _route_tokens(x_new, fg_ref, wr_ref, *refs[:2])


def _mixer_out(x, outs, lses, rest, conv_w, pool_w_bd, pool_scale, w_out, xg, wq, kv, wo, layer,
               ffn_g, wr, routed, S, tm):
    T, D = x.shape
    M = kv.shape[1]
    hb = tm // POOL_HALO
    tps = S // tm
    row = lambda i: (i, 0)
    const = lambda i: (0, 0)
    res_spec = lambda d, c: pl.BlockSpec((1, d, tm // d, c), lambda i: (i // tps, 0, i % tps, 0))
    pwin = jnp.asarray(np.repeat(np.array(POOL_WINDOWS, np.float32), POOL_GROUP_DIM)[None, :])
    expand = np.zeros((2, LANES, ATTN_WIDTH), np.float32)
    for h in range(ATTN_HEADS):
        expand[:, h, h * ATTN_HEAD_DIM:(h + 1) * ATTN_HEAD_DIM] = 1.0
    expand = jnp.asarray(expand.reshape(2 * LANES, ATTN_WIDTH), dtype=BF16)
    out_specs = [pl.BlockSpec((tm, D), row)]
    out_shape = [jax.ShapeDtypeStruct((T, D), F32)]
    if routed:
        out_specs += [pl.BlockSpec((tm * PACKED_ROWS, LANES), row), pl.BlockSpec((tm, LANES), row)]
        out_shape += [jax.ShapeDtypeStruct((T * PACKED_ROWS, LANES), jnp.uint32),
                      jax.ShapeDtypeStruct((T, LANES), F32)]
    res = pl.pallas_call(
        partial(_mixer_out_kernel, tm=tm, tiles_per_seq=tps, routed=routed),
        grid=(T // tm,),
        scratch_shapes=[pltpu.VMEM((ATTN_WIDTH // LANES, tm, LANES), F32)] * 2
                       + [pltpu.VMEM((1, tm, LANES), F32)] * 2,
        name="mixer_out",
        in_specs=[pl.BlockSpec((tm, D), row)]
                 + [res_spec(d, ATTN_WIDTH) for d in DILATIONS]
                 + [res_spec(d, LANES) for d in DILATIONS]
                 + [pl.BlockSpec((tm, REST_WIDTH), row),
                    pl.BlockSpec((POOL_HALO, REST_WIDTH), lambda i: (jnp.maximum(i * hb - 1, 0), 0)),
                    pl.BlockSpec(conv_w.shape, const),
                    pl.BlockSpec(pool_w_bd.shape, const),
                    pl.BlockSpec((1, POOL_WIDTH), const),
                    pl.BlockSpec((1, POOL_WIDTH), const),
                    pl.BlockSpec(expand.shape, const),
                    _layer_spec(w_out, layer),
                    pl.BlockSpec((1, D), const),
                    _layer_spec(wq, layer),
                    pl.BlockSpec((1, M, 2 * XATTN_WIDTH), lambda i: (i // tps, 0, 0)),
                    _layer_spec(wo, layer),
                    pl.BlockSpec((1, D), const),
                    pl.BlockSpec(wr.shape, const)],
        out_specs=out_specs,
        out_shape=out_shape,
        compiler_params=_cparams("parallel"),
    )(x, *outs, *lses, rest, rest, conv_w, pool_w_bd, pool_scale, pwin, expand, w_out,
      xg, wq, kv, wo, ffn_g, wr)
    return res if routed else res[0]


def _swiglu(hn, w13_ref, w2_ref):
    dff = w2_ref.shape[0]
    a = jnp.dot(hn, w13_ref[:, :dff], preferred_element_type=F32)
    b = jnp.dot(hn, w13_ref[:, dff:], preferred_element_type=F32)
    h = (a * jax.nn.sigmoid(a) * b).astype(BF16)
    return jnp.dot(h, w2_ref[...], preferred_element_type=F32)


def _dense_ffn_kernel(x_ref, g_ref, w13_ref, w2_ref, out_ref):
    x = x_ref[...]
    out_ref[...] = x + _swiglu(_rms(x, g_ref[...]).astype(BF16), w13_ref, w2_ref)


def _dense_ffn(x, g, w13, w2, layer, tm):
    T, D = x.shape
    return pl.pallas_call(
        _dense_ffn_kernel,
        grid=(T // tm,),
        in_specs=[pl.BlockSpec((tm, D), lambda i: (i, 0)),
                  pl.BlockSpec((1, D), lambda i: (0, 0)),
                  _layer_spec(w13, layer),
                  _layer_spec(w2, layer)],
        out_specs=pl.BlockSpec((tm, D), lambda i: (i, 0)),
        out_shape=jax.ShapeDtypeStruct((T, D), F32),
        compiler_params=_cparams("parallel"),
        name="dense_ffn",
    )(x, g, w13, w2)


def _store_slab(ref, v):
    n, rows = v.shape[0], v.shape[1] // LANES
    for s in range(rows):
        ref[pl.ds(s, n, stride=rows), :] = v[:, s * LANES:(s + 1) * LANES]


def _load_slab(ref, first, n, rows):
    return jnp.concatenate(
        [ref[pl.ds(first * rows + s, n, stride=rows), :] for s in range(rows)], axis=1)


def _slab(row, rows):
    return pl.ds(pl.multiple_of(row * rows, rows), rows)


def _pack_bf16_pairs(v):
    half = v.shape[1] // 2
    bits = pltpu.bitcast(v.astype(BF16).astype(F32), jnp.uint32)
    return (bits[:, half:] & jnp.uint32(0xFFFF0000)) | (bits[:, :half] >> 16)


def _unpack_bf16_pairs(u):
    low = pltpu.bitcast(u << 16, F32)
    high = pltpu.bitcast(u & jnp.uint32(0xFFFF0000), F32)
    return jnp.concatenate([low, high], axis=1).astype(BF16)


def _route_tokens(x, g_ref, wr_ref, hn_ref, route_ref):
    hn = _rms(x, g_ref[...])
    _store_slab(hn_ref, _pack_bf16_pairs(hn))
    logits = jnp.dot(hn.astype(BF16), wr_ref[...], preferred_element_type=F32)
    lane = lax.broadcasted_iota(jnp.int32, logits.shape, 1).astype(F32)
    logits = jnp.where(lane < N_EXPERTS, logits, -jnp.inf)
    m1 = jnp.max(logits, axis=-1, keepdims=True)
    i1 = jnp.min(jnp.where(logits == m1, lane, float(LANES)), axis=-1, keepdims=True)
    rest = jnp.where(lane == i1, -jnp.inf, logits)
    m2 = jnp.max(rest, axis=-1, keepdims=True)
    i2 = jnp.min(jnp.where(rest == m2, lane, float(LANES)), axis=-1, keepdims=True)
    e2 = jnp.exp(m2 - m1)
    p1 = 1.0 / (1.0 + e2)
    p2 = e2 * p1
    route_ref[...] = jnp.where(lane == 0, i1, jnp.where(lane == 1, i2,
                     jnp.where(lane == 2, p1, jnp.where(lane == 3, p2, 0.0))))


def _start_row_copy(src_ref, src_row, dst_ref, dst_row, sem, rows, queue):
    pltpu.async_copy(src_ref.at[_slab(src_row, rows)], dst_ref.at[_slab(dst_row, rows)], sem,
                     priority=queue)


def _wait_rows(src_ref, dst_ref, n, sem, rows):
    span = pl.ds(0, n * rows)
    pltpu.make_async_copy(src_ref.at[span], dst_ref.at[span], sem).wait()


def _dispatch_kernel(pos_ref, pad_ref, hn_ref, xs_ref, sem, *, tm):
    i = pl.program_id(0)
    base = i * tm * TOP_K

    def start(r, c):
        for kk in range(TOP_K):
            _start_row_copy(hn_ref, r, xs_ref, pos_ref[base + r * TOP_K + kk], sem, PACKED_ROWS,
                            kk % DMA_QUEUES)
        return c

    lax.fori_loop(0, tm, start, 0, unroll=DMA_ISSUE_UNROLL)
    _wait_rows(xs_ref, xs_ref, tm * TOP_K, sem, PACKED_ROWS)

    @pl.when(i == 0)
    def _():
        npad = pad_ref.shape[0]

        def pstart(r, c):
            for q in range(DMA_QUEUES):
                _start_row_copy(hn_ref, 0, xs_ref, pad_ref[r * DMA_QUEUES + q], sem, PACKED_ROWS, q)
            return c

        lax.fori_loop(0, npad // DMA_QUEUES, pstart, 0, unroll=DMA_ISSUE_UNROLL)
        _wait_rows(xs_ref, xs_ref, npad, sem, PACKED_ROWS)


def _dispatch(hn_slab, pos, pad_rows, n_rows, tm):
    T = hn_slab.shape[0] // PACKED_ROWS
    return pl.pallas_call(
        partial(_dispatch_kernel, tm=tm),
        grid_spec=pltpu.PrefetchScalarGridSpec(
            num_scalar_prefetch=2,
            grid=(T // tm,),
            in_specs=[pl.BlockSpec((tm * PACKED_ROWS, LANES), lambda i, *_: (i, 0))],
            out_specs=pl.BlockSpec(memory_space=pl.ANY),
            scratch_shapes=[pltpu.SemaphoreType.DMA(())]),
        out_shape=jax.ShapeDtypeStruct((n_rows * PACKED_ROWS, LANES), jnp.uint32),
        compiler_params=_cparams("arbitrary"),
        name="dispatch",
    )(pos, pad_rows, hn_slab)


def _expert_ffn_kernel(te_ref, tv_ref, xs_ref, w13_ref, w2_ref, y_ref, *, tm):
    valid = tv_ref[pl.program_id(0)] > 0

    @pl.when(valid)
    def _():
        hn = _unpack_bf16_pairs(_load_slab(xs_ref, 0, tm, PACKED_ROWS))
        _store_slab(y_ref, _swiglu(hn, w13_ref, w2_ref))

    @pl.when(jnp.logical_not(valid))
    def _():
        y_ref[...] = jnp.zeros_like(y_ref)


def _expert_ffn(xs, tile_expert, tile_valid, w13, w2, layer, tm):
    n_rows = xs.shape[0] // PACKED_ROWS
    expert_spec = lambda w: pl.BlockSpec((None, None) + w.shape[2:],
                                         lambda i, te, tv: (layer, te[i], 0, 0),
                                         pipeline_mode=pl.Buffered(1))
    return pl.pallas_call(
        partial(_expert_ffn_kernel, tm=tm),
        grid_spec=pltpu.PrefetchScalarGridSpec(
            num_scalar_prefetch=2,
            grid=(n_rows // tm,),
            in_specs=[pl.BlockSpec((tm * PACKED_ROWS, LANES), lambda i, te, tv: (i, 0)),
                      expert_spec(w13), expert_spec(w2)],
            out_specs=pl.BlockSpec((tm * SLAB_ROWS, LANES), lambda i, te, tv: (i, 0))),
        out_shape=jax.ShapeDtypeStruct((n_rows * SLAB_ROWS, LANES), F32),
        compiler_params=_cparams("parallel"),
        name="expert_ffn",
    )(tile_expert, tile_valid, xs, w13, w2)


def _combine_kernel(pos_ref, x_ref, route_ref, g_ref, y_ref, out_ref, buf, sem, *, tm,
                    final_norm):
    i = pl.program_id(0)

    def issue(tile, slot):
        base = tile * tm * TOP_K

        def start(r, c):
            for kk in range(TOP_K):
                _start_row_copy(y_ref, pos_ref[base + r * TOP_K + kk], buf.at[slot], kk * tm + r,
                                sem.at[slot], SLAB_ROWS, kk % DMA_QUEUES)
            return c

        lax.fori_loop(0, tm, start, 0, unroll=DMA_ISSUE_UNROLL)

    @pl.when(i == 0)
    def _():
        issue(0, 0)

    @pl.when(i + 1 < pl.num_programs(0))
    def _():
        issue(i + 1, (i + 1) % 2)

    slot = i % 2
    _wait_rows(y_ref, buf.at[slot], tm * TOP_K, sem.at[slot], SLAB_ROWS)
    route = route_ref[...]
    acc = x_ref[...]
    for kk in range(TOP_K):
        gate = route[:, TOP_K + kk:TOP_K + kk + 1]
        acc = acc + gate * _load_slab(buf.at[slot], kk * tm, tm, SLAB_ROWS)
    out_ref[...] = _rms(acc, g_ref[...]) if final_norm else acc


def _combine(x, route, y, pos, final_g, final_norm, tm):
    T, D = x.shape
    return pl.pallas_call(
        partial(_combine_kernel, tm=tm, final_norm=final_norm),
        grid_spec=pltpu.PrefetchScalarGridSpec(
            num_scalar_prefetch=1,
            grid=(T // tm,),
            in_specs=[pl.BlockSpec((tm, D), lambda i, *_: (i, 0)),
                      pl.BlockSpec((tm, LANES), lambda i, *_: (i, 0)),
                      pl.BlockSpec((1, D), lambda i, *_: (0, 0)),
                      pl.BlockSpec(memory_space=pl.ANY)],
            out_specs=pl.BlockSpec((tm, D), lambda i, *_: (i, 0)),
            scratch_shapes=[pltpu.VMEM((2, tm * TOP_K * SLAB_ROWS, LANES), F32),
                            pltpu.SemaphoreType.DMA((2,))]),
        out_shape=jax.ShapeDtypeStruct((T, D), F32),
        compiler_params=_cparams("arbitrary"),
        name="combine",
    )(pos, x, route, final_g, y)


def _routing_tables(expert_ids, tm, n_tiles):
    e_flat = expert_ids.reshape(-1)
    onehot = (e_flat[:, None] == jnp.arange(N_EXPERTS, dtype=jnp.int32)[None, :]).astype(jnp.int32)
    csum = jnp.cumsum(onehot, axis=0)
    counts = csum[-1]
    rank = jnp.sum(onehot * (csum - 1), axis=1)
    padded = ((counts + tm - 1) // tm) * tm
    ends = jnp.cumsum(padded)
    starts = ends - padded
    pos = (jnp.sum(onehot * starts[None, :], axis=1) + rank).astype(jnp.int32)

    tile_start = jnp.arange(n_tiles, dtype=jnp.int32) * tm
    tile_valid = (tile_start < ends[-1]).astype(jnp.int32)
    te = jnp.sum((tile_start[:, None] >= ends[None, :]).astype(jnp.int32), axis=1)
    last_e = jnp.sum((ends[-1] - 1 >= ends).astype(jnp.int32))
    tile_expert = jnp.where(tile_valid > 0, te, last_e).astype(jnp.int32)

    npad_e = padded - counts
    pcum = jnp.cumsum(npad_e)
    k = jnp.arange(N_EXPERTS * tm, dtype=jnp.int32)
    ek = jnp.sum((k[:, None] >= pcum[None, :]).astype(jnp.int32), axis=1)
    ohk = (ek[:, None] == jnp.arange(N_EXPERTS, dtype=jnp.int32)[None, :]).astype(jnp.int32)
    in_group = jnp.sum(ohk * (starts + counts - (pcum - npad_e))[None, :], axis=1) + k
    pad = jnp.where(ek < N_EXPERTS, in_group, ends[-1] + k - pcum[-1]).astype(jnp.int32)
    return pos, tile_expert, tile_valid, pad


def _moe_ffn(x, hn_slab, route, w13, w2, layer, final_g, final_norm, tm, tme):
    T, D = x.shape
    n_tiles = (T * TOP_K) // tme + N_EXPERTS
    expert_ids = route[:, :TOP_K].astype(jnp.int32)
    pos, tile_expert, tile_valid, pad = _routing_tables(expert_ids, tme, n_tiles)
    xs = _dispatch(hn_slab, pos, pad, n_tiles * tme, tm)
    y = _expert_ffn(xs, tile_expert, tile_valid, w13, w2, layer, tme)
    return _combine(x, route, y, pos, final_g, final_norm, min(COMBINE_ROW_TILE, tm))


def _final_norm_kernel(x_ref, g_ref, o_ref):
    o_ref[...] = _rms(x_ref[...], g_ref[...])


def _final_norm(x, g, tm):
    T, D = x.shape
    return pl.pallas_call(
        _final_norm_kernel,
        grid=(T // tm,),
        in_specs=[pl.BlockSpec((tm, D), lambda i: (i, 0)), pl.BlockSpec((1, D), lambda i: (0, 0))],
        out_specs=pl.BlockSpec((tm, D), lambda i: (i, 0)),
        out_shape=jax.ShapeDtypeStruct((T, D), F32),
        compiler_params=_cparams("parallel"),
        name="final_norm",
    )(x, g)


def _block_diag(pw):
    G, C, _ = pw.shape
    eye = jnp.eye(G, dtype=pw.dtype)
    return (eye[:, None, :, None] * pw[:, :, None, :]).reshape(G * C, G * C)


def kernel(x, mem, rel_bias, mem_norm_g, mix_norm_g, w_in, conv_w, pool_w, pool_scale, w_out,
           xattn_norm_g, wq_x, wkv_x, wo_x, ffn_norm_g, w13_dense, w2_dense, router_w, w13_moe,
           w2_moe, final_norm_g):
    B, S, D = x.shape
    depth = w_in.shape[0]
    M = mem.shape[1]
    T = B * S
    tm = min(ROW_TILE, S)
    assert D == D_MODEL and S % (max(DILATIONS) * Q_BLOCK) == 0 and S % tm == 0

    xf = x.reshape(T, D)
    row2 = lambda v: v.reshape(1, -1)
    q_scale = jnp.where(jnp.arange(w_in.shape[-1]) < ATTN_WIDTH,
                        ATTN_HEAD_DIM ** -0.5 * LOG2_E, 1.0)
    w_in_b = (w_in * q_scale).astype(BF16)
    biases = [_band_bias(rel_bias * LOG2_E, d) for d in DILATIONS]
    wr_pad = jnp.pad(router_w, ((0, 0), (0, 0), (0, LANES - N_EXPERTS))).astype(BF16)
    w_out_b, wkv_b, wo_b = (w.astype(BF16) for w in (w_out, wkv_x, wo_x))
    wq_b = (wq_x * (XATTN_HEAD_DIM ** -0.5 * LOG2_E)).astype(BF16)
    w13_dense_b, w2_dense_b = w13_dense.astype(BF16), w2_dense.astype(BF16)
    w13_moe_b, w2_moe_b = w13_moe.astype(BF16), w2_moe.astype(BF16)

    mem_f = mem.reshape(B * M, D)
    for layer in range(depth):
        *qkvs, rest = _in_proj(xf, row2(mix_norm_g[layer]), w_in_b, layer, B, S, tm)
        branch = []
        for qkv, bias, d in zip(qkvs, biases, DILATIONS):
            o, lse = _window_attn(qkv.reshape(B * d, S // d, qkv.shape[-1]), bias, ATTN_Q_BLOCKS)
            branch.append((o.reshape(B, d, S // d, -1), lse.reshape(B, d, S // d, -1)))
        kv = _norm_proj(mem_f, row2(mem_norm_g), wkv_b, layer, BF16, min(tm, B * M))
        g = row2(ffn_norm_g[layer])
        routed = layer % 2 == 1
        res = _mixer_out(xf, [o for o, _ in branch], [l for _, l in branch], rest,
                         conv_w[layer], _block_diag(pool_w[layer]).astype(BF16),
                         row2(pool_scale[layer]), w_out_b, row2(xattn_norm_g[layer]), wq_b,
                         kv.reshape(B, M, 2 * XATTN_WIDTH), wo_b, layer, g,
                         wr_pad[min(layer // 2, wr_pad.shape[0] - 1)], routed, S,
                         min(MIXER_ROW_TILE, tm))
        last = layer == depth - 1
        if not routed:
            xf = _dense_ffn(res, g, w13_dense_b, w2_dense_b, layer // 2, min(FFN_ROW_TILE, tm))
            if last:
                xf = _final_norm(xf, row2(final_norm_g), tm)
        else:
            xf, hn_slab, route = res
            xf = _moe_ffn(xf, hn_slab, route, w13_moe_b, w2_moe_b, layer // 2,
                          row2(final_norm_g), last, tm, FFN_ROW_TILE)
    return xf.reshape(B, S, D)
```

```python
from functools import partial

import numpy as np
import jax
import jax.numpy as jnp
from jax import lax
from jax.experimental import pallas as pl
from jax.experimental.pallas import tpu as pltpu

F32 = jnp.float32
BF16 = jnp.bfloat16

D_MODEL = 1024
ATTN_HEADS = 8
ATTN_HEAD_DIM = 64
ATTN_WIDTH = ATTN_HEADS * ATTN_HEAD_DIM
POOL_WIDTH = 256
POOL_GROUP_DIM = 64
POOL_WINDOWS = (2, 4, 8, 16)
POOL_HALO = 16
CONV_WIDTH = 256
REST_WIDTH = POOL_WIDTH + 3 * CONV_WIDTH
DILATIONS = (1, 4, 16)
WINDOW_STEPS = 128
Q_BLOCK = 128
NEG_INF = -1e30
REL_BUCKETS = 32
REL_MAX_DIST = 128
XATTN_HEADS = 4
XATTN_HEAD_DIM = 128
XATTN_WIDTH = XATTN_HEADS * XATTN_HEAD_DIM
N_EXPERTS = 8
TOP_K = 2
EPS = 1e-6

LANES = 128
MXU_COLS = 256
SLAB_ROWS = D_MODEL // LANES
PACKED_ROWS = SLAB_ROWS // 2
ROW_TILE = 1024
FFN_ROW_TILE = 512
COMBINE_ROW_TILE = 512
MIXER_ROW_TILE = 512
LOG2_E = 1.4426950408889634
LN_2 = 0.6931471805599453
ATTN_Q_BLOCKS = 4
DMA_ISSUE_UNROLL = 4
WEIGHT_STAGE_BYTES = 1 << 20
DMA_QUEUES = 2
VMEM_LIMIT = 56 * 1024 * 1024


def _cparams(*sem):
    return pltpu.CompilerParams(dimension_semantics=sem, vmem_limit_bytes=VMEM_LIMIT)


def _rms(x, g):
    return x * lax.rsqrt(jnp.mean(x * x, axis=-1, keepdims=True) + EPS) * g


def _norm_proj_kernel(x_ref, g_ref, w_ref, o_ref):
    xn = _rms(x_ref[...], g_ref[...]).astype(BF16)
    o_ref[...] = jnp.dot(xn, w_ref[...], preferred_element_type=F32).astype(o_ref.dtype)


def _layer_spec(w, layer):
    return pl.BlockSpec((None,) + w.shape[1:], lambda *_: (layer, 0, 0),
                        pipeline_mode=pl.Buffered(1))


def _norm_proj(x, g, w, layer, out_dtype, tm):
    T, D = x.shape
    N = w.shape[-1]
    return pl.pallas_call(
        _norm_proj_kernel,
        grid=(T // tm,),
        in_specs=[pl.BlockSpec((tm, D), lambda i: (i, 0)),
                  pl.BlockSpec((1, D), lambda i: (0, 0)),
                  _layer_spec(w, layer)],
        out_specs=pl.BlockSpec((tm, N), lambda i: (i, 0)),
        out_shape=jax.ShapeDtypeStruct((T, N), out_dtype),
        compiler_params=_cparams("parallel"),
        name="norm_proj",
    )(x, g, w)


def _in_proj_kernel(x_ref, g_ref, w_ref, *refs):
    nd = len(DILATIONS)
    qkv_refs, rest_ref, acc_refs = refs[:nd], refs[nd], refs[nd + 1:]
    per, tm, _ = acc_refs[0].shape
    nq = len(acc_refs) * MXU_COLS
    xn = _rms(x_ref[...], g_ref[...]).astype(BF16)
    for nb, acc_ref in enumerate(acc_refs):
        cols = slice(nb * MXU_COLS, (nb + 1) * MXU_COLS)
        acc = jnp.dot(xn, w_ref[:, cols], preferred_element_type=F32)
        for c in range(per):
            acc_ref[c] = acc[:, c * LANES:(c + 1) * LANES]
        for d, qkv_ref in zip(DILATIONS, qkv_refs):
            for r in range(d):
                rows = [acc_ref[c, pl.ds(r, tm // d, stride=d), :] for c in range(per)]
                qkv_ref[0, r, :, cols] = jnp.concatenate(rows, axis=1).astype(BF16)
    rest_ref[...] = jnp.dot(xn, w_ref[:, nq:], preferred_element_type=F32)


def _in_proj(x, g, w, layer, B, S, tm):
    T, D = x.shape
    N = w.shape[-1]
    nq = 3 * ATTN_WIDTH
    tps = S // tm
    res_spec = lambda d: pl.BlockSpec((1, d, tm // d, nq), lambda i: (i // tps, 0, i % tps, 0))
    return pl.pallas_call(
        _in_proj_kernel,
        grid=(T // tm,),
        in_specs=[pl.BlockSpec((tm, D), lambda i: (i, 0)),
                  pl.BlockSpec((1, D), lambda i: (0, 0)),
                  _layer_spec(w, layer)],
        out_specs=[res_spec(d) for d in DILATIONS]
                  + [pl.BlockSpec((tm, N - nq), lambda i: (i, 0))],
        out_shape=[jax.ShapeDtypeStruct((B, d, S // d, nq), BF16) for d in DILATIONS]
                  + [jax.ShapeDtypeStruct((T, N - nq), F32)],
        scratch_shapes=[pltpu.VMEM((MXU_COLS // LANES, tm, LANES), F32)] * (nq // MXU_COLS),
        compiler_params=_cparams("parallel"),
        name="in_proj",
    )(x, g, w)


def _t5_causal_bucket(dist):
    max_exact = REL_BUCKETS // 2
    d = np.maximum(dist, 1).astype(np.float32)
    large = max_exact + (np.log(d / max_exact) / np.log(REL_MAX_DIST / max_exact)
                         * (REL_BUCKETS - max_exact)).astype(np.int32)
    large = np.minimum(large, REL_BUCKETS - 1)
    return np.where(dist < max_exact, dist, large).astype(np.int32)


def _band_bias(rel_bias, dil):
    row = np.arange(Q_BLOCK)[:, None]
    col = np.arange(2 * Q_BLOCK)[None, :]
    steps = Q_BLOCK + row - col
    valid = (steps >= 0) & (steps <= WINDOW_STEPS)
    bucket = _t5_causal_bucket(dil * np.clip(steps, 0, WINDOW_STEPS))
    onehot = (bucket[..., None] == np.arange(REL_BUCKETS)).astype(np.float32)
    b = jnp.einsum('qcb,bh->hqc', jnp.asarray(onehot), rel_bias.astype(F32),
                   precision=lax.Precision.HIGHEST)
    masks = np.stack([valid, valid & (col >= Q_BLOCK)])
    return jnp.where(jnp.asarray(masks)[:, None], b[None], NEG_INF)


def _window_attn_kernel(q_ref, kp_ref, k_ref, vp_ref, v_ref, bias_ref, o_ref, lse_ref, *, nq):
    first_step = jnp.where(pl.program_id(1) == 0, 1, 0)
    lane = lax.broadcasted_iota(jnp.int32, (Q_BLOCK, LANES), 1)
    low_half = lane < ATTN_HEAD_DIM
    Q = Q_BLOCK
    for j in range(nq):
        rows = slice(j * Q, (j + 1) * Q)
        if j == 0:
            k2 = jnp.concatenate([kp_ref[0], k_ref[0, :Q, :]], axis=0)
            v2 = jnp.concatenate([vp_ref[0], v_ref[0, :Q, :]], axis=0)
            table = first_step
        else:
            k2 = k_ref[0, (j - 1) * Q:(j + 1) * Q, :]
            v2 = v_ref[0, (j - 1) * Q:(j + 1) * Q, :]
            table = 0
        lse = jnp.zeros((Q, LANES), F32)
        for g in range(ATTN_WIDTH // LANES):
            cols = slice(g * LANES, (g + 1) * LANES)
            qg, kg, vg = q_ref[0, rows, cols], k2[:, cols], v2[:, cols]
            og = None
            for half in range(2):
                h = 2 * g + half
                keep = low_half if half == 0 else jnp.logical_not(low_half)
                qh = jnp.where(keep, qg, jnp.zeros_like(qg))
                s = lax.dot_general(qh, kg, (((1,), (1,)), ((), ())),
                                    preferred_element_type=F32)
                s = s + bias_ref[table, h]
                m = jnp.max(s, axis=-1, keepdims=True)
                p = jnp.exp2(s - m)
                l = jnp.sum(p, axis=-1, keepdims=True)
                o = jnp.dot(p.astype(BF16), vg, preferred_element_type=F32) * (1.0 / l)
                og = o if half == 0 else jnp.where(low_half, og, o)
                lse = jnp.where(lane == h, (m + jnp.log2(l)) * LN_2, lse)
            o_ref[0, rows, cols] = og.astype(o_ref.dtype)
        lse_ref[0, rows, :] = lse


def _window_attn(qkv, bias, nq):
    N, L, _ = qkv.shape
    W = ATTN_WIDTH
    nq = min(nq, L // Q_BLOCK)
    step = nq * Q_BLOCK
    prev = lambda i: jnp.maximum(i * nq - 1, 0)
    return pl.pallas_call(
        partial(_window_attn_kernel, nq=nq),
        grid=(N, L // step),
        in_specs=[pl.BlockSpec((1, step, W), lambda n, i: (n, i, 0)),
                  pl.BlockSpec((1, Q_BLOCK, W), lambda n, i: (n, prev(i), 1)),
                  pl.BlockSpec((1, step, W), lambda n, i: (n, i, 1)),
                  pl.BlockSpec((1, Q_BLOCK, W), lambda n, i: (n, prev(i), 2)),
                  pl.BlockSpec((1, step, W), lambda n, i: (n, i, 2)),
                  pl.BlockSpec(bias.shape, lambda n, i: (0, 0, 0, 0))],
        out_specs=[pl.BlockSpec((1, step, W), lambda n, i: (n, i, 0)),
                   pl.BlockSpec((1, step, LANES), lambda n, i: (n, i, 0))],
        out_shape=[jax.ShapeDtypeStruct((N, L, W), BF16),
                   jax.ShapeDtypeStruct((N, L, LANES), F32)],
        compiler_params=_cparams("parallel", "parallel"),
        name="window_attn",
    )(qkv, qkv, qkv, qkv, qkv, bias)


def _token_order(src_ref, scratch_ref, d):
    if d == 1:
        return src_ref[0, 0].astype(F32)
    nchunk, n, _ = scratch_ref.shape
    for r in range(d):
        part = src_ref[0, r].astype(F32)
        for c in range(nchunk):
            scratch_ref[c, pl.ds(r, n // d, stride=d), :] = part[:, c * LANES:(c + 1) * LANES]
    return jnp.concatenate([scratch_ref[c] for c in range(nchunk)], axis=1)


def _memory_xattn(x, g_ref, wq_ref, kv_ref, wo_ref):
    xn = _rms(x, g_ref[...]).astype(BF16)
    q = jnp.dot(xn, wq_ref[...], preferred_element_type=F32).astype(BF16)
    heads = []
    for h in range(XATTN_HEADS):
        sl = slice(h * XATTN_HEAD_DIM, (h + 1) * XATTN_HEAD_DIM)
        k = kv_ref[0, :, sl]
        v = kv_ref[0, :, XATTN_WIDTH + h * XATTN_HEAD_DIM:XATTN_WIDTH + (h + 1) * XATTN_HEAD_DIM]
        s = lax.dot_general(q[:, sl], k, (((1,), (1,)), ((), ())), preferred_element_type=F32)
        m = jnp.max(s, axis=-1, keepdims=True)
        p = jnp.exp2(s - m)
        l = jnp.sum(p, axis=-1, keepdims=True)
        heads.append(jnp.dot(p.astype(BF16), v, preferred_element_type=F32) * (1.0 / l))
    o = jnp.concatenate(heads, axis=1).astype(BF16)
    return x + jnp.dot(o, wo_ref[...], preferred_element_type=F32)


def _mixer_out_kernel(x_ref, o1_ref, o2_ref, o3_ref, l1_ref, l2_ref, l3_ref, rest_ref, halo_ref,
                      convw_ref, poolw_ref, pscale_ref, pwin_ref, expand_ref, wout_ref,
                      xg_ref, wq_ref, kv_ref, wo_ref, fg_ref, wr_ref, out_ref, *refs,
                      tm, tiles_per_seq, routed):
    os2_ref, os3_ref, ls2_ref, ls3_ref = refs[-4:]
    it = pl.program_id(0) % tiles_per_seq
    lses = [_token_order(r, s, d)
            for r, s, d in zip((l1_ref, l2_ref, l3_ref), (None, ls2_ref, ls3_ref), DILATIONS)]
    outs = [_token_order(r, s, d)
            for r, s, d in zip((o1_ref, o2_ref, o3_ref), (None, os2_ref, os3_ref), DILATIONS)]
    lm = jnp.maximum(jnp.maximum(lses[0], lses[1]), lses[2])
    es = [jnp.exp(l - lm) for l in lses]
    inv = 1.0 / (es[0] + es[1] + es[2])
    y_attn = jnp.zeros((tm, ATTN_WIDTH), F32)
    for e, o in zip(es, outs):
        w = e * inv
        w_hi = w.astype(BF16)
        w_lo = (w - w_hi.astype(F32)).astype(BF16)
        wide = jnp.dot(jnp.concatenate([w_hi, w_lo], axis=1), expand_ref[...],
                       preferred_element_type=F32)
        y_attn = y_attn + wide * o
    y_attn = y_attn.astype(BF16)

    rest = rest_ref[...]
    halo = jnp.where(it == 0, 0.0, halo_ref[...])
    ext = jnp.concatenate([halo, rest], axis=0)

    u = ext[:, :POOL_WIDTH]
    s2 = u + pltpu.roll(u, 1, 0)
    s4 = s2 + pltpu.roll(s2, 2, 0)
    s8 = s4 + pltpu.roll(s4, 4, 0)
    s16 = s8 + pltpu.roll(s8, 8, 0)
    col = lax.broadcasted_iota(jnp.int32, u.shape, 1)
    g = POOL_GROUP_DIM
    wsum = jnp.where(col < g, s2, jnp.where(col < 2 * g, s4, jnp.where(col < 3 * g, s8, s16)))
    pos = it * tm + lax.broadcasted_iota(jnp.int32, (tm, POOL_WIDTH), 0)
    count = jnp.minimum((pos + 1).astype(F32), pwin_ref[...])
    pooled = wsum[POOL_HALO:] / count - u[POOL_HALO:]
    y_pool = jnp.dot(pooled.astype(BF16), poolw_ref[...], preferred_element_type=F32)
    y_pool = (y_pool * pscale_ref[...]).astype(BF16)

    b_gate = rest[:, POOL_WIDTH:POOL_WIDTH + CONV_WIDTH]
    cu = ext[:, POOL_WIDTH + CONV_WIDTH:POOL_WIDTH + 2 * CONV_WIDTH] * ext[:, POOL_WIDTH + 2 * CONV_WIDTH:]
    y = (convw_ref[0:1, :] * cu + convw_ref[1:2, :] * pltpu.roll(cu, 1, 0)
         + convw_ref[2:3, :] * pltpu.roll(cu, 2, 0))
    y_conv = (b_gate * y[POOL_HALO:]).astype(BF16)

    a0, a1 = ATTN_WIDTH, ATTN_WIDTH + POOL_WIDTH
    acc = jnp.dot(y_attn, wout_ref[:a0, :], preferred_element_type=F32)
    acc += jnp.dot(y_pool, wout_ref[a0:a1, :], preferred_element_type=F32)
    acc += jnp.dot(y_conv, wout_ref[a1:, :], preferred_element_type=F32)
    x_new = _memory_xattn(x_ref[...] + acc, xg_ref, wq_ref, kv_ref, wo_ref)
    out_ref[...] = x_new
    if routed:
        _route_tokens(x_new, fg_ref, wr_ref, *refs[:2])


def _mixer_out(x, outs, lses, rest, conv_w, pool_w_bd, pool_scale, w_out, xg, wq, kv, wo, layer,
               ffn_g, wr, routed, S, tm):
    T, D = x.shape
    M = kv.shape[1]
    hb = tm // POOL_HALO
    tps = S // tm
    row = lambda i: (i, 0)
    const = lambda i: (0, 0)
    res_spec = lambda d, c: pl.BlockSpec((1, d, tm // d, c), lambda i: (i // tps, 0, i % tps, 0))
    pwin = jnp.asarray(np.repeat(np.array(POOL_WINDOWS, np.float32), POOL_GROUP_DIM)[None, :])
    expand = np.zeros((2, LANES, ATTN_WIDTH), np.float32)
    for h in range(ATTN_HEADS):
        expand[:, h, h * ATTN_HEAD_DIM:(h + 1) * ATTN_HEAD_DIM] = 1.0
    expand = jnp.asarray(expand.reshape(2 * LANES, ATTN_WIDTH), dtype=BF16)
    out_specs = [pl.BlockSpec((tm, D), row)]
    out_shape = [jax.ShapeDtypeStruct((T, D), F32)]
    if routed:
        out_specs += [pl.BlockSpec((tm * PACKED_ROWS, LANES), row), pl.BlockSpec((tm, LANES), row)]
        out_shape += [jax.ShapeDtypeStruct((T * PACKED_ROWS, LANES), jnp.uint32),
                      jax.ShapeDtypeStruct((T, LANES), F32)]
    res = pl.pallas_call(
        partial(_mixer_out_kernel, tm=tm, tiles_per_seq=tps, routed=routed),
        grid=(T // tm,),
        scratch_shapes=[pltpu.VMEM((ATTN_WIDTH // LANES, tm, LANES), F32)] * 2
                       + [pltpu.VMEM((1, tm, LANES), F32)] * 2,
        name="mixer_out",
        in_specs=[pl.BlockSpec((tm, D), row)]
                 + [res_spec(d, ATTN_WIDTH) for d in DILATIONS]
                 + [res_spec(d, LANES) for d in DILATIONS]
                 + [pl.BlockSpec((tm, REST_WIDTH), row),
                    pl.BlockSpec((POOL_HALO, REST_WIDTH), lambda i: (jnp.maximum(i * hb - 1, 0), 0)),
                    pl.BlockSpec(conv_w.shape, const),
                    pl.BlockSpec(pool_w_bd.shape, const),
                    pl.BlockSpec((1, POOL_WIDTH), const),
                    pl.BlockSpec((1, POOL_WIDTH), const),
                    pl.BlockSpec(expand.shape, const),
                    _layer_spec(w_out, layer),
                    pl.BlockSpec((1, D), const),
                    _layer_spec(wq, layer),
                    pl.BlockSpec((1, M, 2 * XATTN_WIDTH), lambda i: (i // tps, 0, 0)),
                    _layer_spec(wo, layer),
                    pl.BlockSpec((1, D), const),
                    pl.BlockSpec(wr.shape, const)],
        out_specs=out_specs,
        out_shape=out_shape,
        compiler_params=_cparams("parallel"),
    )(x, *outs, *lses, rest, rest, conv_w, pool_w_bd, pool_scale, pwin, expand, w_out,
      xg, wq, kv, wo, ffn_g, wr)
    return res if routed else res[0]


def _swiglu(hn, w13_ref, w2_ref):
    dff = w2_ref.shape[0]
    a = jnp.dot(hn, w13_ref[:, :dff], preferred_element_type=F32)
    b = jnp.dot(hn, w13_ref[:, dff:], preferred_element_type=F32)
    h = (a * jax.nn.sigmoid(a) * b).astype(BF16)
    return jnp.dot(h, w2_ref[...], preferred_element_type=F32)


def _dense_ffn_kernel(x_ref, g_ref, w13_ref, w2_ref, out_ref):
    x = x_ref[...]
    out_ref[...] = x + _swiglu(_rms(x, g_ref[...]).astype(BF16), w13_ref, w2_ref)


def _dense_ffn(x, g, w13, w2, layer, tm):
    T, D = x.shape
    return pl.pallas_call(
        _dense_ffn_kernel,
        grid=(T // tm,),
        in_specs=[pl.BlockSpec((tm, D), lambda i: (i, 0)),
                  pl.BlockSpec((1, D), lambda i: (0, 0)),
                  _layer_spec(w13, layer),
                  _layer_spec(w2, layer)],
        out_specs=pl.BlockSpec((tm, D), lambda i: (i, 0)),
        out_shape=jax.ShapeDtypeStruct((T, D), F32),
        compiler_params=_cparams("parallel"),
        name="dense_ffn",
    )(x, g, w13, w2)


def _store_slab(ref, v):
    n, rows = v.shape[0], v.shape[1] // LANES
    for s in range(rows):
        ref[pl.ds(s, n, stride=rows), :] = v[:, s * LANES:(s + 1) * LANES]


def _load_slab(ref, first, n, rows):
    return jnp.concatenate(
        [ref[pl.ds(first * rows + s, n, stride=rows), :] for s in range(rows)], axis=1)


def _slab(row, rows):
    return pl.ds(pl.multiple_of(row * rows, rows), rows)


def _pack_bf16_pairs(v):
    half = v.shape[1] // 2
    bits = pltpu.bitcast(v.astype(BF16).astype(F32), jnp.uint32)
    return (bits[:, half:] & jnp.uint32(0xFFFF0000)) | (bits[:, :half] >> 16)


def _unpack_bf16_pairs(u):
    low = pltpu.bitcast(u << 16, F32)
    high = pltpu.bitcast(u & jnp.uint32(0xFFFF0000), F32)
    return jnp.concatenate([low, high], axis=1).astype(BF16)


def _route_tokens(x, g_ref, wr_ref, hn_ref, route_ref):
    hn = _rms(x, g_ref[...])
    _store_slab(hn_ref, _pack_bf16_pairs(hn))
    logits = jnp.dot(hn.astype(BF16), wr_ref[...], preferred_element_type=F32)
    lane = lax.broadcasted_iota(jnp.int32, logits.shape, 1).astype(F32)
    logits = jnp.where(lane < N_EXPERTS, logits, -jnp.inf)
    m1 = jnp.max(logits, axis=-1, keepdims=True)
    i1 = jnp.min(jnp.where(logits == m1, lane, float(LANES)), axis=-1, keepdims=True)
    rest = jnp.where(lane == i1, -jnp.inf, logits)
    m2 = jnp.max(rest, axis=-1, keepdims=True)
    i2 = jnp.min(jnp.where(rest == m2, lane, float(LANES)), axis=-1, keepdims=True)
    e2 = jnp.exp(m2 - m1)
    p1 = 1.0 / (1.0 + e2)
    p2 = e2 * p1
    route_ref[...] = jnp.where(lane == 0, i1, jnp.where(lane == 1, i2,
                     jnp.where(lane == 2, p1, jnp.where(lane == 3, p2, 0.0))))


def _start_row_copy(src_ref, src_row, dst_ref, dst_row, sem, rows, queue):
    pltpu.async_copy(src_ref.at[_slab(src_row, rows)], dst_ref.at[_slab(dst_row, rows)], sem,
                     priority=queue)


def _wait_rows(src_ref, dst_ref, n, sem, rows):
    span = pl.ds(0, n * rows)
    pltpu.make_async_copy(src_ref.at[span], dst_ref.at[span], sem).wait()


def _dispatch_kernel(pos_ref, pad_ref, hn_ref, xs_ref, sem, *, tm):
    i = pl.program_id(0)
    base = i * tm * TOP_K

    def start(r, c):
        for kk in range(TOP_K):
            _start_row_copy(hn_ref, r, xs_ref, pos_ref[base + r * TOP_K + kk], sem, PACKED_ROWS,
                            kk % DMA_QUEUES)
        return c

    lax.fori_loop(0, tm, start, 0, unroll=DMA_ISSUE_UNROLL)
    _wait_rows(xs_ref, xs_ref, tm * TOP_K, sem, PACKED_ROWS)

    @pl.when(i == 0)
    def _():
        npad = pad_ref.shape[0]

        def pstart(r, c):
            for q in range(DMA_QUEUES):
                _start_row_copy(hn_ref, 0, xs_ref, pad_ref[r * DMA_QUEUES + q], sem, PACKED_ROWS, q)
            return c

        lax.fori_loop(0, npad // DMA_QUEUES, pstart, 0, unroll=DMA_ISSUE_UNROLL)
        _wait_rows(xs_ref, xs_ref, npad, sem, PACKED_ROWS)


def _dispatch(hn_slab, pos, pad_rows, n_rows, tm):
    T = hn_slab.shape[0] // PACKED_ROWS
    return pl.pallas_call(
        partial(_dispatch_kernel, tm=tm),
        grid_spec=pltpu.PrefetchScalarGridSpec(
            num_scalar_prefetch=2,
            grid=(T // tm,),
            in_specs=[pl.BlockSpec((tm * PACKED_ROWS, LANES), lambda i, *_: (i, 0))],
            out_specs=pl.BlockSpec(memory_space=pl.ANY),
            scratch_shapes=[pltpu.SemaphoreType.DMA(())]),
        out_shape=jax.ShapeDtypeStruct((n_rows * PACKED_ROWS, LANES), jnp.uint32),
        compiler_params=_cparams("arbitrary"),
        name="dispatch",
    )(pos, pad_rows, hn_slab)


def _load_weight_as_bf16(src_hbm, layer, expert, dst_ref, stage_ref, sem, sem_base):
    rows = stage_ref.shape[1]
    n = dst_ref.shape[0] // rows

    def chunk(c, slot):
        return pltpu.make_async_copy(src_hbm.at[layer, expert, pl.ds(c * rows, rows), :],
                                     stage_ref.at[slot], sem.at[sem_base + slot])

    chunk(0, 0).start()

    def body(c, carry):
        slot = c % 2

        @pl.when(c + 1 < n)
        def _():
            chunk(c + 1, 1 - slot).start()

        chunk(c, slot).wait()
        dst_ref[pl.ds(pl.multiple_of(c * rows, rows), rows), :] = stage_ref[slot].astype(BF16)
        return carry

    lax.fori_loop(0, n, body, 0)


def _expert_ffn_kernel(te_ref, tv_ref, xs_ref, w13_hbm, w2_hbm, y_ref,
                       w13_ref, w2_ref, stage13_ref, stage2_ref, sem, *, tm, layer):
    i = pl.program_id(0)
    valid = tv_ref[i] > 0
    expert = te_ref[i]
    changed = (i == 0) | (expert != te_ref[jnp.maximum(i - 1, 0)])

    @pl.when(changed)
    def _():
        _load_weight_as_bf16(w13_hbm, layer, expert, w13_ref, stage13_ref, sem, 0)
        _load_weight_as_bf16(w2_hbm, layer, expert, w2_ref, stage2_ref, sem, 2)

    @pl.when(valid)
    def _():
        hn = _unpack_bf16_pairs(_load_slab(xs_ref, 0, tm, PACKED_ROWS))
        _store_slab(y_ref, _swiglu(hn, w13_ref, w2_ref))

    @pl.when(jnp.logical_not(valid))
    def _():
        y_ref[...] = jnp.zeros_like(y_ref)


def _expert_ffn(xs, tile_expert, tile_valid, w13, w2, layer, tm):
    n_rows = xs.shape[0] // PACKED_ROWS
    D, dff2 = w13.shape[2:]
    dff = w2.shape[2]
    stage_rows = lambda cols: 1 << ((WEIGHT_STAGE_BYTES // (4 * cols)).bit_length() - 1)
    r13, r2 = stage_rows(dff2), stage_rows(D)
    assert D % r13 == 0 and dff % r2 == 0
    return pl.pallas_call(
        partial(_expert_ffn_kernel, tm=tm, layer=layer),
        grid_spec=pltpu.PrefetchScalarGridSpec(
            num_scalar_prefetch=2,
            grid=(n_rows // tm,),
            in_specs=[pl.BlockSpec((tm * PACKED_ROWS, LANES), lambda i, te, tv: (i, 0)),
                      pl.BlockSpec(memory_space=pl.ANY), pl.BlockSpec(memory_space=pl.ANY)],
            out_specs=pl.BlockSpec((tm * SLAB_ROWS, LANES), lambda i, te, tv: (i, 0)),
            scratch_shapes=[pltpu.VMEM((D, dff2), BF16), pltpu.VMEM((dff, D), BF16),
                            pltpu.VMEM((2, r13, dff2), F32), pltpu.VMEM((2, r2, D), F32),
                            pltpu.SemaphoreType.DMA((4,))]),
        out_shape=jax.ShapeDtypeStruct((n_rows * SLAB_ROWS, LANES), F32),
        compiler_params=_cparams("arbitrary"),
        name="expert_ffn",
    )(tile_expert, tile_valid, xs, w13, w2)


def _combine_kernel(pos_ref, x_ref, route_ref, g_ref, y_ref, out_ref, buf, sem, *, tm,
                    final_norm):
    i = pl.program_id(0)

    def issue(tile, slot):
        base = tile * tm * TOP_K

        def start(r, c):
            for kk in range(TOP_K):
                _start_row_copy(y_ref, pos_ref[base + r * TOP_K + kk], buf.at[slot], kk * tm + r,
                                sem.at[slot], SLAB_ROWS, kk % DMA_QUEUES)
            return c

        lax.fori_loop(0, tm, start, 0, unroll=DMA_ISSUE_UNROLL)

    @pl.when(i == 0)
    def _():
        issue(0, 0)

    @pl.when(i + 1 < pl.num_programs(0))
    def _():
        issue(i + 1, (i + 1) % 2)

    slot = i % 2
    _wait_rows(y_ref, buf.at[slot], tm * TOP_K, sem.at[slot], SLAB_ROWS)
    route = route_ref[...]
    acc = x_ref[...]
    for kk in range(TOP_K):
        gate = route[:, TOP_K + kk:TOP_K + kk + 1]
        acc = acc + gate * _load_slab(buf.at[slot], kk * tm, tm, SLAB_ROWS)
    out_ref[...] = _rms(acc, g_ref[...]) if final_norm else acc


def _combine(x, route, y, pos, final_g, final_norm, tm):
    T, D = x.shape
    return pl.pallas_call(
        partial(_combine_kernel, tm=tm, final_norm=final_norm),
        grid_spec=pltpu.PrefetchScalarGridSpec(
            num_scalar_prefetch=1,
            grid=(T // tm,),
            in_specs=[pl.BlockSpec((tm, D), lambda i, *_: (i, 0)),
                      pl.BlockSpec((tm, LANES), lambda i, *_: (i, 0)),
                      pl.BlockSpec((1, D), lambda i, *_: (0, 0)),
                      pl.BlockSpec(memory_space=pl.ANY)],
            out_specs=pl.BlockSpec((tm, D), lambda i, *_: (i, 0)),
            scratch_shapes=[pltpu.VMEM((2, tm * TOP_K * SLAB_ROWS, LANES), F32),
                            pltpu.SemaphoreType.DMA((2,))]),
        out_shape=jax.ShapeDtypeStruct((T, D), F32),
        compiler_params=_cparams("arbitrary"),
        name="combine",
    )(pos, x, route, final_g, y)


def _routing_tables(expert_ids, tm, n_tiles):
    e_flat = expert_ids.reshape(-1)
    onehot = (e_flat[:, None] == jnp.arange(N_EXPERTS, dtype=jnp.int32)[None, :]).astype(jnp.int32)
    csum = jnp.cumsum(onehot, axis=0)
    counts = csum[-1]
    rank = jnp.sum(onehot * (csum - 1), axis=1)
    padded = ((counts + tm - 1) // tm) * tm
    ends = jnp.cumsum(padded)
    starts = ends - padded
    pos = (jnp.sum(onehot * starts[None, :], axis=1) + rank).astype(jnp.int32)

    tile_start = jnp.arange(n_tiles, dtype=jnp.int32) * tm
    tile_valid = (tile_start < ends[-1]).astype(jnp.int32)
    te = jnp.sum((tile_start[:, None] >= ends[None, :]).astype(jnp.int32), axis=1)
    last_e = jnp.sum((ends[-1] - 1 >= ends).astype(jnp.int32))
    tile_expert = jnp.where(tile_valid > 0, te, last_e).astype(jnp.int32)

    npad_e = padded - counts
    pcum = jnp.cumsum(npad_e)
    k = jnp.arange(N_EXPERTS * tm, dtype=jnp.int32)
    ek = jnp.sum((k[:, None] >= pcum[None, :]).astype(jnp.int32), axis=1)
    ohk = (ek[:, None] == jnp.arange(N_EXPERTS, dtype=jnp.int32)[None, :]).astype(jnp.int32)
    in_group = jnp.sum(ohk * (starts + counts - (pcum - npad_e))[None, :], axis=1) + k
    pad = jnp.where(ek < N_EXPERTS, in_group, ends[-1] + k - pcum[-1]).astype(jnp.int32)
    return pos, tile_expert, tile_valid, pad


def _moe_ffn(x, hn_slab, route, w13, w2, layer, final_g, final_norm, tm, tme):
    T, D = x.shape
    n_tiles = (T * TOP_K) // tme + N_EXPERTS
    expert_ids = route[:, :TOP_K].astype(jnp.int32)
    pos, tile_expert, tile_valid, pad = _routing_tables(expert_ids, tme, n_tiles)
    xs = _dispatch(hn_slab, pos, pad, n_tiles * tme, tm)
    y = _expert_ffn(xs, tile_expert, tile_valid, w13, w2, layer, tme)
    return _combine(x, route, y, pos, final_g, final_norm, min(COMBINE_ROW_TILE, tm))


def _final_norm_kernel(x_ref, g_ref, o_ref):
    o_ref[...] = _rms(x_ref[...], g_ref[...])


def _final_norm(x, g, tm):
    T, D = x.shape
    return pl.pallas_call(
        _final_norm_kernel,
        grid=(T // tm,),
        in_specs=[pl.BlockSpec((tm, D), lambda i: (i, 0)), pl.BlockSpec((1, D), lambda i: (0, 0))],
        out_specs=pl.BlockSpec((tm, D), lambda i: (i, 0)),
        out_shape=jax.ShapeDtypeStruct((T, D), F32),
        compiler_params=_cparams("parallel"),
        name="final_norm",
    )(x, g)


def _block_diag(pw):
    G, C, _ = pw.shape
    eye = jnp.eye(G, dtype=pw.dtype)
    return (eye[:, None, :, None] * pw[:, :, None, :]).reshape(G * C, G * C)


def kernel(x, mem, rel_bias, mem_norm_g, mix_norm_g, w_in, conv_w, pool_w, pool_scale, w_out,
           xattn_norm_g, wq_x, wkv_x, wo_x, ffn_norm_g, w13_dense, w2_dense, router_w, w13_moe,
           w2_moe, final_norm_g):
    B, S, D = x.shape
    depth = w_in.shape[0]
    M = mem.shape[1]
    T = B * S
    tm = min(ROW_TILE, S)
    assert D == D_MODEL and S % (max(DILATIONS) * Q_BLOCK) == 0 and S % tm == 0

    xf = x.reshape(T, D)
    row2 = lambda v: v.reshape(1, -1)
    q_scale = jnp.where(jnp.arange(w_in.shape[-1]) < ATTN_WIDTH,
                        ATTN_HEAD_DIM ** -0.5 * LOG2_E, 1.0)
    w_in_b = (w_in * q_scale).astype(BF16)
    biases = [_band_bias(rel_bias * LOG2_E, d) for d in DILATIONS]
    wr_pad = jnp.pad(router_w, ((0, 0), (0, 0), (0, LANES - N_EXPERTS))).astype(BF16)
    w_out_b, wkv_b, wo_b = (w.astype(BF16) for w in (w_out, wkv_x, wo_x))
    wq_b = (wq_x * (XATTN_HEAD_DIM ** -0.5 * LOG2_E)).astype(BF16)
    w13_dense_b, w2_dense_b = w13_dense.astype(BF16), w2_dense.astype(BF16)

    mem_f = mem.reshape(B * M, D)
    for layer in range(depth):
        *qkvs, rest = _in_proj(xf, row2(mix_norm_g[layer]), w_in_b, layer, B, S, tm)
        branch = []
        for qkv, bias, d in zip(qkvs, biases, DILATIONS):
            o, lse = _window_attn(qkv.reshape(B * d, S // d, qkv.shape[-1]), bias, ATTN_Q_BLOCKS)
            branch.append((o.reshape(B, d, S // d, -1), lse.reshape(B, d, S // d, -1)))
        kv = _norm_proj(mem_f, row2(mem_norm_g), wkv_b, layer, BF16, min(tm, B * M))
        g = row2(ffn_norm_g[layer])
        routed = layer % 2 == 1
        res = _mixer_out(xf, [o for o, _ in branch], [l for _, l in branch], rest,
                         conv_w[layer], _block_diag(pool_w[layer]).astype(BF16),
                         row2(pool_scale[layer]), w_out_b, row2(xattn_norm_g[layer]), wq_b,
                         kv.reshape(B, M, 2 * XATTN_WIDTH), wo_b, layer, g,
                         wr_pad[min(layer // 2, wr_pad.shape[0] - 1)], routed, S,
                         min(MIXER_ROW_TILE, tm))
        last = layer == depth - 1
        if not routed:
            xf = _dense_ffn(res, g, w13_dense_b, w2_dense_b, layer // 2, min(FFN_ROW_TILE, tm))
            if last:
                xf = _final_norm(xf, row2(final_norm_g), tm)
        else:
            xf, hn_slab, route = res
            xf = _moe_ffn(xf, hn_slab, route, w13_moe, w2_moe, layer // 2,
                          row2(final_norm_g), last, tm, FFN_ROW_TILE)
    return xf.reshape(B, S, D)
```

```python
from functools import partial

import numpy as np
import jax
import jax.numpy as jnp
from jax import lax
from jax.experimental import pallas as pl
from jax.experimental.pallas import tpu as pltpu

F32 = jnp.float32
BF16 = jnp.bfloat16

D_MODEL = 1024
ATTN_HEADS = 8
ATTN_HEAD_DIM = 64
ATTN_WIDTH = ATTN_HEADS * ATTN_HEAD_DIM
POOL_WIDTH = 256
POOL_GROUP_DIM = 64
POOL_WINDOWS = (2, 4, 8, 16)
POOL_HALO = 16
CONV_WIDTH = 256
REST_WIDTH = POOL_WIDTH + 3 * CONV_WIDTH
DILATIONS = (1, 4, 16)
WINDOW_STEPS = 128
Q_BLOCK = 128
NEG_INF = -1e30
REL_BUCKETS = 32
REL_MAX_DIST = 128
XATTN_HEADS = 4
XATTN_HEAD_DIM = 128
XATTN_WIDTH = XATTN_HEADS * XATTN_HEAD_DIM
N_EXPERTS = 8
TOP_K = 2
EPS = 1e-6

LANES = 128
MXU_COLS = 256
SLAB_ROWS = D_MODEL // LANES
PACKED_ROWS = SLAB_ROWS // 2
ROW_TILE = 1024
FFN_ROW_TILE = 512
COMBINE_ROW_TILE = 512
MIXER_ROW_TILE = 512
LOG2_E = 1.4426950408889634
LN_2 = 0.6931471805599453
ATTN_Q_BLOCKS = 16
DMA_ISSUE_UNROLL = 8
DMA_QUEUES = 2
VMEM_LIMIT = 56 * 1024 * 1024


def _cparams(*sem):
    return pltpu.CompilerParams(dimension_semantics=sem, vmem_limit_bytes=VMEM_LIMIT)


def _rms(x, g):
    return x * lax.rsqrt(jnp.mean(x * x, axis=-1, keepdims=True) + EPS) * g


def _norm_proj_kernel(x_ref, g_ref, w_ref, o_ref):
    xn = _rms(x_ref[...], g_ref[...]).astype(BF16)
    o_ref[...] = jnp.dot(xn, w_ref[...], preferred_element_type=F32).astype(o_ref.dtype)


def _layer_spec(w, layer):
    return pl.BlockSpec((None,) + w.shape[1:], lambda *_: (layer, 0, 0),
                        pipeline_mode=pl.Buffered(1))


def _norm_proj(x, g, w, layer, out_dtype, tm):
    T, D = x.shape
    N = w.shape[-1]
    return pl.pallas_call(
        _norm_proj_kernel,
        grid=(T // tm,),
        in_specs=[pl.BlockSpec((tm, D), lambda i: (i, 0)),
                  pl.BlockSpec((1, D), lambda i: (0, 0)),
                  _layer_spec(w, layer)],
        out_specs=pl.BlockSpec((tm, N), lambda i: (i, 0)),
        out_shape=jax.ShapeDtypeStruct((T, N), out_dtype),
        compiler_params=_cparams("parallel"),
        name="norm_proj",
    )(x, g, w)


def _in_proj_kernel(x_ref, g_ref, w_ref, *refs):
    nd = len(DILATIONS)
    qkv_refs, rest_ref, acc_refs = refs[:nd], refs[nd], refs[nd + 1:]
    per, tm, _ = acc_refs[0].shape
    nq = len(acc_refs) * MXU_COLS
    xn = _rms(x_ref[...], g_ref[...]).astype(BF16)
    for nb, acc_ref in enumerate(acc_refs):
        cols = slice(nb * MXU_COLS, (nb + 1) * MXU_COLS)
        acc = jnp.dot(xn, w_ref[:, cols], preferred_element_type=F32)
        for c in range(per):
            acc_ref[c] = acc[:, c * LANES:(c + 1) * LANES]
        for d, qkv_ref in zip(DILATIONS, qkv_refs):
            for r in range(d):
                rows = [acc_ref[c, pl.ds(r, tm // d, stride=d), :] for c in range(per)]
                qkv_ref[0, r, :, cols] = jnp.concatenate(rows, axis=1).astype(BF16)
    rest_ref[...] = jnp.dot(xn, w_ref[:, nq:], preferred_element_type=F32)


def _in_proj(x, g, w, layer, B, S, tm):
    T, D = x.shape
    N = w.shape[-1]
    nq = 3 * ATTN_WIDTH
    tps = S // tm
    res_spec = lambda d: pl.BlockSpec((1, d, tm // d, nq), lambda i: (i // tps, 0, i % tps, 0))
    return pl.pallas_call(
        _in_proj_kernel,
        grid=(T // tm,),
        in_specs=[pl.BlockSpec((tm, D), lambda i: (i, 0)),
                  pl.BlockSpec((1, D), lambda i: (0, 0)),
                  _layer_spec(w, layer)],
        out_specs=[res_spec(d) for d in DILATIONS]
                  + [pl.BlockSpec((tm, N - nq), lambda i: (i, 0))],
        out_shape=[jax.ShapeDtypeStruct((B, d, S // d, nq), BF16) for d in DILATIONS]
                  + [jax.ShapeDtypeStruct((T, N - nq), F32)],
        scratch_shapes=[pltpu.VMEM((MXU_COLS // LANES, tm, LANES), F32)] * (nq // MXU_COLS),
        compiler_params=_cparams("parallel"),
        name="in_proj",
    )(x, g, w)


def _t5_causal_bucket(dist):
    max_exact = REL_BUCKETS // 2
    d = np.maximum(dist, 1).astype(np.float32)
    large = max_exact + (np.log(d / max_exact) / np.log(REL_MAX_DIST / max_exact)
                         * (REL_BUCKETS - max_exact)).astype(np.int32)
    large = np.minimum(large, REL_BUCKETS - 1)
    return np.where(dist < max_exact, dist, large).astype(np.int32)


def _band_bias(rel_bias, dil):
    row = np.arange(Q_BLOCK)[:, None]
    col = np.arange(2 * Q_BLOCK)[None, :]
    steps = Q_BLOCK + row - col
    valid = (steps >= 0) & (steps <= WINDOW_STEPS)
    bucket = _t5_causal_bucket(dil * np.clip(steps, 0, WINDOW_STEPS))
    onehot = (bucket[..., None] == np.arange(REL_BUCKETS)).astype(np.float32)
    b = jnp.einsum('qcb,bh->hqc', jnp.asarray(onehot), rel_bias.astype(F32),
                   precision=lax.Precision.HIGHEST)
    masks = np.stack([valid, valid & (col >= Q_BLOCK)])
    return jnp.where(jnp.asarray(masks)[:, None], b[None], NEG_INF)


def _window_attn_kernel(q_ref, kp_ref, k_ref, vp_ref, v_ref, bias_ref, o_ref, lse_ref, *,
                        nseq, nq):
    first_step = jnp.where(pl.program_id(1) == 0, 1, 0)
    lane = lax.broadcasted_iota(jnp.int32, (Q_BLOCK, LANES), 1)
    low_half = lane < ATTN_HEAD_DIM
    Q = Q_BLOCK
    for n, j in [(n, j) for n in range(nseq) for j in range(nq)]:
        rows = slice(j * Q, (j + 1) * Q)
        if j == 0:
            k2 = jnp.concatenate([kp_ref[n], k_ref[n, :Q, :]], axis=0)
            v2 = jnp.concatenate([vp_ref[n], v_ref[n, :Q, :]], axis=0)
            table = first_step
        else:
            k2 = k_ref[n, (j - 1) * Q:(j + 1) * Q, :]
            v2 = v_ref[n, (j - 1) * Q:(j + 1) * Q, :]
            table = 0
        lse = jnp.zeros((Q, LANES), F32)
        for g in range(ATTN_WIDTH // LANES):
            cols = slice(g * LANES, (g + 1) * LANES)
            qg, kg, vg = q_ref[n, rows, cols], k2[:, cols], v2[:, cols]
            og = None
            for half in range(2):
                h = 2 * g + half
                keep = low_half if half == 0 else jnp.logical_not(low_half)
                qh = jnp.where(keep, qg, jnp.zeros_like(qg))
                s = lax.dot_general(qh, kg, (((1,), (1,)), ((), ())),
                                    preferred_element_type=F32)
                s = s + bias_ref[table, h]
                m = jnp.max(s, axis=-1, keepdims=True)
                p = jnp.exp2(s - m)
                l = jnp.sum(p, axis=-1, keepdims=True)
                o = jnp.dot(p.astype(BF16), vg, preferred_element_type=F32) * (1.0 / l)
                og = o if half == 0 else jnp.where(low_half, og, o)
                lse = jnp.where(lane == h, (m + jnp.log2(l)) * LN_2, lse)
            o_ref[n, rows, cols] = og.astype(o_ref.dtype)
        lse_ref[n, rows, :] = lse


def _window_attn(qkv, bias, blocks_per_step):
    N, L, _ = qkv.shape
    W = ATTN_WIDTH
    nq = min(blocks_per_step, L // Q_BLOCK)
    nseq = min(max(blocks_per_step // nq, 1), N)
    assert N % nseq == 0
    step = nq * Q_BLOCK
    prev = lambda i: jnp.maximum(i * nq - 1, 0)
    return pl.pallas_call(
        partial(_window_attn_kernel, nseq=nseq, nq=nq),
        grid=(N // nseq, L // step),
        in_specs=[pl.BlockSpec((nseq, step, W), lambda n, i: (n, i, 0)),
                  pl.BlockSpec((nseq, Q_BLOCK, W), lambda n, i: (n, prev(i), 1)),
                  pl.BlockSpec((nseq, step, W), lambda n, i: (n, i, 1)),
                  pl.BlockSpec((nseq, Q_BLOCK, W), lambda n, i: (n, prev(i), 2)),
                  pl.BlockSpec((nseq, step, W), lambda n, i: (n, i, 2)),
                  pl.BlockSpec(bias.shape, lambda n, i: (0, 0, 0, 0))],
        out_specs=[pl.BlockSpec((nseq, step, W), lambda n, i: (n, i, 0)),
                   pl.BlockSpec((nseq, step, LANES), lambda n, i: (n, i, 0))],
        out_shape=[jax.ShapeDtypeStruct((N, L, W), BF16),
                   jax.ShapeDtypeStruct((N, L, LANES), F32)],
        compiler_params=_cparams("parallel", "parallel"),
        name="window_attn",
    )(qkv, qkv, qkv, qkv, qkv, bias)


def _token_order(src_ref, scratch_ref, d):
    if d == 1:
        return src_ref[0, 0].astype(F32)
    nchunk, n, _ = scratch_ref.shape
    for r in range(d):
        part = src_ref[0, r].astype(F32)
        for c in range(nchunk):
            scratch_ref[c, pl.ds(r, n // d, stride=d), :] = part[:, c * LANES:(c + 1) * LANES]
    return jnp.concatenate([scratch_ref[c] for c in range(nchunk)], axis=1)


def _memory_xattn(x, g_ref, wq_ref, kv_ref, wo_ref):
    xn = _rms(x, g_ref[...]).astype(BF16)
    q = jnp.dot(xn, wq_ref[...], preferred_element_type=F32).astype(BF16)
    heads = []
    for h in range(XATTN_HEADS):
        sl = slice(h * XATTN_HEAD_DIM, (h + 1) * XATTN_HEAD_DIM)
        k = kv_ref[0, :, sl]
        v = kv_ref[0, :, XATTN_WIDTH + h * XATTN_HEAD_DIM:XATTN_WIDTH + (h + 1) * XATTN_HEAD_DIM]
        s = lax.dot_general(q[:, sl], k, (((1,), (1,)), ((), ())), preferred_element_type=F32)
        m = jnp.max(s, axis=-1, keepdims=True)
        p = jnp.exp2(s - m)
        l = jnp.sum(p, axis=-1, keepdims=True)
        heads.append(jnp.dot(p.astype(BF16), v, preferred_element_type=F32) * (1.0 / l))
    o = jnp.concatenate(heads, axis=1).astype(BF16)
    return x + jnp.dot(o, wo_ref[...], preferred_element_type=F32)


def _mixer_out_kernel(x_ref, o1_ref, o2_ref, o3_ref, l1_ref, l2_ref, l3_ref, rest_ref, halo_ref,
                      convw_ref, poolw_ref, pscale_ref, pwin_ref, expand_ref, wout_ref,
                      xg_ref, wq_ref, kv_ref, wo_ref, fg_ref, wr_ref, out_ref, *refs,
                      tm, tiles_per_seq, routed):
    os2_ref, os3_ref, ls2_ref, ls3_ref = refs[-4:]
    it = pl.program_id(0) % tiles_per_seq
    lses = [_token_order(r, s, d)
            for r, s, d in zip((l1_ref, l2_ref, l3_ref), (None, ls2_ref, ls3_ref), DILATIONS)]
    outs = [_token_order(r, s, d)
            for r, s, d in zip((o1_ref, o2_ref, o3_ref), (None, os2_ref, os3_ref), DILATIONS)]
    lm = jnp.maximum(jnp.maximum(lses[0], lses[1]), lses[2])
    es = [jnp.exp(l - lm) for l in lses]
    inv = 1.0 / (es[0] + es[1] + es[2])
    y_attn = jnp.zeros((tm, ATTN_WIDTH), F32)
    for e, o in zip(es, outs):
        w = e * inv
        w_hi = w.astype(BF16)
        w_lo = (w - w_hi.astype(F32)).astype(BF16)
        wide = jnp.dot(jnp.concatenate([w_hi, w_lo], axis=1), expand_ref[...],
                       preferred_element_type=F32)
        y_attn = y_attn + wide * o
    y_attn = y_attn.astype(BF16)

    rest = rest_ref[...]
    halo = jnp.where(it == 0, 0.0, halo_ref[...])
    ext = jnp.concatenate([halo, rest], axis=0)

    u = ext[:, :POOL_WIDTH]
    s2 = u + pltpu.roll(u, 1, 0)
    s4 = s2 + pltpu.roll(s2, 2, 0)
    s8 = s4 + pltpu.roll(s4, 4, 0)
    s16 = s8 + pltpu.roll(s8, 8, 0)
    col = lax.broadcasted_iota(jnp.int32, u.shape, 1)
    g = POOL_GROUP_DIM
    wsum = jnp.where(col < g, s2, jnp.where(col < 2 * g, s4, jnp.where(col < 3 * g, s8, s16)))
    pos = it * tm + lax.broadcasted_iota(jnp.int32, (tm, POOL_WIDTH), 0)
    count = jnp.minimum((pos + 1).astype(F32), pwin_ref[...])
    pooled = wsum[POOL_HALO:] / count - u[POOL_HALO:]
    y_pool = jnp.dot(pooled.astype(BF16), poolw_ref[...], preferred_element_type=F32)
    y_pool = (y_pool * pscale_ref[...]).astype(BF16)

    b_gate = rest[:, POOL_WIDTH:POOL_WIDTH + CONV_WIDTH]
    cu = ext[:, POOL_WIDTH + CONV_WIDTH:POOL_WIDTH + 2 * CONV_WIDTH] * ext[:, POOL_WIDTH + 2 * CONV_WIDTH:]
    y = (convw_ref[0:1, :] * cu + convw_ref[1:2, :] * pltpu.roll(cu, 1, 0)
         + convw_ref[2:3, :] * pltpu.roll(cu, 2, 0))
    y_conv = (b_gate * y[POOL_HALO:]).astype(BF16)

    a0, a1 = ATTN_WIDTH, ATTN_WIDTH + POOL_WIDTH
    acc = jnp.dot(y_attn, wout_ref[:a0, :], preferred_element_type=F32)
    acc += jnp.dot(y_pool, wout_ref[a0:a1, :], preferred_element_type=F32)
    acc += jnp.dot(y_conv, wout_ref[a1:, :], preferred_element_type=F32)
    x_new = _memory_xattn(x_ref[...] + acc, xg_ref, wq_ref, kv_ref, wo_ref)
    out_ref[...] = x_new
    if routed:
        _route_tokens(x_new, fg_ref, wr_ref, *refs[:2])


def _mixer_out(x, outs, lses, rest, conv_w, pool_w_bd, pool_scale, w_out, xg, wq, kv, wo, layer,
               ffn_g, wr, routed, S, tm):
    T, D = x.shape
    M = kv.shape[1]
    hb = tm // POOL_HALO
    tps = S // tm
    row = lambda i: (i, 0)
    const = lambda i: (0, 0)
    res_spec = lambda d, c: pl.BlockSpec((1, d, tm // d, c), lambda i: (i // tps, 0, i % tps, 0))
    pwin = jnp.asarray(np.repeat(np.array(POOL_WINDOWS, np.float32), POOL_GROUP_DIM)[None, :])
    expand = np.zeros((2, LANES, ATTN_WIDTH), np.float32)
    for h in range(ATTN_HEADS):
        expand[:, h, h * ATTN_HEAD_DIM:(h + 1) * ATTN_HEAD_DIM] = 1.0
    expand = jnp.asarray(expand.reshape(2 * LANES, ATTN_WIDTH), dtype=BF16)
    out_specs = [pl.BlockSpec((tm, D), row)]
    out_shape = [jax.ShapeDtypeStruct((T, D), F32)]
    if routed:
        out_specs += [pl.BlockSpec((tm * PACKED_ROWS, LANES), row), pl.BlockSpec((tm, LANES), row)]
        out_shape += [jax.ShapeDtypeStruct((T * PACKED_ROWS, LANES), jnp.uint32),
                      jax.ShapeDtypeStruct((T, LANES), F32)]
    res = pl.pallas_call(
        partial(_mixer_out_kernel, tm=tm, tiles_per_seq=tps, routed=routed),
        grid=(T // tm,),
        scratch_shapes=[pltpu.VMEM((ATTN_WIDTH // LANES, tm, LANES), F32)] * 2
                       + [pltpu.VMEM((1, tm, LANES), F32)] * 2,
        name="mixer_out",
        in_specs=[pl.BlockSpec((tm, D), row)]
                 + [res_spec(d, ATTN_WIDTH) for d in DILATIONS]
                 + [res_spec(d, LANES) for d in DILATIONS]
                 + [pl.BlockSpec((tm, REST_WIDTH), row),
                    pl.BlockSpec((POOL_HALO, REST_WIDTH), lambda i: (jnp.maximum(i * hb - 1, 0), 0)),
                    pl.BlockSpec(conv_w.shape, const),
                    pl.BlockSpec(pool_w_bd.shape, const),
                    pl.BlockSpec((1, POOL_WIDTH), const),
                    pl.BlockSpec((1, POOL_WIDTH), const),
                    pl.BlockSpec(expand.shape, const),
                    _layer_spec(w_out, layer),
                    pl.BlockSpec((1, D), const),
                    _layer_spec(wq, layer),
                    pl.BlockSpec((1, M, 2 * XATTN_WIDTH), lambda i: (i // tps, 0, 0)),
                    _layer_spec(wo, layer),
                    pl.BlockSpec((1, D), const),
                    pl.BlockSpec(wr.shape, const)],
        out_specs=out_specs,
        out_shape=out_shape,
        compiler_params=_cparams("parallel"),
    )(x, *outs, *lses, rest, rest, conv_w, pool_w_bd, pool_scale, pwin, expand, w_out,
      xg, wq, kv, wo, ffn_g, wr)
    return res if routed else res[0]


def _swiglu(hn, w13_ref, w2_ref):
    dff = w2_ref.shape[0]
    a = jnp.dot(hn, w13_ref[:, :dff], preferred_element_type=F32)
    b = jnp.dot(hn, w13_ref[:, dff:], preferred_element_type=F32)
    h = (a * jax.nn.sigmoid(a) * b).astype(BF16)
    return jnp.dot(h, w2_ref[...], preferred_element_type=F32)


def _dense_ffn_kernel(x_ref, g_ref, w13_ref, w2_ref, out_ref):
    x = x_ref[...]
    out_ref[...] = x + _swiglu(_rms(x, g_ref[...]).astype(BF16), w13_ref, w2_ref)


def _dense_ffn(x, g, w13, w2, layer, tm):
    T, D = x.shape
    return pl.pallas_call(
        _dense_ffn_kernel,
        grid=(T // tm,),
        in_specs=[pl.BlockSpec((tm, D), lambda i: (i, 0)),
                  pl.BlockSpec((1, D), lambda i: (0, 0)),
                  _layer_spec(w13, layer),
                  _layer_spec(w2, layer)],
        out_specs=pl.BlockSpec((tm, D), lambda i: (i, 0)),
        out_shape=jax.ShapeDtypeStruct((T, D), F32),
        compiler_params=_cparams("parallel"),
        name="dense_ffn",
    )(x, g, w13, w2)


def _store_slab(ref, v):
    n, rows = v.shape[0], v.shape[1] // LANES
    for s in range(rows):
        ref[pl.ds(s, n, stride=rows), :] = v[:, s * LANES:(s + 1) * LANES]


def _load_slab(ref, first, n, rows):
    return jnp.concatenate(
        [ref[pl.ds(first * rows + s, n, stride=rows), :] for s in range(rows)], axis=1)


def _slab(row, rows):
    return pl.ds(pl.multiple_of(row * rows, rows), rows)


def _pack_bf16_pairs(v):
    half = v.shape[1] // 2
    bits = pltpu.bitcast(v.astype(BF16).astype(F32), jnp.uint32)
    return (bits[:, half:] & jnp.uint32(0xFFFF0000)) | (bits[:, :half] >> 16)


def _unpack_bf16_pairs(u):
    low = pltpu.bitcast(u << 16, F32)
    high = pltpu.bitcast(u & jnp.uint32(0xFFFF0000), F32)
    return jnp.concatenate([low, high], axis=1).astype(BF16)


def _route_tokens(x, g_ref, wr_ref, hn_ref, route_ref):
    hn = _rms(x, g_ref[...])
    _store_slab(hn_ref, _pack_bf16_pairs(hn))
    logits = jnp.dot(hn.astype(BF16), wr_ref[...], preferred_element_type=F32)
    lane = lax.broadcasted_iota(jnp.int32, logits.shape, 1).astype(F32)
    logits = jnp.where(lane < N_EXPERTS, logits, -jnp.inf)
    m1 = jnp.max(logits, axis=-1, keepdims=True)
    i1 = jnp.min(jnp.where(logits == m1, lane, float(LANES)), axis=-1, keepdims=True)
    rest = jnp.where(lane == i1, -jnp.inf, logits)
    m2 = jnp.max(rest, axis=-1, keepdims=True)
    i2 = jnp.min(jnp.where(rest == m2, lane, float(LANES)), axis=-1, keepdims=True)
    e2 = jnp.exp(m2 - m1)
    p1 = 1.0 / (1.0 + e2)
    p2 = e2 * p1
    route_ref[...] = jnp.where(lane == 0, i1, jnp.where(lane == 1, i2,
                     jnp.where(lane == 2, p1, jnp.where(lane == 3, p2, 0.0))))


def _start_row_copy(src_ref, src_row, dst_ref, dst_row, sem, rows, queue):
    pltpu.async_copy(src_ref.at[_slab(src_row, rows)], dst_ref.at[_slab(dst_row, rows)], sem,
                     priority=queue)


def _wait_rows(src_ref, dst_ref, n, sem, rows):
    span = pl.ds(0, n * rows)
    pltpu.make_async_copy(src_ref.at[span], dst_ref.at[span], sem).wait()


def _dispatch_kernel(pos_ref, pad_ref, hn_ref, xs_ref, sem, *, tm):
    i = pl.program_id(0)
    base = i * tm * TOP_K

    def start(r, c):
        for kk in range(TOP_K):
            _start_row_copy(hn_ref, r, xs_ref, pos_ref[base + r * TOP_K + kk], sem, PACKED_ROWS,
                            kk % DMA_QUEUES)
        return c

    lax.fori_loop(0, tm, start, 0, unroll=DMA_ISSUE_UNROLL)
    _wait_rows(xs_ref, xs_ref, tm * TOP_K, sem, PACKED_ROWS)

    @pl.when(i == 0)
    def _():
        npad = pad_ref.shape[0]

        def pstart(r, c):
            for q in range(DMA_QUEUES):
                _start_row_copy(hn_ref, 0, xs_ref, pad_ref[r * DMA_QUEUES + q], sem, PACKED_ROWS, q)
            return c

        lax.fori_loop(0, npad // DMA_QUEUES, pstart, 0, unroll=DMA_ISSUE_UNROLL)
        _wait_rows(xs_ref, xs_ref, npad, sem, PACKED_ROWS)


def _dispatch(hn_slab, pos, pad_rows, n_rows, tm):
    T = hn_slab.shape[0] // PACKED_ROWS
    return pl.pallas_call(
        partial(_dispatch_kernel, tm=tm),
        grid_spec=pltpu.PrefetchScalarGridSpec(
            num_scalar_prefetch=2,
            grid=(T // tm,),
            in_specs=[pl.BlockSpec((tm * PACKED_ROWS, LANES), lambda i, *_: (i, 0))],
            out_specs=pl.BlockSpec(memory_space=pl.ANY),
            scratch_shapes=[pltpu.SemaphoreType.DMA(())]),
        out_shape=jax.ShapeDtypeStruct((n_rows * PACKED_ROWS, LANES), jnp.uint32),
        compiler_params=_cparams("arbitrary"),
        name="dispatch",
    )(pos, pad_rows, hn_slab)


def _expert_ffn_kernel(te_ref, tv_ref, xs_ref, w13_ref, w2_ref, y_ref, *, tm):
    valid = tv_ref[pl.program_id(0)] > 0

    @pl.when(valid)
    def _():
        hn = _unpack_bf16_pairs(_load_slab(xs_ref, 0, tm, PACKED_ROWS))
        _store_slab(y_ref, _swiglu(hn, w13_ref, w2_ref))

    @pl.when(jnp.logical_not(valid))
    def _():
        y_ref[...] = jnp.zeros_like(y_ref)


def _expert_ffn(xs, tile_expert, tile_valid, w13, w2, layer, tm):
    n_rows = xs.shape[0] // PACKED_ROWS
    expert_spec = lambda w: pl.BlockSpec((None, None) + w.shape[2:],
                                         lambda i, te, tv: (layer, te[i], 0, 0),
                                         pipeline_mode=pl.Buffered(1))
    return pl.pallas_call(
        partial(_expert_ffn_kernel, tm=tm),
        grid_spec=pltpu.PrefetchScalarGridSpec(
            num_scalar_prefetch=2,
            grid=(n_rows // tm,),
            in_specs=[pl.BlockSpec((tm * PACKED_ROWS, LANES), lambda i, te, tv: (i, 0)),
                      expert_spec(w13), expert_spec(w2)],
            out_specs=pl.BlockSpec((tm * SLAB_ROWS, LANES), lambda i, te, tv: (i, 0))),
        out_shape=jax.ShapeDtypeStruct((n_rows * SLAB_ROWS, LANES), F32),
        compiler_params=_cparams("parallel"),
        name="expert_ffn",
    )(tile_expert, tile_valid, xs, w13, w2)


def _combine_kernel(pos_ref, x_ref, route_ref, g_ref, y_ref, out_ref, buf, sem, *, tm,
                    final_norm):
    i = pl.program_id(0)

    def issue(tile, slot):
        base = tile * tm * TOP_K

        def start(r, c):
            for kk in range(TOP_K):
                _start_row_copy(y_ref, pos_ref[base + r * TOP_K + kk], buf.at[slot], kk * tm + r,
                                sem.at[slot], SLAB_ROWS, kk % DMA_QUEUES)
            return c

        lax.fori_loop(0, tm, start, 0, unroll=DMA_ISSUE_UNROLL)

    @pl.when(i == 0)
    def _():
        issue(0, 0)

    @pl.when(i + 1 < pl.num_programs(0))
    def _():
        issue(i + 1, (i + 1) % 2)

    slot = i % 2
    _wait_rows(y_ref, buf.at[slot], tm * TOP_K, sem.at[slot], SLAB_ROWS)
    route = route_ref[...]
    acc = x_ref[...]
    for kk in range(TOP_K):
        gate = route[:, TOP_K + kk:TOP_K + kk + 1]
        acc = acc + gate * _load_slab(buf.at[slot], kk * tm, tm, SLAB_ROWS)
    out_ref[...] = _rms(acc, g_ref[...]) if final_norm else acc


def _combine(x, route, y, pos, final_g, final_norm, tm):
    T, D = x.shape
    return pl.pallas_call(
        partial(_combine_kernel, tm=tm, final_norm=final_norm),
        grid_spec=pltpu.PrefetchScalarGridSpec(
            num_scalar_prefetch=1,
            grid=(T // tm,),
            in_specs=[pl.BlockSpec((tm, D), lambda i, *_: (i, 0)),
                      pl.BlockSpec((tm, LANES), lambda i, *_: (i, 0)),
                      pl.BlockSpec((1, D), lambda i, *_: (0, 0)),
                      pl.BlockSpec(memory_space=pl.ANY)],
            out_specs=pl.BlockSpec((tm, D), lambda i, *_: (i, 0)),
            scratch_shapes=[pltpu.VMEM((2, tm * TOP_K * SLAB_ROWS, LANES), F32),
                            pltpu.SemaphoreType.DMA((2,))]),
        out_shape=jax.ShapeDtypeStruct((T, D), F32),
        compiler_params=_cparams("arbitrary"),
        name="combine",
    )(pos, x, route, final_g, y)


def _routing_tables(expert_ids, tm, n_tiles):
    e_flat = expert_ids.reshape(-1)
    onehot = (e_flat[:, None] == jnp.arange(N_EXPERTS, dtype=jnp.int32)[None, :]).astype(jnp.int32)
    csum = jnp.cumsum(onehot, axis=0)
    counts = csum[-1]
    rank = jnp.sum(onehot * (csum - 1), axis=1)
    padded = ((counts + tm - 1) // tm) * tm
    ends = jnp.cumsum(padded)
    starts = ends - padded
    pos = (jnp.sum(onehot * starts[None, :], axis=1) + rank).astype(jnp.int32)

    tile_start = jnp.arange(n_tiles, dtype=jnp.int32) * tm
    tile_valid = (tile_start < ends[-1]).astype(jnp.int32)
    te = jnp.sum((tile_start[:, None] >= ends[None, :]).astype(jnp.int32), axis=1)
    last_e = jnp.sum((ends[-1] - 1 >= ends).astype(jnp.int32))
    tile_expert = jnp.where(tile_valid > 0, te, last_e).astype(jnp.int32)

    npad_e = padded - counts
    pcum = jnp.cumsum(npad_e)
    k = jnp.arange(N_EXPERTS * tm, dtype=jnp.int32)
    ek = jnp.sum((k[:, None] >= pcum[None, :]).astype(jnp.int32), axis=1)
    ohk = (ek[:, None] == jnp.arange(N_EXPERTS, dtype=jnp.int32)[None, :]).astype(jnp.int32)
    in_group = jnp.sum(ohk * (starts + counts - (pcum - npad_e))[None, :], axis=1) + k
    pad = jnp.where(ek < N_EXPERTS, in_group, ends[-1] + k - pcum[-1]).astype(jnp.int32)
    return pos, tile_expert, tile_valid, pad


def _moe_ffn(x, hn_slab, route, w13, w2, layer, final_g, final_norm, tm, tme):
    T, D = x.shape
    n_tiles = (T * TOP_K) // tme + N_EXPERTS
    expert_ids = route[:, :TOP_K].astype(jnp.int32)
    pos, tile_expert, tile_valid, pad = _routing_tables(expert_ids, tme, n_tiles)
    xs = _dispatch(hn_slab, pos, pad, n_tiles * tme, tm)
    y = _expert_ffn(xs, tile_expert, tile_valid, w13, w2, layer, tme)
    return _combine(x, route, y, pos, final_g, final_norm, min(COMBINE_ROW_TILE, tm))


def _final_norm_kernel(x_ref, g_ref, o_ref):
    o_ref[...] = _rms(x_ref[...], g_ref[...])


def _final_norm(x, g, tm):
    T, D = x.shape
    return pl.pallas_call(
        _final_norm_kernel,
        grid=(T // tm,),
        in_specs=[pl.BlockSpec((tm, D), lambda i: (i, 0)), pl.BlockSpec((1, D), lambda i: (0, 0))],
        out_specs=pl.BlockSpec((tm, D), lambda i: (i, 0)),
        out_shape=jax.ShapeDtypeStruct((T, D), F32),
        compiler_params=_cparams("parallel"),
        name="final_norm",
    )(x, g)


def _block_diag(pw):
    G, C, _ = pw.shape
    eye = jnp.eye(G, dtype=pw.dtype)
    return (eye[:, None, :, None] * pw[:, :, None, :]).reshape(G * C, G * C)


def kernel(x, mem, rel_bias, mem_norm_g, mix_norm_g, w_in, conv_w, pool_w, pool_scale, w_out,
           xattn_norm_g, wq_x, wkv_x, wo_x, ffn_norm_g, w13_dense, w2_dense, router_w, w13_moe,
           w2_moe, final_norm_g):
    B, S, D = x.shape
    depth = w_in.shape[0]
    M = mem.shape[1]
    T = B * S
    tm = min(ROW_TILE, S)
    assert D == D_MODEL and S % (max(DILATIONS) * Q_BLOCK) == 0 and S % tm == 0

    xf = x.reshape(T, D)
    row2 = lambda v: v.reshape(1, -1)
    q_scale = jnp.where(jnp.arange(w_in.shape[-1]) < ATTN_WIDTH,
                        ATTN_HEAD_DIM ** -0.5 * LOG2_E, 1.0)
    w_in_b = (w_in * q_scale).astype(BF16)
    biases = [_band_bias(rel_bias * LOG2_E, d) for d in DILATIONS]
    wr_pad = jnp.pad(router_w, ((0, 0), (0, 0), (0, LANES - N_EXPERTS))).astype(BF16)
    w_out_b, wkv_b, wo_b = (w.astype(BF16) for w in (w_out, wkv_x, wo_x))
    wq_b = (wq_x * (XATTN_HEAD_DIM ** -0.5 * LOG2_E)).astype(BF16)
    w13_dense_b, w2_dense_b = w13_dense.astype(BF16), w2_dense.astype(BF16)
    w13_moe_b, w2_moe_b = w13_moe.astype(BF16), w2_moe.astype(BF16)

    mem_f = mem.reshape(B * M, D)
    for layer in range(depth):
        *qkvs, rest = _in_proj(xf, row2(mix_norm_g[layer]), w_in_b, layer, B, S, tm)
        branch = []
        for qkv, bias, d in zip(qkvs, biases, DILATIONS):
            o, lse = _window_attn(qkv.reshape(B * d, S // d, qkv.shape[-1]), bias, ATTN_Q_BLOCKS)
            branch.append((o.reshape(B, d, S // d, -1), lse.reshape(B, d, S // d, -1)))
        kv = _norm_proj(mem_f, row2(mem_norm_g), wkv_b, layer, BF16, min(tm, B * M))
        g = row2(ffn_norm_g[layer])
        routed = layer % 2 == 1
        res = _mixer_out(xf, [o for o, _ in branch], [l for _, l in branch], rest,
                         conv_w[layer], _block_diag(pool_w[layer]).astype(BF16),
                         row2(pool_scale[layer]), w_out_b, row2(xattn_norm_g[layer]), wq_b,
                         kv.reshape(B, M, 2 * XATTN_WIDTH), wo_b, layer, g,
                         wr_pad[min(layer // 2, wr_pad.shape[0] - 1)], routed, S,
                         min(MIXER_ROW_TILE, tm) if routed else tm)
        last = layer == depth - 1
        if not routed:
            xf = _dense_ffn(res, g, w13_dense_b, w2_dense_b, layer // 2, min(FFN_ROW_TILE, tm))
            if last:
                xf = _final_norm(xf, row2(final_norm_g), tm)
        else:
            xf, hn_slab, route = res
            xf = _moe_ffn(xf, hn_slab, route, w13_moe_b, w2_moe_b, layer // 2,
                          row2(final_norm_g), last, tm, FFN_ROW_TILE)
    return xf.reshape(B, S, D)
```

```python
from functools import partial

import numpy as np
import jax
import jax.numpy as jnp
from jax import lax
from jax.experimental import pallas as pl
from jax.experimental.pallas import tpu as pltpu

F32 = jnp.float32
BF16 = jnp.bfloat16

D_MODEL = 1024
ATTN_HEADS = 8
ATTN_HEAD_DIM = 64
ATTN_WIDTH = ATTN_HEADS * ATTN_HEAD_DIM
POOL_WIDTH = 256
POOL_GROUP_DIM = 64
POOL_WINDOWS = (2, 4, 8, 16)
POOL_HALO = 16
CONV_WIDTH = 256
REST_WIDTH = POOL_WIDTH + 3 * CONV_WIDTH
DILATIONS = (1, 4, 16)
WINDOW_STEPS = 128
Q_BLOCK = 128
NEG_INF = -1e30
REL_BUCKETS = 32
REL_MAX_DIST = 128
XATTN_HEADS = 4
XATTN_HEAD_DIM = 128
XATTN_WIDTH = XATTN_HEADS * XATTN_HEAD_DIM
N_EXPERTS = 8
TOP_K = 2
EPS = 1e-6

LANES = 128
MXU_COLS = 256
SLAB_ROWS = D_MODEL // LANES
PACKED_ROWS = SLAB_ROWS // 2
ROW_TILE = 1024
FFN_ROW_TILE = 512
COMBINE_ROW_TILE = 512
MIXER_ROW_TILE = 512
LOG2_E = 1.4426950408889634
LN_2 = 0.6931471805599453
ATTN_Q_BLOCKS = 16
DMA_ISSUE_UNROLL = 8
DMA_QUEUES = 2
VMEM_LIMIT = 56 * 1024 * 1024


def _cparams(*sem):
    return pltpu.CompilerParams(dimension_semantics=sem, vmem_limit_bytes=VMEM_LIMIT)


def _rms(x, g):
    return x * lax.rsqrt(jnp.mean(x * x, axis=-1, keepdims=True) + EPS) * g


def _norm_proj_kernel(x_ref, g_ref, w_ref, o_ref):
    xn = _rms(x_ref[...], g_ref[...]).astype(BF16)
    o_ref[...] = jnp.dot(xn, w_ref[...], preferred_element_type=F32).astype(o_ref.dtype)


def _layer_spec(w, layer):
    return pl.BlockSpec((None,) + w.shape[1:], lambda *_: (layer, 0, 0),
                        pipeline_mode=pl.Buffered(1))


def _norm_proj(x, g, w, layer, out_dtype, tm):
    T, D = x.shape
    N = w.shape[-1]
    return pl.pallas_call(
        _norm_proj_kernel,
        grid=(T // tm,),
        in_specs=[pl.BlockSpec((tm, D), lambda i: (i, 0)),
                  pl.BlockSpec((1, D), lambda i: (0, 0)),
                  _layer_spec(w, layer)],
        out_specs=pl.BlockSpec((tm, N), lambda i: (i, 0)),
        out_shape=jax.ShapeDtypeStruct((T, N), out_dtype),
        compiler_params=_cparams("parallel"),
        name="norm_proj",
    )(x, g, w)


def _in_proj_kernel(x_ref, g_ref, w_ref, *refs):
    nd = len(DILATIONS)
    qkv_refs, rest_ref, acc_refs = refs[:nd], refs[nd], refs[nd + 1:]
    per, tm, _ = acc_refs[0].shape
    nq = len(acc_refs) * MXU_COLS
    xn = _rms(x_ref[...], g_ref[...]).astype(BF16)
    for nb, acc_ref in enumerate(acc_refs):
        cols = slice(nb * MXU_COLS, (nb + 1) * MXU_COLS)
        acc = jnp.dot(xn, w_ref[:, cols], preferred_element_type=F32)
        for c in range(per):
            acc_ref[c] = acc[:, c * LANES:(c + 1) * LANES]
        for d, qkv_ref in zip(DILATIONS, qkv_refs):
            for r in range(d):
                rows = [acc_ref[c, pl.ds(r, tm // d, stride=d), :] for c in range(per)]
                qkv_ref[0, r, :, cols] = jnp.concatenate(rows, axis=1).astype(BF16)
    rest_ref[...] = jnp.dot(xn, w_ref[:, nq:], preferred_element_type=F32)


def _in_proj(x, g, w, layer, B, S, tm):
    T, D = x.shape
    N = w.shape[-1]
    nq = 3 * ATTN_WIDTH
    tps = S // tm
    res_spec = lambda d: pl.BlockSpec((1, d, tm // d, nq), lambda i: (i // tps, 0, i % tps, 0))
    return pl.pallas_call(
        _in_proj_kernel,
        grid=(T // tm,),
        in_specs=[pl.BlockSpec((tm, D), lambda i: (i, 0)),
                  pl.BlockSpec((1, D), lambda i: (0, 0)),
                  _layer_spec(w, layer)],
        out_specs=[res_spec(d) for d in DILATIONS]
                  + [pl.BlockSpec((tm, N - nq), lambda i: (i, 0))],
        out_shape=[jax.ShapeDtypeStruct((B, d, S // d, nq), BF16) for d in DILATIONS]
                  + [jax.ShapeDtypeStruct((T, N - nq), F32)],
        scratch_shapes=[pltpu.VMEM((MXU_COLS // LANES, tm, LANES), F32)] * (nq // MXU_COLS),
        compiler_params=_cparams("parallel"),
        name="in_proj",
    )(x, g, w)


def _t5_causal_bucket(dist):
    max_exact = REL_BUCKETS // 2
    d = np.maximum(dist, 1).astype(np.float32)
    large = max_exact + (np.log(d / max_exact) / np.log(REL_MAX_DIST / max_exact)
                         * (REL_BUCKETS - max_exact)).astype(np.int32)
    large = np.minimum(large, REL_BUCKETS - 1)
    return np.where(dist < max_exact, dist, large).astype(np.int32)


def _band_bias(rel_bias, dil):
    row = np.arange(Q_BLOCK)[:, None]
    col = np.arange(2 * Q_BLOCK)[None, :]
    steps = Q_BLOCK + row - col
    valid = (steps >= 0) & (steps <= WINDOW_STEPS)
    bucket = _t5_causal_bucket(dil * np.clip(steps, 0, WINDOW_STEPS))
    onehot = (bucket[..., None] == np.arange(REL_BUCKETS)).astype(np.float32)
    b = jnp.einsum('qcb,bh->hqc', jnp.asarray(onehot), rel_bias.astype(F32),
                   precision=lax.Precision.HIGHEST)
    masks = np.stack([valid, valid & (col >= Q_BLOCK)])
    return jnp.where(jnp.asarray(masks)[:, None], b[None], NEG_INF)


def _window_attn_kernel(q_ref, kp_ref, k_ref, vp_ref, v_ref, bias_ref, o_ref, lse_ref, *,
                        nseq, nq):
    first_step = jnp.where(pl.program_id(1) == 0, 1, 0)
    lane = lax.broadcasted_iota(jnp.int32, (Q_BLOCK, LANES), 1)
    low_half = lane < ATTN_HEAD_DIM
    Q = Q_BLOCK
    for n, j in [(n, j) for n in range(nseq) for j in range(nq)]:
        rows = slice(j * Q, (j + 1) * Q)
        if j == 0:
            k2 = jnp.concatenate([kp_ref[n], k_ref[n, :Q, :]], axis=0)
            v2 = jnp.concatenate([vp_ref[n], v_ref[n, :Q, :]], axis=0)
            table = first_step
        else:
            k2 = k_ref[n, (j - 1) * Q:(j + 1) * Q, :]
            v2 = v_ref[n, (j - 1) * Q:(j + 1) * Q, :]
            table = 0
        m_all = jnp.zeros((Q, LANES), F32)
        l_all = jnp.ones((Q, LANES), F32)
        for g in range(ATTN_WIDTH // LANES):
            cols = slice(g * LANES, (g + 1) * LANES)
            qg, kg, vg = q_ref[n, rows, cols], k2[:, cols], v2[:, cols]
            og = lg = None
            for half in range(2):
                h = 2 * g + half
                keep = low_half if half == 0 else jnp.logical_not(low_half)
                qh = jnp.where(keep, qg, jnp.zeros_like(qg))
                s = lax.dot_general(qh, kg, (((1,), (1,)), ((), ())),
                                    preferred_element_type=F32)
                s = s + bias_ref[table, h]
                m = jnp.max(s, axis=-1, keepdims=True)
                p = jnp.exp2(s - m)
                l = jnp.sum(p, axis=-1, keepdims=True)
                o = jnp.dot(p.astype(BF16), vg, preferred_element_type=F32)
                og = o if half == 0 else jnp.where(low_half, og, o)
                lg = l if half == 0 else jnp.where(low_half, lg, l)
                m_all = jnp.where(lane == h, m, m_all)
                l_all = jnp.where(lane == h, l, l_all)
            o_ref[n, rows, cols] = (og * (1.0 / lg)).astype(o_ref.dtype)
        lse_ref[n, rows, :] = (m_all + jnp.log2(l_all)) * LN_2


def _window_attn(qkv, bias, blocks_per_step):
    N, L, _ = qkv.shape
    W = ATTN_WIDTH
    nq = min(blocks_per_step, L // Q_BLOCK)
    nseq = min(max(blocks_per_step // nq, 1), N)
    assert N % nseq == 0
    step = nq * Q_BLOCK
    prev = lambda i: jnp.maximum(i * nq - 1, 0)
    return pl.pallas_call(
        partial(_window_attn_kernel, nseq=nseq, nq=nq),
        grid=(N // nseq, L // step),
        in_specs=[pl.BlockSpec((nseq, step, W), lambda n, i: (n, i, 0)),
                  pl.BlockSpec((nseq, Q_BLOCK, W), lambda n, i: (n, prev(i), 1)),
                  pl.BlockSpec((nseq, step, W), lambda n, i: (n, i, 1)),
                  pl.BlockSpec((nseq, Q_BLOCK, W), lambda n, i: (n, prev(i), 2)),
                  pl.BlockSpec((nseq, step, W), lambda n, i: (n, i, 2)),
                  pl.BlockSpec(bias.shape, lambda n, i: (0, 0, 0, 0))],
        out_specs=[pl.BlockSpec((nseq, step, W), lambda n, i: (n, i, 0)),
                   pl.BlockSpec((nseq, step, LANES), lambda n, i: (n, i, 0))],
        out_shape=[jax.ShapeDtypeStruct((N, L, W), BF16),
                   jax.ShapeDtypeStruct((N, L, LANES), F32)],
        compiler_params=_cparams("parallel", "parallel"),
        name="window_attn",
    )(qkv, qkv, qkv, qkv, qkv, bias)


def _token_order(src_ref, scratch_ref, d):
    if d == 1:
        return src_ref[0, 0].astype(F32)
    nchunk, n, _ = scratch_ref.shape
    for r in range(d):
        part = src_ref[0, r].astype(F32)
        for c in range(nchunk):
            scratch_ref[c, pl.ds(r, n // d, stride=d), :] = part[:, c * LANES:(c + 1) * LANES]
    return jnp.concatenate([scratch_ref[c] for c in range(nchunk)], axis=1)


def _memory_xattn(x, g_ref, wq_ref, kv_ref, wo_ref):
    xn = _rms(x, g_ref[...]).astype(BF16)
    q = jnp.dot(xn, wq_ref[...], preferred_element_type=F32).astype(BF16)
    heads = []
    for h in range(XATTN_HEADS):
        sl = slice(h * XATTN_HEAD_DIM, (h + 1) * XATTN_HEAD_DIM)
        k = kv_ref[0, :, sl]
        v = kv_ref[0, :, XATTN_WIDTH + h * XATTN_HEAD_DIM:XATTN_WIDTH + (h + 1) * XATTN_HEAD_DIM]
        s = lax.dot_general(q[:, sl], k, (((1,), (1,)), ((), ())), preferred_element_type=F32)
        m = jnp.max(s, axis=-1, keepdims=True)
        p = jnp.exp2(s - m)
        l = jnp.sum(p, axis=-1, keepdims=True)
        heads.append(jnp.dot(p.astype(BF16), v, preferred_element_type=F32) * (1.0 / l))
    o = jnp.concatenate(heads, axis=1).astype(BF16)
    return x + jnp.dot(o, wo_ref[...], preferred_element_type=F32)


def _mixer_out_kernel(x_ref, o1_ref, o2_ref, o3_ref, l1_ref, l2_ref, l3_ref, rest_ref, halo_ref,
                      convw_ref, poolw_ref, pscale_ref, pwin_ref, expand_ref, wout_ref,
                      xg_ref, wq_ref, kv_ref, wo_ref, fg_ref, wr_ref, out_ref, *refs,
                      tm, tiles_per_seq, routed):
    os2_ref, os3_ref, ls2_ref, ls3_ref = refs[-4:]
    it = pl.program_id(0) % tiles_per_seq
    lses = [_token_order(r, s, d)
            for r, s, d in zip((l1_ref, l2_ref, l3_ref), (None, ls2_ref, ls3_ref), DILATIONS)]
    outs = [_token_order(r, s, d)
            for r, s, d in zip((o1_ref, o2_ref, o3_ref), (None, os2_ref, os3_ref), DILATIONS)]
    lm = jnp.maximum(jnp.maximum(lses[0], lses[1]), lses[2])
    es = [jnp.exp(l - lm) for l in lses]
    inv = 1.0 / (es[0] + es[1] + es[2])
    y_attn = jnp.zeros((tm, ATTN_WIDTH), F32)
    for e, o in zip(es, outs):
        w = e * inv
        w_hi = w.astype(BF16)
        w_lo = (w - w_hi.astype(F32)).astype(BF16)
        wide = jnp.dot(jnp.concatenate([w_hi, w_lo], axis=1), expand_ref[...],
                       preferred_element_type=F32)
        y_attn = y_attn + wide * o
    y_attn = y_attn.astype(BF16)

    rest = rest_ref[...]
    halo = jnp.where(it == 0, 0.0, halo_ref[...])
    ext = jnp.concatenate([halo, rest], axis=0)

    u = ext[:, :POOL_WIDTH]
    s2 = u + pltpu.roll(u, 1, 0)
    s4 = s2 + pltpu.roll(s2, 2, 0)
    s8 = s4 + pltpu.roll(s4, 4, 0)
    s16 = s8 + pltpu.roll(s8, 8, 0)
    col = lax.broadcasted_iota(jnp.int32, u.shape, 1)
    g = POOL_GROUP_DIM
    wsum = jnp.where(col < g, s2, jnp.where(col < 2 * g, s4, jnp.where(col < 3 * g, s8, s16)))
    pos = it * tm + lax.broadcasted_iota(jnp.int32, (tm, POOL_WIDTH), 0)
    count = jnp.minimum((pos + 1).astype(F32), pwin_ref[...])
    pooled = wsum[POOL_HALO:] / count - u[POOL_HALO:]
    y_pool = jnp.dot(pooled.astype(BF16), poolw_ref[...], preferred_element_type=F32)
    y_pool = (y_pool * pscale_ref[...]).astype(BF16)

    b_gate = rest[:, POOL_WIDTH:POOL_WIDTH + CONV_WIDTH]
    cu = ext[:, POOL_WIDTH + CONV_WIDTH:POOL_WIDTH + 2 * CONV_WIDTH] * ext[:, POOL_WIDTH + 2 * CONV_WIDTH:]
    y = (convw_ref[0:1, :] * cu + convw_ref[1:2, :] * pltpu.roll(cu, 1, 0)
         + convw_ref[2:3, :] * pltpu.roll(cu, 2, 0))
    y_conv = (b_gate * y[POOL_HALO:]).astype(BF16)

    a0, a1 = ATTN_WIDTH, ATTN_WIDTH + POOL_WIDTH
    acc = jnp.dot(y_attn, wout_ref[:a0, :], preferred_element_type=F32)
    acc += jnp.dot(y_pool, wout_ref[a0:a1, :], preferred_element_type=F32)
    acc += jnp.dot(y_conv, wout_ref[a1:, :], preferred_element_type=F32)
    x_new = _memory_xattn(x_ref[...] + acc, xg_ref, wq_ref, kv_ref, wo_ref)
    out_ref[...] = x_new
    if routed:
        _route_tokens(x_new, fg_ref, wr_ref, *refs[:2])


def _mixer_out(x, outs, lses, rest, conv_w, pool_w_bd, pool_scale, w_out, xg, wq, kv, wo, layer,
               ffn_g, wr, routed, S, tm):
    T, D = x.shape
    M = kv.shape[1]
    hb = tm // POOL_HALO
    tps = S // tm
    row = lambda i: (i, 0)
    const = lambda i: (0, 0)
    res_spec = lambda d, c: pl.BlockSpec((1, d, tm // d, c), lambda i: (i // tps, 0, i % tps, 0))
    pwin = jnp.asarray(np.repeat(np.array(POOL_WINDOWS, np.float32), POOL_GROUP_DIM)[None, :])
    expand = np.zeros((2, LANES, ATTN_WIDTH), np.float32)
    for h in range(ATTN_HEADS):
        expand[:, h, h * ATTN_HEAD_DIM:(h + 1) * ATTN_HEAD_DIM] = 1.0
    expand = jnp.asarray(expand.reshape(2 * LANES, ATTN_WIDTH), dtype=BF16)
    out_specs = [pl.BlockSpec((tm, D), row)]
    out_shape = [jax.ShapeDtypeStruct((T, D), F32)]
    if routed:
        out_specs += [pl.BlockSpec((tm * PACKED_ROWS, LANES), row), pl.BlockSpec((tm, LANES), row)]
        out_shape += [jax.ShapeDtypeStruct((T * PACKED_ROWS, LANES), jnp.uint32),
                      jax.ShapeDtypeStruct((T, LANES), F32)]
    res = pl.pallas_call(
        partial(_mixer_out_kernel, tm=tm, tiles_per_seq=tps, routed=routed),
        grid=(T // tm,),
        scratch_shapes=[pltpu.VMEM((ATTN_WIDTH // LANES, tm, LANES), F32)] * 2
                       + [pltpu.VMEM((1, tm, LANES), F32)] * 2,
        name="mixer_out",
        in_specs=[pl.BlockSpec((tm, D), row)]
                 + [res_spec(d, ATTN_WIDTH) for d in DILATIONS]
                 + [res_spec(d, LANES) for d in DILATIONS]
                 + [pl.BlockSpec((tm, REST_WIDTH), row),
                    pl.BlockSpec((POOL_HALO, REST_WIDTH), lambda i: (jnp.maximum(i * hb - 1, 0), 0)),
                    pl.BlockSpec(conv_w.shape, const),
                    pl.BlockSpec(pool_w_bd.shape, const),
                    pl.BlockSpec((1, POOL_WIDTH), const),
                    pl.BlockSpec((1, POOL_WIDTH), const),
                    pl.BlockSpec(expand.shape, const),
                    _layer_spec(w_out, layer),
                    pl.BlockSpec((1, D), const),
                    _layer_spec(wq, layer),
                    pl.BlockSpec((1, M, 2 * XATTN_WIDTH), lambda i: (i // tps, 0, 0)),
                    _layer_spec(wo, layer),
                    pl.BlockSpec((1, D), const),
                    pl.BlockSpec(wr.shape, const)],
        out_specs=out_specs,
        out_shape=out_shape,
        compiler_params=_cparams("parallel"),
    )(x, *outs, *lses, rest, rest, conv_w, pool_w_bd, pool_scale, pwin, expand, w_out,
      xg, wq, kv, wo, ffn_g, wr)
    return res if routed else res[0]


def _swiglu(hn, w13_ref, w2_ref):
    dff = w2_ref.shape[0]
    a = jnp.dot(hn, w13_ref[:, :dff], preferred_element_type=F32)
    b = jnp.dot(hn, w13_ref[:, dff:], preferred_element_type=F32)
    h = (a * jax.nn.sigmoid(a) * b).astype(BF16)
    return jnp.dot(h, w2_ref[...], preferred_element_type=F32)


def _dense_ffn_kernel(x_ref, g_ref, w13_ref, w2_ref, out_ref):
    x = x_ref[...]
    out_ref[...] = x + _swiglu(_rms(x, g_ref[...]).astype(BF16), w13_ref, w2_ref)


def _dense_ffn(x, g, w13, w2, layer, tm):
    T, D = x.shape
    return pl.pallas_call(
        _dense_ffn_kernel,
        grid=(T // tm,),
        in_specs=[pl.BlockSpec((tm, D), lambda i: (i, 0)),
                  pl.BlockSpec((1, D), lambda i: (0, 0)),
                  _layer_spec(w13, layer),
                  _layer_spec(w2, layer)],
        out_specs=pl.BlockSpec((tm, D), lambda i: (i, 0)),
        out_shape=jax.ShapeDtypeStruct((T, D), F32),
        compiler_params=_cparams("parallel"),
        name="dense_ffn",
    )(x, g, w13, w2)


def _store_slab(ref, v):
    n, rows = v.shape[0], v.shape[1] // LANES
    for s in range(rows):
        ref[pl.ds(s, n, stride=rows), :] = v[:, s * LANES:(s + 1) * LANES]


def _load_slab(ref, first, n, rows):
    return jnp.concatenate(
        [ref[pl.ds(first * rows + s, n, stride=rows), :] for s in range(rows)], axis=1)


def _slab(row, rows):
    return pl.ds(pl.multiple_of(row * rows, rows), rows)


def _pack_bf16_pairs(v):
    half = v.shape[1] // 2
    bits = pltpu.bitcast(v.astype(BF16).astype(F32), jnp.uint32)
    return (bits[:, half:] & jnp.uint32(0xFFFF0000)) | (bits[:, :half] >> 16)


def _unpack_bf16_pairs(u):
    low = pltpu.bitcast(u << 16, F32)
    high = pltpu.bitcast(u & jnp.uint32(0xFFFF0000), F32)
    return jnp.concatenate([low, high], axis=1).astype(BF16)


def _route_tokens(x, g_ref, wr_ref, hn_ref, route_ref):
    hn = _rms(x, g_ref[...])
    _store_slab(hn_ref, _pack_bf16_pairs(hn))
    logits = jnp.dot(hn.astype(BF16), wr_ref[...], preferred_element_type=F32)
    lane = lax.broadcasted_iota(jnp.int32, logits.shape, 1).astype(F32)
    logits = jnp.where(lane < N_EXPERTS, logits, -jnp.inf)
    m1 = jnp.max(logits, axis=-1, keepdims=True)
    i1 = jnp.min(jnp.where(logits == m1, lane, float(LANES)), axis=-1, keepdims=True)
    rest = jnp.where(lane == i1, -jnp.inf, logits)
    m2 = jnp.max(rest, axis=-1, keepdims=True)
    i2 = jnp.min(jnp.where(rest == m2, lane, float(LANES)), axis=-1, keepdims=True)
    e2 = jnp.exp(m2 - m1)
    p1 = 1.0 / (1.0 + e2)
    p2 = e2 * p1
    route_ref[...] = jnp.where(lane == 0, i1, jnp.where(lane == 1, i2,
                     jnp.where(lane == 2, p1, jnp.where(lane == 3, p2, 0.0))))


def _start_row_copy(src_ref, src_row, dst_ref, dst_row, sem, rows, queue):
    pltpu.async_copy(src_ref.at[_slab(src_row, rows)], dst_ref.at[_slab(dst_row, rows)], sem,
                     priority=queue)


def _wait_rows(src_ref, dst_ref, n, sem, rows):
    span = pl.ds(0, n * rows)
    pltpu.make_async_copy(src_ref.at[span], dst_ref.at[span], sem).wait()


def _dispatch_kernel(pos_ref, pad_ref, hn_ref, xs_ref, sem, *, tm):
    i = pl.program_id(0)
    base = i * tm * TOP_K

    def start(r, c):
        for kk in range(TOP_K):
            _start_row_copy(hn_ref, r, xs_ref, pos_ref[base + r * TOP_K + kk], sem, PACKED_ROWS,
                            kk % DMA_QUEUES)
        return c

    lax.fori_loop(0, tm, start, 0, unroll=DMA_ISSUE_UNROLL)
    _wait_rows(xs_ref, xs_ref, tm * TOP_K, sem, PACKED_ROWS)

    @pl.when(i == 0)
    def _():
        npad = pad_ref.shape[0]

        def pstart(r, c):
            for q in range(DMA_QUEUES):
                _start_row_copy(hn_ref, 0, xs_ref, pad_ref[r * DMA_QUEUES + q], sem, PACKED_ROWS, q)
            return c

        lax.fori_loop(0, npad // DMA_QUEUES, pstart, 0, unroll=DMA_ISSUE_UNROLL)
        _wait_rows(xs_ref, xs_ref, npad, sem, PACKED_ROWS)


def _dispatch(hn_slab, pos, pad_rows, n_rows, tm):
    T = hn_slab.shape[0] // PACKED_ROWS
    return pl.pallas_call(
        partial(_dispatch_kernel, tm=tm),
        grid_spec=pltpu.PrefetchScalarGridSpec(
            num_scalar_prefetch=2,
            grid=(T // tm,),
            in_specs=[pl.BlockSpec((tm * PACKED_ROWS, LANES), lambda i, *_: (i, 0))],
            out_specs=pl.BlockSpec(memory_space=pl.ANY),
            scratch_shapes=[pltpu.SemaphoreType.DMA(())]),
        out_shape=jax.ShapeDtypeStruct((n_rows * PACKED_ROWS, LANES), jnp.uint32),
        compiler_params=_cparams("arbitrary"),
        name="dispatch",
    )(pos, pad_rows, hn_slab)


def _expert_ffn_kernel(te_ref, tv_ref, xs_ref, w13_ref, w2_ref, y_ref, *, tm):
    valid = tv_ref[pl.program_id(0)] > 0

    @pl.when(valid)
    def _():
        hn = _unpack_bf16_pairs(_load_slab(xs_ref, 0, tm, PACKED_ROWS))
        _store_slab(y_ref, _swiglu(hn, w13_ref, w2_ref))

    @pl.when(jnp.logical_not(valid))
    def _():
        y_ref[...] = jnp.zeros_like(y_ref)


def _expert_ffn(xs, tile_expert, tile_valid, w13, w2, layer, tm):
    n_rows = xs.shape[0] // PACKED_ROWS
    expert_spec = lambda w: pl.BlockSpec((None, None) + w.shape[2:],
                                         lambda i, te, tv: (layer, te[i], 0, 0),
                                         pipeline_mode=pl.Buffered(1))
    return pl.pallas_call(
        partial(_expert_ffn_kernel, tm=tm),
        grid_spec=pltpu.PrefetchScalarGridSpec(
            num_scalar_prefetch=2,
            grid=(n_rows // tm,),
            in_specs=[pl.BlockSpec((tm * PACKED_ROWS, LANES), lambda i, te, tv: (i, 0)),
                      expert_spec(w13), expert_spec(w2)],
            out_specs=pl.BlockSpec((tm * SLAB_ROWS, LANES), lambda i, te, tv: (i, 0))),
        out_shape=jax.ShapeDtypeStruct((n_rows * SLAB_ROWS, LANES), F32),
        compiler_params=_cparams("parallel"),
        name="expert_ffn",
    )(tile_expert, tile_valid, xs, w13, w2)


def _combine_kernel(pos_ref, x_ref, route_ref, g_ref, y_ref, out_ref, buf, sem, *, tm,
                    final_norm):
    i = pl.program_id(0)

    def issue(tile, slot):
        base = tile * tm * TOP_K

        def start(r, c):
            for kk in range(TOP_K):
                _start_row_copy(y_ref, pos_ref[base + r * TOP_K + kk], buf.at[slot], kk * tm + r,
                                sem.at[slot], SLAB_ROWS, kk % DMA_QUEUES)
            return c

        lax.fori_loop(0, tm, start, 0, unroll=DMA_ISSUE_UNROLL)

    @pl.when(i == 0)
    def _():
        issue(0, 0)

    @pl.when(i + 1 < pl.num_programs(0))
    def _():
        issue(i + 1, (i + 1) % 2)

    slot = i % 2
    _wait_rows(y_ref, buf.at[slot], tm * TOP_K, sem.at[slot], SLAB_ROWS)
    route = route_ref[...]
    acc = x_ref[...]
    for kk in range(TOP_K):
        gate = route[:, TOP_K + kk:TOP_K + kk + 1]
        acc = acc + gate * _load_slab(buf.at[slot], kk * tm, tm, SLAB_ROWS)
    out_ref[...] = _rms(acc, g_ref[...]) if final_norm else acc


def _combine(x, route, y, pos, final_g, final_norm, tm):
    T, D = x.shape
    return pl.pallas_call(
        partial(_combine_kernel, tm=tm, final_norm=final_norm),
        grid_spec=pltpu.PrefetchScalarGridSpec(
            num_scalar_prefetch=1,
            grid=(T // tm,),
            in_specs=[pl.BlockSpec((tm, D), lambda i, *_: (i, 0)),
                      pl.BlockSpec((tm, LANES), lambda i, *_: (i, 0)),
                      pl.BlockSpec((1, D), lambda i, *_: (0, 0)),
                      pl.BlockSpec(memory_space=pl.ANY)],
            out_specs=pl.BlockSpec((tm, D), lambda i, *_: (i, 0)),
            scratch_shapes=[pltpu.VMEM((2, tm * TOP_K * SLAB_ROWS, LANES), F32),
                            pltpu.SemaphoreType.DMA((2,))]),
        out_shape=jax.ShapeDtypeStruct((T, D), F32),
        compiler_params=_cparams("arbitrary"),
        name="combine",
    )(pos, x, route, final_g, y)


def _routing_tables(expert_ids, tm, n_tiles):
    e_flat = expert_ids.reshape(-1)
    onehot = (e_flat[:, None] == jnp.arange(N_EXPERTS, dtype=jnp.int32)[None, :]).astype(jnp.int32)
    csum = jnp.cumsum(onehot, axis=0)
    counts = csum[-1]
    rank = jnp.sum(onehot * (csum - 1), axis=1)
    padded = ((counts + tm - 1) // tm) * tm
    ends = jnp.cumsum(padded)
    starts = ends - padded
    pos = (jnp.sum(onehot * starts[None, :], axis=1) + rank).astype(jnp.int32)

    tile_start = jnp.arange(n_tiles, dtype=jnp.int32) * tm
    tile_valid = (tile_start < ends[-1]).astype(jnp.int32)
    te = jnp.sum((tile_start[:, None] >= ends[None, :]).astype(jnp.int32), axis=1)
    last_e = jnp.sum((ends[-1] - 1 >= ends).astype(jnp.int32))
    tile_expert = jnp.where(tile_valid > 0, te, last_e).astype(jnp.int32)

    npad_e = padded - counts
    pcum = jnp.cumsum(npad_e)
    k = jnp.arange(N_EXPERTS * tm, dtype=jnp.int32)
    ek = jnp.sum((k[:, None] >= pcum[None, :]).astype(jnp.int32), axis=1)
    ohk = (ek[:, None] == jnp.arange(N_EXPERTS, dtype=jnp.int32)[None, :]).astype(jnp.int32)
    in_group = jnp.sum(ohk * (starts + counts - (pcum - npad_e))[None, :], axis=1) + k
    pad = jnp.where(ek < N_EXPERTS, in_group, ends[-1] + k - pcum[-1]).astype(jnp.int32)
    return pos, tile_expert, tile_valid, pad


def _moe_ffn(x, hn_slab, route, w13, w2, layer, final_g, final_norm, tm, tme):
    T, D = x.shape
    n_tiles = (T * TOP_K) // tme + N_EXPERTS
    expert_ids = route[:, :TOP_K].astype(jnp.int32)
    pos, tile_expert, tile_valid, pad = _routing_tables(expert_ids, tme, n_tiles)
    xs = _dispatch(hn_slab, pos, pad, n_tiles * tme, tm)
    y = _expert_ffn(xs, tile_expert, tile_valid, w13, w2, layer, tme)
    return _combine(x, route, y, pos, final_g, final_norm, min(COMBINE_ROW_TILE, tm))


def _final_norm_kernel(x_ref, g_ref, o_ref):
    o_ref[...] = _rms(x_ref[...], g_ref[...])


def _final_norm(x, g, tm):
    T, D = x.shape
    return pl.pallas_call(
        _final_norm_kernel,
        grid=(T // tm,),
        in_specs=[pl.BlockSpec((tm, D), lambda i: (i, 0)), pl.BlockSpec((1, D), lambda i: (0, 0))],
        out_specs=pl.BlockSpec((tm, D), lambda i: (i, 0)),
        out_shape=jax.ShapeDtypeStruct((T, D), F32),
        compiler_params=_cparams("parallel"),
        name="final_norm",
    )(x, g)


def _block_diag(pw):
    G, C, _ = pw.shape
    eye = jnp.eye(G, dtype=pw.dtype)
    return (eye[:, None, :, None] * pw[:, :, None, :]).reshape(G * C, G * C)


def kernel(x, mem, rel_bias, mem_norm_g, mix_norm_g, w_in, conv_w, pool_w, pool_scale, w_out,
           xattn_norm_g, wq_x, wkv_x, wo_x, ffn_norm_g, w13_dense, w2_dense, router_w, w13_moe,
           w2_moe, final_norm_g):
    B, S, D = x.shape
    depth = w_in.shape[0]
    M = mem.shape[1]
    T = B * S
    tm = min(ROW_TILE, S)
    assert D == D_MODEL and S % (max(DILATIONS) * Q_BLOCK) == 0 and S % tm == 0

    xf = x.reshape(T, D)
    row2 = lambda v: v.reshape(1, -1)
    q_scale = jnp.where(jnp.arange(w_in.shape[-1]) < ATTN_WIDTH,
                        ATTN_HEAD_DIM ** -0.5 * LOG2_E, 1.0)
    w_in_b = (w_in * q_scale).astype(BF16)
    biases = [_band_bias(rel_bias * LOG2_E, d) for d in DILATIONS]
    wr_pad = jnp.pad(router_w, ((0, 0), (0, 0), (0, LANES - N_EXPERTS))).astype(BF16)
    w_out_b, wkv_b, wo_b = (w.astype(BF16) for w in (w_out, wkv_x, wo_x))
    wq_b = (wq_x * (XATTN_HEAD_DIM ** -0.5 * LOG2_E)).astype(BF16)
    w13_dense_b, w2_dense_b = w13_dense.astype(BF16), w2_dense.astype(BF16)
    w13_moe_b, w2_moe_b = w13_moe.astype(BF16), w2_moe.astype(BF16)

    mem_f = mem.reshape(B * M, D)
    for layer in range(depth):
        *qkvs, rest = _in_proj(xf, row2(mix_norm_g[layer]), w_in_b, layer, B, S, tm)
        branch = []
        for qkv, bias, d in zip(qkvs, biases, DILATIONS):
            o, lse = _window_attn(qkv.reshape(B * d, S // d, qkv.shape[-1]), bias, ATTN_Q_BLOCKS)
            branch.append((o.reshape(B, d, S // d, -1), lse.reshape(B, d, S // d, -1)))
        kv = _norm_proj(mem_f, row2(mem_norm_g), wkv_b, layer, BF16, min(tm, B * M))
        g = row2(ffn_norm_g[layer])
        routed = layer % 2 == 1
        res = _mixer_out(xf, [o for o, _ in branch], [l for _, l in branch], rest,
                         conv_w[layer], _block_diag(pool_w[layer]).astype(BF16),
                         row2(pool_scale[layer]), w_out_b, row2(xattn_norm_g[layer]), wq_b,
                         kv.reshape(B, M, 2 * XATTN_WIDTH), wo_b, layer, g,
                         wr_pad[min(layer // 2, wr_pad.shape[0] - 1)], routed, S,
                         min(MIXER_ROW_TILE, tm) if routed else tm)
        last = layer == depth - 1
        if not routed:
            xf = _dense_ffn(res, g, w13_dense_b, w2_dense_b, layer // 2, min(FFN_ROW_TILE, tm))
            if last:
                xf = _final_norm(xf, row2(final_norm_g), tm)
        else:
            xf, hn_slab, route = res
            xf = _moe_ffn(xf, hn_slab, route, w13_moe_b, w2_moe_b, layer // 2,
                          row2(final_norm_g), last, tm, FFN_ROW_TILE)
    return xf.reshape(B, S, D)
```

```python
from functools import partial

import numpy as np
import jax
import jax.numpy as jnp
from jax import lax
from jax.experimental import pallas as pl
from jax.experimental.pallas import tpu as pltpu

F32 = jnp.float32
BF16 = jnp.bfloat16

D_MODEL = 1024
ATTN_HEADS = 8
ATTN_HEAD_DIM = 64
ATTN_WIDTH = ATTN_HEADS * ATTN_HEAD_DIM
POOL_WIDTH = 256
POOL_GROUP_DIM = 64
POOL_WINDOWS = (2, 4, 8, 16)
POOL_HALO = 16
CONV_WIDTH = 256
REST_WIDTH = POOL_WIDTH + 3 * CONV_WIDTH
DILATIONS = (1, 4, 16)
WINDOW_STEPS = 128
Q_BLOCK = 128
NEG_INF = -1e30
REL_BUCKETS = 32
REL_MAX_DIST = 128
XATTN_HEADS = 4
XATTN_HEAD_DIM = 128
XATTN_WIDTH = XATTN_HEADS * XATTN_HEAD_DIM
N_EXPERTS = 8
TOP_K = 2
EPS = 1e-6

LANES = 128
MXU_COLS = 256
SLAB_ROWS = D_MODEL // LANES
PACKED_ROWS = SLAB_ROWS // 2
ROW_TILE = 1024
FFN_ROW_TILE = 512
COMBINE_ROW_TILE = 512
MIXER_ROW_TILE = 1024
LOG2_E = 1.4426950408889634
LN_2 = 0.6931471805599453
ATTN_Q_BLOCKS = 16
DMA_ISSUE_UNROLL = 8
DMA_QUEUES = 2
VMEM_LIMIT = 60 * 1024 * 1024


def _cparams(*sem):
    return pltpu.CompilerParams(dimension_semantics=sem, vmem_limit_bytes=VMEM_LIMIT)


def _rms(x, g):
    return x * lax.rsqrt(jnp.mean(x * x, axis=-1, keepdims=True) + EPS) * g


def _norm_proj_kernel(x_ref, g_ref, w_ref, o_ref):
    xn = _rms(x_ref[...], g_ref[...]).astype(BF16)
    o_ref[...] = jnp.dot(xn, w_ref[...], preferred_element_type=F32).astype(o_ref.dtype)


def _layer_spec(w, layer):
    return pl.BlockSpec((None,) + w.shape[1:], lambda *_: (layer, 0, 0),
                        pipeline_mode=pl.Buffered(1))


def _norm_proj(x, g, w, layer, out_dtype, tm):
    T, D = x.shape
    N = w.shape[-1]
    return pl.pallas_call(
        _norm_proj_kernel,
        grid=(T // tm,),
        in_specs=[pl.BlockSpec((tm, D), lambda i: (i, 0)),
                  pl.BlockSpec((1, D), lambda i: (0, 0)),
                  _layer_spec(w, layer)],
        out_specs=pl.BlockSpec((tm, N), lambda i: (i, 0)),
        out_shape=jax.ShapeDtypeStruct((T, N), out_dtype),
        compiler_params=_cparams("parallel"),
        name="norm_proj",
    )(x, g, w)


def _in_proj_kernel(x_ref, g_ref, w_ref, *refs):
    nd = len(DILATIONS)
    qkv_refs, rest_ref, acc_refs = refs[:nd], refs[nd], refs[nd + 1:]
    per, tm, _ = acc_refs[0].shape
    nq = len(acc_refs) * MXU_COLS
    xn = _rms(x_ref[...], g_ref[...]).astype(BF16)
    for nb, acc_ref in enumerate(acc_refs):
        cols = slice(nb * MXU_COLS, (nb + 1) * MXU_COLS)
        acc = jnp.dot(xn, w_ref[:, cols], preferred_element_type=F32)
        for c in range(per):
            acc_ref[c] = acc[:, c * LANES:(c + 1) * LANES]
        for d, qkv_ref in zip(DILATIONS, qkv_refs):
            for r in range(d):
                rows = [acc_ref[c, pl.ds(r, tm // d, stride=d), :] for c in range(per)]
                qkv_ref[0, r, :, cols] = jnp.concatenate(rows, axis=1).astype(BF16)
    rest_ref[...] = jnp.dot(xn, w_ref[:, nq:], preferred_element_type=F32)


def _in_proj(x, g, w, layer, B, S, tm):
    T, D = x.shape
    N = w.shape[-1]
    nq = 3 * ATTN_WIDTH
    tps = S // tm
    res_spec = lambda d: pl.BlockSpec((1, d, tm // d, nq), lambda i: (i // tps, 0, i % tps, 0))
    return pl.pallas_call(
        _in_proj_kernel,
        grid=(T // tm,),
        in_specs=[pl.BlockSpec((tm, D), lambda i: (i, 0)),
                  pl.BlockSpec((1, D), lambda i: (0, 0)),
                  _layer_spec(w, layer)],
        out_specs=[res_spec(d) for d in DILATIONS]
                  + [pl.BlockSpec((tm, N - nq), lambda i: (i, 0))],
        out_shape=[jax.ShapeDtypeStruct((B, d, S // d, nq), BF16) for d in DILATIONS]
                  + [jax.ShapeDtypeStruct((T, N - nq), F32)],
        scratch_shapes=[pltpu.VMEM((MXU_COLS // LANES, tm, LANES), F32)] * (nq // MXU_COLS),
        compiler_params=_cparams("parallel"),
        name="in_proj",
    )(x, g, w)


def _t5_causal_bucket(dist):
    max_exact = REL_BUCKETS // 2
    d = np.maximum(dist, 1).astype(np.float32)
    large = max_exact + (np.log(d / max_exact) / np.log(REL_MAX_DIST / max_exact)
                         * (REL_BUCKETS - max_exact)).astype(np.int32)
    large = np.minimum(large, REL_BUCKETS - 1)
    return np.where(dist < max_exact, dist, large).astype(np.int32)


def _band_bias(rel_bias, dil):
    row = np.arange(Q_BLOCK)[:, None]
    col = np.arange(2 * Q_BLOCK)[None, :]
    steps = Q_BLOCK + row - col
    valid = (steps >= 0) & (steps <= WINDOW_STEPS)
    bucket = _t5_causal_bucket(dil * np.clip(steps, 0, WINDOW_STEPS))
    onehot = (bucket[..., None] == np.arange(REL_BUCKETS)).astype(np.float32)
    b = jnp.einsum('qcb,bh->hqc', jnp.asarray(onehot), rel_bias.astype(F32),
                   precision=lax.Precision.HIGHEST)
    masks = np.stack([valid, valid & (col >= Q_BLOCK)])
    return jnp.where(jnp.asarray(masks)[:, None], b[None], NEG_INF)


def _window_attn_kernel(q_ref, kp_ref, k_ref, vp_ref, v_ref, bias_ref, o_ref, lse_ref, *,
                        nseq, nq):
    first_step = jnp.where(pl.program_id(1) == 0, 1, 0)
    lane = lax.broadcasted_iota(jnp.int32, (Q_BLOCK, LANES), 1)
    low_half = lane < ATTN_HEAD_DIM
    Q = Q_BLOCK
    for n, j in [(n, j) for n in range(nseq) for j in range(nq)]:
        rows = slice(j * Q, (j + 1) * Q)
        if j == 0:
            k2 = jnp.concatenate([kp_ref[n], k_ref[n, :Q, :]], axis=0)
            v2 = jnp.concatenate([vp_ref[n], v_ref[n, :Q, :]], axis=0)
            table = first_step
        else:
            k2 = k_ref[n, (j - 1) * Q:(j + 1) * Q, :]
            v2 = v_ref[n, (j - 1) * Q:(j + 1) * Q, :]
            table = 0
        m_all = jnp.zeros((Q, LANES), F32)
        l_all = jnp.ones((Q, LANES), F32)
        for g in range(ATTN_WIDTH // LANES):
            cols = slice(g * LANES, (g + 1) * LANES)
            qg, kg, vg = q_ref[n, rows, cols], k2[:, cols], v2[:, cols]
            og = lg = None
            for half in range(2):
                h = 2 * g + half
                keep = low_half if half == 0 else jnp.logical_not(low_half)
                qh = jnp.where(keep, qg, jnp.zeros_like(qg))
                s = lax.dot_general(qh, kg, (((1,), (1,)), ((), ())),
                                    preferred_element_type=F32)
                s = s + bias_ref[table, h]
                m = jnp.max(s, axis=-1, keepdims=True)
                p = jnp.exp2(s - m)
                l = jnp.sum(p, axis=-1, keepdims=True)
                o = jnp.dot(p.astype(BF16), vg, preferred_element_type=F32)
                og = o if half == 0 else jnp.where(low_half, og, o)
                lg = l if half == 0 else jnp.where(low_half, lg, l)
                m_all = jnp.where(lane == h, m, m_all)
                l_all = jnp.where(lane == h, l, l_all)
            o_ref[n, rows, cols] = (og * (1.0 / lg)).astype(o_ref.dtype)
        lse_ref[n, rows, :] = (m_all + jnp.log2(l_all)) * LN_2


def _window_attn(qkv, bias, blocks_per_step):
    N, L, _ = qkv.shape
    W = ATTN_WIDTH
    nq = min(blocks_per_step, L // Q_BLOCK)
    nseq = min(max(blocks_per_step // nq, 1), N)
    assert N % nseq == 0
    step = nq * Q_BLOCK
    prev = lambda i: jnp.maximum(i * nq - 1, 0)
    return pl.pallas_call(
        partial(_window_attn_kernel, nseq=nseq, nq=nq),
        grid=(N // nseq, L // step),
        in_specs=[pl.BlockSpec((nseq, step, W), lambda n, i: (n, i, 0)),
                  pl.BlockSpec((nseq, Q_BLOCK, W), lambda n, i: (n, prev(i), 1)),
                  pl.BlockSpec((nseq, step, W), lambda n, i: (n, i, 1)),
                  pl.BlockSpec((nseq, Q_BLOCK, W), lambda n, i: (n, prev(i), 2)),
                  pl.BlockSpec((nseq, step, W), lambda n, i: (n, i, 2)),
                  pl.BlockSpec(bias.shape, lambda n, i: (0, 0, 0, 0))],
        out_specs=[pl.BlockSpec((nseq, step, W), lambda n, i: (n, i, 0)),
                   pl.BlockSpec((nseq, step, LANES), lambda n, i: (n, i, 0))],
        out_shape=[jax.ShapeDtypeStruct((N, L, W), BF16),
                   jax.ShapeDtypeStruct((N, L, LANES), F32)],
        compiler_params=_cparams("parallel", "parallel"),
        name="window_attn",
    )(qkv, qkv, qkv, qkv, qkv, bias)


def _token_order(src_ref, scratch_ref, d):
    if d == 1:
        return src_ref[0, 0].astype(F32)
    nchunk, n, _ = scratch_ref.shape
    for r in range(d):
        part = src_ref[0, r].astype(F32)
        for c in range(nchunk):
            scratch_ref[c, pl.ds(r, n // d, stride=d), :] = part[:, c * LANES:(c + 1) * LANES]
    return jnp.concatenate([scratch_ref[c] for c in range(nchunk)], axis=1)


def _memory_xattn(x, g_ref, wq_ref, kv_ref, wo_ref):
    xn = _rms(x, g_ref[...]).astype(BF16)
    q = jnp.dot(xn, wq_ref[...], preferred_element_type=F32).astype(BF16)
    heads = []
    for h in range(XATTN_HEADS):
        sl = slice(h * XATTN_HEAD_DIM, (h + 1) * XATTN_HEAD_DIM)
        k = kv_ref[0, :, sl]
        v = kv_ref[0, :, XATTN_WIDTH + h * XATTN_HEAD_DIM:XATTN_WIDTH + (h + 1) * XATTN_HEAD_DIM]
        s = lax.dot_general(q[:, sl], k, (((1,), (1,)), ((), ())), preferred_element_type=F32)
        m = jnp.max(s, axis=-1, keepdims=True)
        p = jnp.exp2(s - m)
        l = jnp.sum(p, axis=-1, keepdims=True)
        heads.append(jnp.dot(p.astype(BF16), v, preferred_element_type=F32) * (1.0 / l))
    o = jnp.concatenate(heads, axis=1).astype(BF16)
    return x + jnp.dot(o, wo_ref[...], preferred_element_type=F32)


def _mixer_out_kernel(x_ref, o1_ref, o2_ref, o3_ref, l1_ref, l2_ref, l3_ref, rest_ref, halo_ref,
                      convw_ref, poolw_ref, pscale_ref, pwin_ref, expand_ref, wout_ref,
                      xg_ref, wq_ref, kv_ref, wo_ref, fg_ref, wr_ref, out_ref, *refs,
                      tm, tiles_per_seq, routed):
    os2_ref, os3_ref, ls2_ref, ls3_ref = refs[-4:]
    it = pl.program_id(0) % tiles_per_seq
    lses = [_token_order(r, s, d)
            for r, s, d in zip((l1_ref, l2_ref, l3_ref), (None, ls2_ref, ls3_ref), DILATIONS)]
    outs = [_token_order(r, s, d)
            for r, s, d in zip((o1_ref, o2_ref, o3_ref), (None, os2_ref, os3_ref), DILATIONS)]
    lm = jnp.maximum(jnp.maximum(lses[0], lses[1]), lses[2])
    es = [jnp.exp(l - lm) for l in lses]
    inv = 1.0 / (es[0] + es[1] + es[2])
    y_attn = jnp.zeros((tm, ATTN_WIDTH), F32)
    for e, o in zip(es, outs):
        w = e * inv
        w_hi = w.astype(BF16)
        w_lo = (w - w_hi.astype(F32)).astype(BF16)
        wide = jnp.dot(jnp.concatenate([w_hi, w_lo], axis=1), expand_ref[...],
                       preferred_element_type=F32)
        y_attn = y_attn + wide * o
    y_attn = y_attn.astype(BF16)

    rest = rest_ref[...]
    halo = jnp.where(it == 0, 0.0, halo_ref[...])
    ext = jnp.concatenate([halo, rest], axis=0)

    u = ext[:, :POOL_WIDTH]
    s2 = u + pltpu.roll(u, 1, 0)
    s4 = s2 + pltpu.roll(s2, 2, 0)
    s8 = s4 + pltpu.roll(s4, 4, 0)
    s16 = s8 + pltpu.roll(s8, 8, 0)
    col = lax.broadcasted_iota(jnp.int32, u.shape, 1)
    g = POOL_GROUP_DIM
    wsum = jnp.where(col < g, s2, jnp.where(col < 2 * g, s4, jnp.where(col < 3 * g, s8, s16)))
    pos = it * tm + lax.broadcasted_iota(jnp.int32, (tm, POOL_WIDTH), 0)
    count = jnp.minimum((pos + 1).astype(F32), pwin_ref[...])
    pooled = wsum[POOL_HALO:] / count - u[POOL_HALO:]
    y_pool = jnp.dot(pooled.astype(BF16), poolw_ref[...], preferred_element_type=F32)
    y_pool = (y_pool * pscale_ref[...]).astype(BF16)

    b_gate = rest[:, POOL_WIDTH:POOL_WIDTH + CONV_WIDTH]
    cu = ext[:, POOL_WIDTH + CONV_WIDTH:POOL_WIDTH + 2 * CONV_WIDTH] * ext[:, POOL_WIDTH + 2 * CONV_WIDTH:]
    y = (convw_ref[0:1, :] * cu + convw_ref[1:2, :] * pltpu.roll(cu, 1, 0)
         + convw_ref[2:3, :] * pltpu.roll(cu, 2, 0))
    y_conv = (b_gate * y[POOL_HALO:]).astype(BF16)

    a0, a1 = ATTN_WIDTH, ATTN_WIDTH + POOL_WIDTH
    acc = jnp.dot(y_attn, wout_ref[:a0, :], preferred_element_type=F32)
    acc += jnp.dot(y_pool, wout_ref[a0:a1, :], preferred_element_type=F32)
    acc += jnp.dot(y_conv, wout_ref[a1:, :], preferred_element_type=F32)
    x_new = _memory_xattn(x_ref[...] + acc, xg_ref, wq_ref, kv_ref, wo_ref)
    out_ref[...] = x_new
    if routed:
        _route_tokens(x_new, fg_ref, wr_ref, *refs[:2])


def _mixer_out(x, outs, lses, rest, conv_w, pool_w_bd, pool_scale, w_out, xg, wq, kv, wo, layer,
               ffn_g, wr, routed, S, tm):
    T, D = x.shape
    M = kv.shape[1]
    hb = tm // POOL_HALO
    tps = S // tm
    row = lambda i: (i, 0)
    const = lambda i: (0, 0)
    res_spec = lambda d, c: pl.BlockSpec((1, d, tm // d, c), lambda i: (i // tps, 0, i % tps, 0))
    pwin = jnp.asarray(np.repeat(np.array(POOL_WINDOWS, np.float32), POOL_GROUP_DIM)[None, :])
    expand = np.zeros((2, LANES, ATTN_WIDTH), np.float32)
    for h in range(ATTN_HEADS):
        expand[:, h, h * ATTN_HEAD_DIM:(h + 1) * ATTN_HEAD_DIM] = 1.0
    expand = jnp.asarray(expand.reshape(2 * LANES, ATTN_WIDTH), dtype=BF16)
    out_specs = [pl.BlockSpec((tm, D), row)]
    out_shape = [jax.ShapeDtypeStruct((T, D), F32)]
    if routed:
        out_specs += [pl.BlockSpec((tm * PACKED_ROWS, LANES), row), pl.BlockSpec((tm, LANES), row)]
        out_shape += [jax.ShapeDtypeStruct((T * PACKED_ROWS, LANES), jnp.uint32),
                      jax.ShapeDtypeStruct((T, LANES), F32)]
    res = pl.pallas_call(
        partial(_mixer_out_kernel, tm=tm, tiles_per_seq=tps, routed=routed),
        grid=(T // tm,),
        scratch_shapes=[pltpu.VMEM((ATTN_WIDTH // LANES, tm, LANES), F32)] * 2
                       + [pltpu.VMEM((1, tm, LANES), F32)] * 2,
        name="mixer_out",
        in_specs=[pl.BlockSpec((tm, D), row)]
                 + [res_spec(d, ATTN_WIDTH) for d in DILATIONS]
                 + [res_spec(d, LANES) for d in DILATIONS]
                 + [pl.BlockSpec((tm, REST_WIDTH), row),
                    pl.BlockSpec((POOL_HALO, REST_WIDTH), lambda i: (jnp.maximum(i * hb - 1, 0), 0)),
                    pl.BlockSpec(conv_w.shape, const),
                    pl.BlockSpec(pool_w_bd.shape, const),
                    pl.BlockSpec((1, POOL_WIDTH), const),
                    pl.BlockSpec((1, POOL_WIDTH), const),
                    pl.BlockSpec(expand.shape, const),
                    _layer_spec(w_out, layer),
                    pl.BlockSpec((1, D), const),
                    _layer_spec(wq, layer),
                    pl.BlockSpec((1, M, 2 * XATTN_WIDTH), lambda i: (i // tps, 0, 0)),
                    _layer_spec(wo, layer),
                    pl.BlockSpec((1, D), const),
                    pl.BlockSpec(wr.shape, const)],
        out_specs=out_specs,
        out_shape=out_shape,
        compiler_params=_cparams("parallel"),
    )(x, *outs, *lses, rest, rest, conv_w, pool_w_bd, pool_scale, pwin, expand, w_out,
      xg, wq, kv, wo, ffn_g, wr)
    return res if routed else res[0]


def _swiglu(hn, w13_ref, w2_ref):
    dff = w2_ref.shape[0]
    a = jnp.dot(hn, w13_ref[:, :dff], preferred_element_type=F32)
    b = jnp.dot(hn, w13_ref[:, dff:], preferred_element_type=F32)
    h = (a * jax.nn.sigmoid(a) * b).astype(BF16)
    return jnp.dot(h, w2_ref[...], preferred_element_type=F32)


def _dense_ffn_kernel(x_ref, g_ref, w13_ref, w2_ref, out_ref):
    x = x_ref[...]
    out_ref[...] = x + _swiglu(_rms(x, g_ref[...]).astype(BF16), w13_ref, w2_ref)


def _dense_ffn(x, g, w13, w2, layer, tm):
    T, D = x.shape
    return pl.pallas_call(
        _dense_ffn_kernel,
        grid=(T // tm,),
        in_specs=[pl.BlockSpec((tm, D), lambda i: (i, 0)),
                  pl.BlockSpec((1, D), lambda i: (0, 0)),
                  _layer_spec(w13, layer),
                  _layer_spec(w2, layer)],
        out_specs=pl.BlockSpec((tm, D), lambda i: (i, 0)),
        out_shape=jax.ShapeDtypeStruct((T, D), F32),
        compiler_params=_cparams("parallel"),
        name="dense_ffn",
    )(x, g, w13, w2)


def _store_slab(ref, v):
    n, rows = v.shape[0], v.shape[1] // LANES
    for s in range(rows):
        ref[pl.ds(s, n, stride=rows), :] = v[:, s * LANES:(s + 1) * LANES]


def _load_slab(ref, first, n, rows):
    return jnp.concatenate(
        [ref[pl.ds(first * rows + s, n, stride=rows), :] for s in range(rows)], axis=1)


def _slab(row, rows):
    return pl.ds(pl.multiple_of(row * rows, rows), rows)


def _pack_bf16_pairs(v):
    half = v.shape[1] // 2
    bits = pltpu.bitcast(v.astype(BF16).astype(F32), jnp.uint32)
    return (bits[:, half:] & jnp.uint32(0xFFFF0000)) | (bits[:, :half] >> 16)


def _unpack_bf16_pairs(u):
    low = pltpu.bitcast(u << 16, F32)
    high = pltpu.bitcast(u & jnp.uint32(0xFFFF0000), F32)
    return jnp.concatenate([low, high], axis=1).astype(BF16)


def _route_tokens(x, g_ref, wr_ref, hn_ref, route_ref):
    hn = _rms(x, g_ref[...])
    _store_slab(hn_ref, _pack_bf16_pairs(hn))
    logits = jnp.dot(hn.astype(BF16), wr_ref[...], preferred_element_type=F32)
    lane = lax.broadcasted_iota(jnp.int32, logits.shape, 1).astype(F32)
    logits = jnp.where(lane < N_EXPERTS, logits, -jnp.inf)
    m1 = jnp.max(logits, axis=-1, keepdims=True)
    i1 = jnp.min(jnp.where(logits == m1, lane, float(LANES)), axis=-1, keepdims=True)
    rest = jnp.where(lane == i1, -jnp.inf, logits)
    m2 = jnp.max(rest, axis=-1, keepdims=True)
    i2 = jnp.min(jnp.where(rest == m2, lane, float(LANES)), axis=-1, keepdims=True)
    e2 = jnp.exp(m2 - m1)
    p1 = 1.0 / (1.0 + e2)
    p2 = e2 * p1
    route_ref[...] = jnp.where(lane == 0, i1, jnp.where(lane == 1, i2,
                     jnp.where(lane == 2, p1, jnp.where(lane == 3, p2, 0.0))))


def _start_row_copy(src_ref, src_row, dst_ref, dst_row, sem, rows, queue):
    pltpu.async_copy(src_ref.at[_slab(src_row, rows)], dst_ref.at[_slab(dst_row, rows)], sem,
                     priority=queue)


def _wait_rows(src_ref, dst_ref, n, sem, rows):
    span = pl.ds(0, n * rows)
    pltpu.make_async_copy(src_ref.at[span], dst_ref.at[span], sem).wait()


def _dispatch_kernel(pos_ref, pad_ref, hn_ref, xs_ref, sem, *, tm):
    i = pl.program_id(0)
    base = i * tm * TOP_K

    def start(r, c):
        for kk in range(TOP_K):
            _start_row_copy(hn_ref, r, xs_ref, pos_ref[base + r * TOP_K + kk], sem, PACKED_ROWS,
                            kk % DMA_QUEUES)
        return c

    lax.fori_loop(0, tm, start, 0, unroll=DMA_ISSUE_UNROLL)
    _wait_rows(xs_ref, xs_ref, tm * TOP_K, sem, PACKED_ROWS)

    @pl.when(i == 0)
    def _():
        npad = pad_ref.shape[0]

        def pstart(r, c):
            for q in range(DMA_QUEUES):
                _start_row_copy(hn_ref, 0, xs_ref, pad_ref[r * DMA_QUEUES + q], sem, PACKED_ROWS, q)
            return c

        lax.fori_loop(0, npad // DMA_QUEUES, pstart, 0, unroll=DMA_ISSUE_UNROLL)
        _wait_rows(xs_ref, xs_ref, npad, sem, PACKED_ROWS)


def _dispatch(hn_slab, pos, pad_rows, n_rows, tm):
    T = hn_slab.shape[0] // PACKED_ROWS
    return pl.pallas_call(
        partial(_dispatch_kernel, tm=tm),
        grid_spec=pltpu.PrefetchScalarGridSpec(
            num_scalar_prefetch=2,
            grid=(T // tm,),
            in_specs=[pl.BlockSpec((tm * PACKED_ROWS, LANES), lambda i, *_: (i, 0))],
            out_specs=pl.BlockSpec(memory_space=pl.ANY),
            scratch_shapes=[pltpu.SemaphoreType.DMA(())]),
        out_shape=jax.ShapeDtypeStruct((n_rows * PACKED_ROWS, LANES), jnp.uint32),
        compiler_params=_cparams("arbitrary"),
        name="dispatch",
    )(pos, pad_rows, hn_slab)


def _expert_ffn_kernel(te_ref, tv_ref, xs_ref, w13_ref, w2_ref, y_ref, *, tm):
    valid = tv_ref[pl.program_id(0)] > 0

    @pl.when(valid)
    def _():
        hn = _unpack_bf16_pairs(_load_slab(xs_ref, 0, tm, PACKED_ROWS))
        _store_slab(y_ref, _swiglu(hn, w13_ref, w2_ref))

    @pl.when(jnp.logical_not(valid))
    def _():
        y_ref[...] = jnp.zeros_like(y_ref)


def _expert_ffn(xs, tile_expert, tile_valid, w13, w2, layer, tm):
    n_rows = xs.shape[0] // PACKED_ROWS
    expert_spec = lambda w, buffers: pl.BlockSpec((None, None) + w.shape[2:],
                                                  lambda i, te, tv: (layer, te[i], 0, 0),
                                                  pipeline_mode=pl.Buffered(buffers))
    return pl.pallas_call(
        partial(_expert_ffn_kernel, tm=tm),
        grid_spec=pltpu.PrefetchScalarGridSpec(
            num_scalar_prefetch=2,
            grid=(n_rows // tm,),
            in_specs=[pl.BlockSpec((tm * PACKED_ROWS, LANES), lambda i, te, tv: (i, 0)),
                      expert_spec(w13, 1), expert_spec(w2, 2)],
            out_specs=pl.BlockSpec((tm * SLAB_ROWS, LANES), lambda i, te, tv: (i, 0))),
        out_shape=jax.ShapeDtypeStruct((n_rows * SLAB_ROWS, LANES), F32),
        compiler_params=_cparams("parallel"),
        name="expert_ffn",
    )(tile_expert, tile_valid, xs, w13, w2)


def _combine_kernel(pos_ref, x_ref, route_ref, g_ref, y_ref, out_ref, buf, sem, *, tm,
                    final_norm):
    i = pl.program_id(0)

    def issue(tile, slot):
        base = tile * tm * TOP_K

        def start(r, c):
            for kk in range(TOP_K):
                _start_row_copy(y_ref, pos_ref[base + r * TOP_K + kk], buf.at[slot], kk * tm + r,
                                sem.at[slot], SLAB_ROWS, kk % DMA_QUEUES)
            return c

        lax.fori_loop(0, tm, start, 0, unroll=DMA_ISSUE_UNROLL)

    @pl.when(i == 0)
    def _():
        issue(0, 0)

    @pl.when(i + 1 < pl.num_programs(0))
    def _():
        issue(i + 1, (i + 1) % 2)

    slot = i % 2
    _wait_rows(y_ref, buf.at[slot], tm * TOP_K, sem.at[slot], SLAB_ROWS)
    route = route_ref[...]
    acc = x_ref[...]
    for kk in range(TOP_K):
        gate = route[:, TOP_K + kk:TOP_K + kk + 1]
        acc = acc + gate * _load_slab(buf.at[slot], kk * tm, tm, SLAB_ROWS)
    out_ref[...] = _rms(acc, g_ref[...]) if final_norm else acc


def _combine(x, route, y, pos, final_g, final_norm, tm):
    T, D = x.shape
    return pl.pallas_call(
        partial(_combine_kernel, tm=tm, final_norm=final_norm),
        grid_spec=pltpu.PrefetchScalarGridSpec(
            num_scalar_prefetch=1,
            grid=(T // tm,),
            in_specs=[pl.BlockSpec((tm, D), lambda i, *_: (i, 0)),
                      pl.BlockSpec((tm, LANES), lambda i, *_: (i, 0)),
                      pl.BlockSpec((1, D), lambda i, *_: (0, 0)),
                      pl.BlockSpec(memory_space=pl.ANY)],
            out_specs=pl.BlockSpec((tm, D), lambda i, *_: (i, 0)),
            scratch_shapes=[pltpu.VMEM((2, tm * TOP_K * SLAB_ROWS, LANES), F32),
                            pltpu.SemaphoreType.DMA((2,))]),
        out_shape=jax.ShapeDtypeStruct((T, D), F32),
        compiler_params=_cparams("arbitrary"),
        name="combine",
    )(pos, x, route, final_g, y)


def _routing_tables(expert_ids, tm, n_tiles):
    e_flat = expert_ids.reshape(-1)
    onehot = (e_flat[:, None] == jnp.arange(N_EXPERTS, dtype=jnp.int32)[None, :]).astype(jnp.int32)
    csum = jnp.cumsum(onehot, axis=0)
    counts = csum[-1]
    rank = jnp.sum(onehot * (csum - 1), axis=1)
    padded = ((counts + tm - 1) // tm) * tm
    ends = jnp.cumsum(padded)
    starts = ends - padded
    pos = (jnp.sum(onehot * starts[None, :], axis=1) + rank).astype(jnp.int32)

    tile_start = jnp.arange(n_tiles, dtype=jnp.int32) * tm
    tile_valid = (tile_start < ends[-1]).astype(jnp.int32)
    te = jnp.sum((tile_start[:, None] >= ends[None, :]).astype(jnp.int32), axis=1)
    last_e = jnp.sum((ends[-1] - 1 >= ends).astype(jnp.int32))
    tile_expert = jnp.where(tile_valid > 0, te, last_e).astype(jnp.int32)

    npad_e = padded - counts
    pcum = jnp.cumsum(npad_e)
    k = jnp.arange(N_EXPERTS * tm, dtype=jnp.int32)
    ek = jnp.sum((k[:, None] >= pcum[None, :]).astype(jnp.int32), axis=1)
    ohk = (ek[:, None] == jnp.arange(N_EXPERTS, dtype=jnp.int32)[None, :]).astype(jnp.int32)
    in_group = jnp.sum(ohk * (starts + counts - (pcum - npad_e))[None, :], axis=1) + k
    pad = jnp.where(ek < N_EXPERTS, in_group, ends[-1] + k - pcum[-1]).astype(jnp.int32)
    return pos, tile_expert, tile_valid, pad


def _moe_ffn(x, hn_slab, route, w13, w2, layer, final_g, final_norm, tm, tme):
    T, D = x.shape
    n_tiles = (T * TOP_K) // tme + N_EXPERTS
    expert_ids = route[:, :TOP_K].astype(jnp.int32)
    pos, tile_expert, tile_valid, pad = _routing_tables(expert_ids, tme, n_tiles)
    xs = _dispatch(hn_slab, pos, pad, n_tiles * tme, tm)
    y = _expert_ffn(xs, tile_expert, tile_valid, w13, w2, layer, tme)
    return _combine(x, route, y, pos, final_g, final_norm, min(COMBINE_ROW_TILE, tm))


def _final_norm_kernel(x_ref, g_ref, o_ref):
    o_ref[...] = _rms(x_ref[...], g_ref[...])


def _final_norm(x, g, tm):
    T, D = x.shape
    return pl.pallas_call(
        _final_norm_kernel,
        grid=(T // tm,),
        in_specs=[pl.BlockSpec((tm, D), lambda i: (i, 0)), pl.BlockSpec((1, D), lambda i: (0, 0))],
        out_specs=pl.BlockSpec((tm, D), lambda i: (i, 0)),
        out_shape=jax.ShapeDtypeStruct((T, D), F32),
        compiler_params=_cparams("parallel"),
        name="final_norm",
    )(x, g)


def _block_diag(pw):
    G, C, _ = pw.shape
    eye = jnp.eye(G, dtype=pw.dtype)
    return (eye[:, None, :, None] * pw[:, :, None, :]).reshape(G * C, G * C)


def kernel(x, mem, rel_bias, mem_norm_g, mix_norm_g, w_in, conv_w, pool_w, pool_scale, w_out,
           xattn_norm_g, wq_x, wkv_x, wo_x, ffn_norm_g, w13_dense, w2_dense, router_w, w13_moe,
           w2_moe, final_norm_g):
    B, S, D = x.shape
    depth = w_in.shape[0]
    M = mem.shape[1]
    T = B * S
    tm = min(ROW_TILE, S)
    assert D == D_MODEL and S % (max(DILATIONS) * Q_BLOCK) == 0 and S % tm == 0

    xf = x.reshape(T, D)
    row2 = lambda v: v.reshape(1, -1)
    q_scale = jnp.where(jnp.arange(w_in.shape[-1]) < ATTN_WIDTH,
                        ATTN_HEAD_DIM ** -0.5 * LOG2_E, 1.0)
    w_in_b = (w_in * q_scale).astype(BF16)
    biases = [_band_bias(rel_bias * LOG2_E, d) for d in DILATIONS]
    wr_pad = jnp.pad(router_w, ((0, 0), (0, 0), (0, LANES - N_EXPERTS))).astype(BF16)
    w_out_b, wkv_b, wo_b = (w.astype(BF16) for w in (w_out, wkv_x, wo_x))
    wq_b = (wq_x * (XATTN_HEAD_DIM ** -0.5 * LOG2_E)).astype(BF16)
    w13_dense_b, w2_dense_b = w13_dense.astype(BF16), w2_dense.astype(BF16)
    w13_moe_b, w2_moe_b = w13_moe.astype(BF16), w2_moe.astype(BF16)

    mem_f = mem.reshape(B * M, D)
    for layer in range(depth):
        *qkvs, rest = _in_proj(xf, row2(mix_norm_g[layer]), w_in_b, layer, B, S, tm)
        branch = []
        for qkv, bias, d in zip(qkvs, biases, DILATIONS):
            o, lse = _window_attn(qkv.reshape(B * d, S // d, qkv.shape[-1]), bias, ATTN_Q_BLOCKS)
            branch.append((o.reshape(B, d, S // d, -1), lse.reshape(B, d, S // d, -1)))
        kv = _norm_proj(mem_f, row2(mem_norm_g), wkv_b, layer, BF16, min(tm, B * M))
        g = row2(ffn_norm_g[layer])
        routed = layer % 2 == 1
        res = _mixer_out(xf, [o for o, _ in branch], [l for _, l in branch], rest,
                         conv_w[layer], _block_diag(pool_w[layer]).astype(BF16),
                         row2(pool_scale[layer]), w_out_b, row2(xattn_norm_g[layer]), wq_b,
                         kv.reshape(B, M, 2 * XATTN_WIDTH), wo_b, layer, g,
                         wr_pad[min(layer // 2, wr_pad.shape[0] - 1)], routed, S,
                         min(MIXER_ROW_TILE, tm) if routed else tm)
        last = layer == depth - 1
        if not routed:
            xf = _dense_ffn(res, g, w13_dense_b, w2_dense_b, layer // 2, min(FFN_ROW_TILE, tm))
            if last:
                xf = _final_norm(xf, row2(final_norm_g), tm)
        else:
            xf, hn_slab, route = res
            xf = _moe_ffn(xf, hn_slab, route, w13_moe_b, w2_moe_b, layer // 2,
                          row2(final_norm_g), last, tm, FFN_ROW_TILE)
    return xf.reshape(B, S, D)
```

```python
from functools import partial

import numpy as np
import jax
import jax.numpy as jnp
from jax import lax
from jax.experimental import pallas as pl
from jax.experimental.pallas import tpu as pltpu

F32 = jnp.float32
BF16 = jnp.bfloat16

D_MODEL = 1024
ATTN_HEADS = 8
ATTN_HEAD_DIM = 64
ATTN_WIDTH = ATTN_HEADS * ATTN_HEAD_DIM
POOL_WIDTH = 256
POOL_GROUP_DIM = 64
POOL_WINDOWS = (2, 4, 8, 16)
POOL_HALO = 16
CONV_WIDTH = 256
REST_WIDTH = POOL_WIDTH + 3 * CONV_WIDTH
DILATIONS = (1, 4, 16)
WINDOW_STEPS = 128
Q_BLOCK = 128
NEG_INF = -1e30
REL_BUCKETS = 32
REL_MAX_DIST = 128
XATTN_HEADS = 4
XATTN_HEAD_DIM = 128
XATTN_WIDTH = XATTN_HEADS * XATTN_HEAD_DIM
N_EXPERTS = 8
TOP_K = 2
EPS = 1e-6

LANES = 128
MXU_COLS = 256
SLAB_ROWS = D_MODEL // LANES
PACKED_ROWS = SLAB_ROWS // 2
ROW_TILE = 1024
FFN_ROW_TILE = 512
COMBINE_ROW_TILE = 512
MIXER_ROW_TILE = 1024
LOG2_E = 1.4426950408889634
LN_2 = 0.6931471805599453
ATTN_Q_BLOCKS = 16
DMA_ISSUE_UNROLL = 8
DMA_QUEUES = 2
VMEM_LIMIT = 60 * 1024 * 1024


def _cparams(*sem):
    return pltpu.CompilerParams(dimension_semantics=sem, vmem_limit_bytes=VMEM_LIMIT)


def _rms(x, g):
    return x * lax.rsqrt(jnp.mean(x * x, axis=-1, keepdims=True) + EPS) * g


def _norm_proj_kernel(x_ref, g_ref, w_ref, o_ref):
    xn = _rms(x_ref[...], g_ref[...]).astype(BF16)
    o_ref[...] = jnp.dot(xn, w_ref[...], preferred_element_type=F32).astype(o_ref.dtype)


def _layer_spec(w, layer):
    return pl.BlockSpec((None,) + w.shape[1:], lambda *_: (layer, 0, 0),
                        pipeline_mode=pl.Buffered(1))


def _norm_proj(x, g, w, layer, out_dtype, tm):
    T, D = x.shape
    N = w.shape[-1]
    return pl.pallas_call(
        _norm_proj_kernel,
        grid=(T // tm,),
        in_specs=[pl.BlockSpec((tm, D), lambda i: (i, 0)),
                  pl.BlockSpec((1, D), lambda i: (0, 0)),
                  _layer_spec(w, layer)],
        out_specs=pl.BlockSpec((tm, N), lambda i: (i, 0)),
        out_shape=jax.ShapeDtypeStruct((T, N), out_dtype),
        compiler_params=_cparams("parallel"),
        name="norm_proj",
    )(x, g, w)


def _in_proj_kernel(x_ref, g_ref, w_ref, *refs):
    nd = len(DILATIONS)
    qkv_refs, rest_ref, acc_refs, mid_ref = refs[:nd], refs[nd], refs[nd + 1:-1], refs[-1]
    per, tm, _ = acc_refs[0].shape
    nq = len(acc_refs) * MXU_COLS
    d_mid, d_far = DILATIONS[1], DILATIONS[2]
    step = d_far // d_mid
    xn = _rms(x_ref[...], g_ref[...]).astype(BF16)
    for nb, acc_ref in enumerate(acc_refs):
        cols = slice(nb * MXU_COLS, (nb + 1) * MXU_COLS)
        acc = jnp.dot(xn, w_ref[:, cols], preferred_element_type=F32)
        qkv_refs[0][0, 0, :, cols] = acc.astype(BF16)
        for c in range(per):
            acc_ref[c] = acc[:, c * LANES:(c + 1) * LANES]
        plane = tm // d_mid
        for r in range(d_mid):
            rows = [acc_ref[c, pl.ds(r, plane, stride=d_mid), :] for c in range(per)]
            qkv_refs[1][0, r, :, cols] = jnp.concatenate(rows, axis=1).astype(BF16)
            for c in range(per):
                mid_ref[c, r * plane:(r + 1) * plane, :] = rows[c]
        for r in range(d_far):
            r0, m = r % d_mid, r // d_mid
            rows = [mid_ref[c, pl.ds(r0 * plane + m, tm // d_far, stride=step), :]
                    for c in range(per)]
            qkv_refs[2][0, r, :, cols] = jnp.concatenate(rows, axis=1).astype(BF16)
    rest_ref[...] = jnp.dot(xn, w_ref[:, nq:], preferred_element_type=F32)


def _in_proj(x, g, w, layer, B, S, tm):
    T, D = x.shape
    N = w.shape[-1]
    nq = 3 * ATTN_WIDTH
    tps = S // tm
    res_spec = lambda d: pl.BlockSpec((1, d, tm // d, nq), lambda i: (i // tps, 0, i % tps, 0))
    return pl.pallas_call(
        _in_proj_kernel,
        grid=(T // tm,),
        in_specs=[pl.BlockSpec((tm, D), lambda i: (i, 0)),
                  pl.BlockSpec((1, D), lambda i: (0, 0)),
                  _layer_spec(w, layer)],
        out_specs=[res_spec(d) for d in DILATIONS]
                  + [pl.BlockSpec((tm, N - nq), lambda i: (i, 0))],
        out_shape=[jax.ShapeDtypeStruct((B, d, S // d, nq), BF16) for d in DILATIONS]
                  + [jax.ShapeDtypeStruct((T, N - nq), F32)],
        scratch_shapes=[pltpu.VMEM((MXU_COLS // LANES, tm, LANES), F32)] * (nq // MXU_COLS + 1),
        compiler_params=_cparams("parallel"),
        name="in_proj",
    )(x, g, w)


def _t5_causal_bucket(dist):
    max_exact = REL_BUCKETS // 2
    d = np.maximum(dist, 1).astype(np.float32)
    large = max_exact + (np.log(d / max_exact) / np.log(REL_MAX_DIST / max_exact)
                         * (REL_BUCKETS - max_exact)).astype(np.int32)
    large = np.minimum(large, REL_BUCKETS - 1)
    return np.where(dist < max_exact, dist, large).astype(np.int32)


def _band_bias(rel_bias, dil):
    row = np.arange(Q_BLOCK)[:, None]
    col = np.arange(2 * Q_BLOCK)[None, :]
    steps = Q_BLOCK + row - col
    valid = (steps >= 0) & (steps <= WINDOW_STEPS)
    bucket = _t5_causal_bucket(dil * np.clip(steps, 0, WINDOW_STEPS))
    onehot = (bucket[..., None] == np.arange(REL_BUCKETS)).astype(np.float32)
    b = jnp.einsum('qcb,bh->hqc', jnp.asarray(onehot), rel_bias.astype(F32),
                   precision=lax.Precision.HIGHEST)
    masks = np.stack([valid, valid & (col >= Q_BLOCK)])
    return jnp.where(jnp.asarray(masks)[:, None], b[None], NEG_INF)


def _window_attn_kernel(q_ref, kp_ref, k_ref, vp_ref, v_ref, bias_ref, o_ref, lse_ref, *,
                        nseq, nq):
    first_step = jnp.where(pl.program_id(1) == 0, 1, 0)
    lane = lax.broadcasted_iota(jnp.int32, (Q_BLOCK, LANES), 1)
    low_half = lane < ATTN_HEAD_DIM
    Q = Q_BLOCK
    for n, j in [(n, j) for n in range(nseq) for j in range(nq)]:
        rows = slice(j * Q, (j + 1) * Q)
        if j == 0:
            k2 = jnp.concatenate([kp_ref[n], k_ref[n, :Q, :]], axis=0)
            v2 = jnp.concatenate([vp_ref[n], v_ref[n, :Q, :]], axis=0)
            table = first_step
        else:
            k2 = k_ref[n, (j - 1) * Q:(j + 1) * Q, :]
            v2 = v_ref[n, (j - 1) * Q:(j + 1) * Q, :]
            table = 0
        m_all = jnp.zeros((Q, LANES), F32)
        l_all = jnp.ones((Q, LANES), F32)
        for g in range(ATTN_WIDTH // LANES):
            cols = slice(g * LANES, (g + 1) * LANES)
            qg, kg, vg = q_ref[n, rows, cols], k2[:, cols], v2[:, cols]
            og = lg = None
            for half in range(2):
                h = 2 * g + half
                keep = low_half if half == 0 else jnp.logical_not(low_half)
                qh = jnp.where(keep, qg, jnp.zeros_like(qg))
                s = lax.dot_general(qh, kg, (((1,), (1,)), ((), ())),
                                    preferred_element_type=F32)
                s = s + bias_ref[table, h]
                m = jnp.max(s, axis=-1, keepdims=True)
                p = jnp.exp2(s - m)
                l = jnp.sum(p, axis=-1, keepdims=True)
                o = jnp.dot(p.astype(BF16), vg, preferred_element_type=F32)
                og = o if half == 0 else jnp.where(low_half, og, o)
                lg = l if half == 0 else jnp.where(low_half, lg, l)
                m_all = jnp.where(lane == h, m, m_all)
                l_all = jnp.where(lane == h, l, l_all)
            o_ref[n, rows, cols] = (og * (1.0 / lg)).astype(o_ref.dtype)
        lse_ref[n, rows, :] = (m_all + jnp.log2(l_all)) * LN_2


def _window_attn(qkv, bias, blocks_per_step):
    N, L, _ = qkv.shape
    W = ATTN_WIDTH
    nq = min(blocks_per_step, L // Q_BLOCK)
    nseq = min(max(blocks_per_step // nq, 1), N)
    assert N % nseq == 0
    step = nq * Q_BLOCK
    prev = lambda i: jnp.maximum(i * nq - 1, 0)
    return pl.pallas_call(
        partial(_window_attn_kernel, nseq=nseq, nq=nq),
        grid=(N // nseq, L // step),
        in_specs=[pl.BlockSpec((nseq, step, W), lambda n, i: (n, i, 0)),
                  pl.BlockSpec((nseq, Q_BLOCK, W), lambda n, i: (n, prev(i), 1)),
                  pl.BlockSpec((nseq, step, W), lambda n, i: (n, i, 1)),
                  pl.BlockSpec((nseq, Q_BLOCK, W), lambda n, i: (n, prev(i), 2)),
                  pl.BlockSpec((nseq, step, W), lambda n, i: (n, i, 2)),
                  pl.BlockSpec(bias.shape, lambda n, i: (0, 0, 0, 0))],
        out_specs=[pl.BlockSpec((nseq, step, W), lambda n, i: (n, i, 0)),
                   pl.BlockSpec((nseq, step, LANES), lambda n, i: (n, i, 0))],
        out_shape=[jax.ShapeDtypeStruct((N, L, W), BF16),
                   jax.ShapeDtypeStruct((N, L, LANES), F32)],
        compiler_params=_cparams("parallel", "parallel"),
        name="window_attn",
    )(qkv, qkv, qkv, qkv, qkv, bias)


def _token_order(src_ref, scratch_ref, d):
    if d == 1:
        return src_ref[0, 0].astype(F32)
    nchunk, n, _ = scratch_ref.shape
    for r in range(d):
        part = src_ref[0, r].astype(F32)
        for c in range(nchunk):
            scratch_ref[c, pl.ds(r, n // d, stride=d), :] = part[:, c * LANES:(c + 1) * LANES]
    return jnp.concatenate([scratch_ref[c] for c in range(nchunk)], axis=1)


def _memory_xattn(x, g_ref, wq_ref, kv_ref, wo_ref):
    xn = _rms(x, g_ref[...]).astype(BF16)
    q = jnp.dot(xn, wq_ref[...], preferred_element_type=F32).astype(BF16)
    heads = []
    for h in range(XATTN_HEADS):
        sl = slice(h * XATTN_HEAD_DIM, (h + 1) * XATTN_HEAD_DIM)
        k = kv_ref[0, :, sl]
        v = kv_ref[0, :, XATTN_WIDTH + h * XATTN_HEAD_DIM:XATTN_WIDTH + (h + 1) * XATTN_HEAD_DIM]
        s = lax.dot_general(q[:, sl], k, (((1,), (1,)), ((), ())), preferred_element_type=F32)
        m = jnp.max(s, axis=-1, keepdims=True)
        p = jnp.exp2(s - m)
        l = jnp.sum(p, axis=-1, keepdims=True)
        heads.append(jnp.dot(p.astype(BF16), v, preferred_element_type=F32) * (1.0 / l))
    o = jnp.concatenate(heads, axis=1).astype(BF16)
    return x + jnp.dot(o, wo_ref[...], preferred_element_type=F32)


def _mixer_out_kernel(x_ref, o1_ref, o2_ref, o3_ref, l1_ref, l2_ref, l3_ref, rest_ref, halo_ref,
                      convw_ref, poolw_ref, pscale_ref, pwin_ref, expand_ref, wout_ref,
                      xg_ref, wq_ref, kv_ref, wo_ref, fg_ref, wr_ref, out_ref, *refs,
                      tm, tiles_per_seq, routed):
    os2_ref, os3_ref, ls2_ref, ls3_ref = refs[-4:]
    it = pl.program_id(0) % tiles_per_seq
    lses = [_token_order(r, s, d)
            for r, s, d in zip((l1_ref, l2_ref, l3_ref), (None, ls2_ref, ls3_ref), DILATIONS)]
    outs = [_token_order(r, s, d)
            for r, s, d in zip((o1_ref, o2_ref, o3_ref), (None, os2_ref, os3_ref), DILATIONS)]
    lm = jnp.maximum(jnp.maximum(lses[0], lses[1]), lses[2])
    es = [jnp.exp(l - lm) for l in lses]
    inv = 1.0 / (es[0] + es[1] + es[2])
    y_attn = jnp.zeros((tm, ATTN_WIDTH), F32)
    for e, o in zip(es, outs):
        w = e * inv
        w_hi = w.astype(BF16)
        w_lo = (w - w_hi.astype(F32)).astype(BF16)
        wide = jnp.dot(jnp.concatenate([w_hi, w_lo], axis=1), expand_ref[...],
                       preferred_element_type=F32)
        y_attn = y_attn + wide * o
    y_attn = y_attn.astype(BF16)

    rest = rest_ref[...]
    halo = jnp.where(it == 0, 0.0, halo_ref[...])
    ext = jnp.concatenate([halo, rest], axis=0)

    u = ext[:, :POOL_WIDTH]
    s2 = u + pltpu.roll(u, 1, 0)
    s4 = s2 + pltpu.roll(s2, 2, 0)
    s8 = s4 + pltpu.roll(s4, 4, 0)
    s16 = s8 + pltpu.roll(s8, 8, 0)
    col = lax.broadcasted_iota(jnp.int32, u.shape, 1)
    g = POOL_GROUP_DIM
    wsum = jnp.where(col < g, s2, jnp.where(col < 2 * g, s4, jnp.where(col < 3 * g, s8, s16)))
    pos = it * tm + lax.broadcasted_iota(jnp.int32, (tm, POOL_WIDTH), 0)
    count = jnp.minimum((pos + 1).astype(F32), pwin_ref[...])
    pooled = wsum[POOL_HALO:] / count - u[POOL_HALO:]
    y_pool = jnp.dot(pooled.astype(BF16), poolw_ref[...], preferred_element_type=F32)
    y_pool = (y_pool * pscale_ref[...]).astype(BF16)

    b_gate = rest[:, POOL_WIDTH:POOL_WIDTH + CONV_WIDTH]
    cu = ext[:, POOL_WIDTH + CONV_WIDTH:POOL_WIDTH + 2 * CONV_WIDTH] * ext[:, POOL_WIDTH + 2 * CONV_WIDTH:]
    y = (convw_ref[0:1, :] * cu + convw_ref[1:2, :] * pltpu.roll(cu, 1, 0)
         + convw_ref[2:3, :] * pltpu.roll(cu, 2, 0))
    y_conv = (b_gate * y[POOL_HALO:]).astype(BF16)

    a0, a1 = ATTN_WIDTH, ATTN_WIDTH + POOL_WIDTH
    acc = jnp.dot(y_attn, wout_ref[:a0, :], preferred_element_type=F32)
    acc += jnp.dot(y_pool, wout_ref[a0:a1, :], preferred_element_type=F32)
    acc += jnp.dot(y_conv, wout_ref[a1:, :], preferred_element_type=F32)
    x_new = _memory_xattn(x_ref[...] + acc, xg_ref, wq_ref, kv_ref, wo_ref)
    out_ref[...] = x_new
    if routed:
        _route_tokens(x_new, fg_ref, wr_ref, *refs[:2])


def _mixer_out(x, outs, lses, rest, conv_w, pool_w_bd, pool_scale, w_out, xg, wq, kv, wo, layer,
               ffn_g, wr, routed, S, tm):
    T, D = x.shape
    M = kv.shape[1]
    hb = tm // POOL_HALO
    tps = S // tm
    row = lambda i: (i, 0)
    const = lambda i: (0, 0)
    res_spec = lambda d, c: pl.BlockSpec((1, d, tm // d, c), lambda i: (i // tps, 0, i % tps, 0))
    pwin = jnp.asarray(np.repeat(np.array(POOL_WINDOWS, np.float32), POOL_GROUP_DIM)[None, :])
    expand = np.zeros((2, LANES, ATTN_WIDTH), np.float32)
    for h in range(ATTN_HEADS):
        expand[:, h, h * ATTN_HEAD_DIM:(h + 1) * ATTN_HEAD_DIM] = 1.0
    expand = jnp.asarray(expand.reshape(2 * LANES, ATTN_WIDTH), dtype=BF16)
    out_specs = [pl.BlockSpec((tm, D), row)]
    out_shape = [jax.ShapeDtypeStruct((T, D), F32)]
    if routed:
        out_specs += [pl.BlockSpec((tm * PACKED_ROWS, LANES), row), pl.BlockSpec((tm, LANES), row)]
        out_shape += [jax.ShapeDtypeStruct((T * PACKED_ROWS, LANES), jnp.uint32),
                      jax.ShapeDtypeStruct((T, LANES), F32)]
    res = pl.pallas_call(
        partial(_mixer_out_kernel, tm=tm, tiles_per_seq=tps, routed=routed),
        grid=(T // tm,),
        scratch_shapes=[pltpu.VMEM((ATTN_WIDTH // LANES, tm, LANES), F32)] * 2
                       + [pltpu.VMEM((1, tm, LANES), F32)] * 2,
        name="mixer_out",
        in_specs=[pl.BlockSpec((tm, D), row)]
                 + [res_spec(d, ATTN_WIDTH) for d in DILATIONS]
                 + [res_spec(d, LANES) for d in DILATIONS]
                 + [pl.BlockSpec((tm, REST_WIDTH), row),
                    pl.BlockSpec((POOL_HALO, REST_WIDTH), lambda i: (jnp.maximum(i * hb - 1, 0), 0)),
                    pl.BlockSpec(conv_w.shape, const),
                    pl.BlockSpec(pool_w_bd.shape, const),
                    pl.BlockSpec((1, POOL_WIDTH), const),
                    pl.BlockSpec((1, POOL_WIDTH), const),
                    pl.BlockSpec(expand.shape, const),
                    _layer_spec(w_out, layer),
                    pl.BlockSpec((1, D), const),
                    _layer_spec(wq, layer),
                    pl.BlockSpec((1, M, 2 * XATTN_WIDTH), lambda i: (i // tps, 0, 0)),
                    _layer_spec(wo, layer),
                    pl.BlockSpec((1, D), const),
                    pl.BlockSpec(wr.shape, const)],
        out_specs=out_specs,
        out_shape=out_shape,
        compiler_params=_cparams("parallel"),
    )(x, *outs, *lses, rest, rest, conv_w, pool_w_bd, pool_scale, pwin, expand, w_out,
      xg, wq, kv, wo, ffn_g, wr)
    return res if routed else res[0]


def _swiglu(hn, w13_ref, w2_ref):
    dff = w2_ref.shape[0]
    a = jnp.dot(hn, w13_ref[:, :dff], preferred_element_type=F32)
    b = jnp.dot(hn, w13_ref[:, dff:], preferred_element_type=F32)
    h = (a * jax.nn.sigmoid(a) * b).astype(BF16)
    return jnp.dot(h, w2_ref[...], preferred_element_type=F32)


def _dense_ffn_kernel(x_ref, g_ref, w13_ref, w2_ref, out_ref):
    x = x_ref[...]
    out_ref[...] = x + _swiglu(_rms(x, g_ref[...]).astype(BF16), w13_ref, w2_ref)


def _dense_ffn(x, g, w13, w2, layer, tm):
    T, D = x.shape
    return pl.pallas_call(
        _dense_ffn_kernel,
        grid=(T // tm,),
        in_specs=[pl.BlockSpec((tm, D), lambda i: (i, 0)),
                  pl.BlockSpec((1, D), lambda i: (0, 0)),
                  _layer_spec(w13, layer),
                  _layer_spec(w2, layer)],
        out_specs=pl.BlockSpec((tm, D), lambda i: (i, 0)),
        out_shape=jax.ShapeDtypeStruct((T, D), F32),
        compiler_params=_cparams("parallel"),
        name="dense_ffn",
    )(x, g, w13, w2)


def _store_slab(ref, v):
    n, rows = v.shape[0], v.shape[1] // LANES
    for s in range(rows):
        ref[pl.ds(s, n, stride=rows), :] = v[:, s * LANES:(s + 1) * LANES]


def _load_slab(ref, first, n, rows):
    return jnp.concatenate(
        [ref[pl.ds(first * rows + s, n, stride=rows), :] for s in range(rows)], axis=1)


def _slab(row, rows):
    return pl.ds(pl.multiple_of(row * rows, rows), rows)


def _pack_bf16_pairs(v):
    half = v.shape[1] // 2
    bits = pltpu.bitcast(v.astype(BF16).astype(F32), jnp.uint32)
    return (bits[:, half:] & jnp.uint32(0xFFFF0000)) | (bits[:, :half] >> 16)


def _unpack_bf16_pairs(u):
    low = pltpu.bitcast(u << 16, F32)
    high = pltpu.bitcast(u & jnp.uint32(0xFFFF0000), F32)
    return jnp.concatenate([low, high], axis=1).astype(BF16)


def _route_tokens(x, g_ref, wr_ref, hn_ref, route_ref):
    hn = _rms(x, g_ref[...])
    _store_slab(hn_ref, _pack_bf16_pairs(hn))
    logits = jnp.dot(hn.astype(BF16), wr_ref[...], preferred_element_type=F32)
    lane = lax.broadcasted_iota(jnp.int32, logits.shape, 1).astype(F32)
    logits = jnp.where(lane < N_EXPERTS, logits, -jnp.inf)
    m1 = jnp.max(logits, axis=-1, keepdims=True)
    i1 = jnp.min(jnp.where(logits == m1, lane, float(LANES)), axis=-1, keepdims=True)
    rest = jnp.where(lane == i1, -jnp.inf, logits)
    m2 = jnp.max(rest, axis=-1, keepdims=True)
    i2 = jnp.min(jnp.where(rest == m2, lane, float(LANES)), axis=-1, keepdims=True)
    e2 = jnp.exp(m2 - m1)
    p1 = 1.0 / (1.0 + e2)
    p2 = e2 * p1
    route_ref[...] = jnp.where(lane == 0, i1, jnp.where(lane == 1, i2,
                     jnp.where(lane == 2, p1, jnp.where(lane == 3, p2, 0.0))))


def _start_row_copy(src_ref, src_row, dst_ref, dst_row, sem, rows, queue):
    pltpu.async_copy(src_ref.at[_slab(src_row, rows)], dst_ref.at[_slab(dst_row, rows)], sem,
                     priority=queue)


def _wait_rows(src_ref, dst_ref, n, sem, rows):
    span = pl.ds(0, n * rows)
    pltpu.make_async_copy(src_ref.at[span], dst_ref.at[span], sem).wait()


def _dispatch_kernel(pos_ref, pad_ref, hn_ref, xs_ref, sem, *, tm):
    i = pl.program_id(0)
    base = i * tm * TOP_K

    def start(r, c):
        for kk in range(TOP_K):
            _start_row_copy(hn_ref, r, xs_ref, pos_ref[base + r * TOP_K + kk], sem, PACKED_ROWS,
                            kk % DMA_QUEUES)
        return c

    lax.fori_loop(0, tm, start, 0, unroll=DMA_ISSUE_UNROLL)
    _wait_rows(xs_ref, xs_ref, tm * TOP_K, sem, PACKED_ROWS)

    @pl.when(i == 0)
    def _():
        npad = pad_ref.shape[0]

        def pstart(r, c):
            for q in range(DMA_QUEUES):
                _start_row_copy(hn_ref, 0, xs_ref, pad_ref[r * DMA_QUEUES + q], sem, PACKED_ROWS, q)
            return c

        lax.fori_loop(0, npad // DMA_QUEUES, pstart, 0, unroll=DMA_ISSUE_UNROLL)
        _wait_rows(xs_ref, xs_ref, npad, sem, PACKED_ROWS)


def _dispatch(hn_slab, pos, pad_rows, n_rows, tm):
    T = hn_slab.shape[0] // PACKED_ROWS
    return pl.pallas_call(
        partial(_dispatch_kernel, tm=tm),
        grid_spec=pltpu.PrefetchScalarGridSpec(
            num_scalar_prefetch=2,
            grid=(T // tm,),
            in_specs=[pl.BlockSpec((tm * PACKED_ROWS, LANES), lambda i, *_: (i, 0))],
            out_specs=pl.BlockSpec(memory_space=pl.ANY),
            scratch_shapes=[pltpu.SemaphoreType.DMA(())]),
        out_shape=jax.ShapeDtypeStruct((n_rows * PACKED_ROWS, LANES), jnp.uint32),
        compiler_params=_cparams("arbitrary"),
        name="dispatch",
    )(pos, pad_rows, hn_slab)


def _expert_ffn_kernel(te_ref, tv_ref, xs_ref, w13_ref, w2_ref, y_ref, *, tm):
    valid = tv_ref[pl.program_id(0)] > 0

    @pl.when(valid)
    def _():
        hn = _unpack_bf16_pairs(_load_slab(xs_ref, 0, tm, PACKED_ROWS))
        _store_slab(y_ref, _swiglu(hn, w13_ref, w2_ref))

    @pl.when(jnp.logical_not(valid))
    def _():
        y_ref[...] = jnp.zeros_like(y_ref)


def _expert_ffn(xs, tile_expert, tile_valid, w13, w2, layer, tm):
    n_rows = xs.shape[0] // PACKED_ROWS
    expert_spec = lambda w, buffers: pl.BlockSpec((None, None) + w.shape[2:],
                                                  lambda i, te, tv: (layer, te[i], 0, 0),
                                                  pipeline_mode=pl.Buffered(buffers))
    return pl.pallas_call(
        partial(_expert_ffn_kernel, tm=tm),
        grid_spec=pltpu.PrefetchScalarGridSpec(
            num_scalar_prefetch=2,
            grid=(n_rows // tm,),
            in_specs=[pl.BlockSpec((tm * PACKED_ROWS, LANES), lambda i, te, tv: (i, 0)),
                      expert_spec(w13, 1), expert_spec(w2, 2)],
            out_specs=pl.BlockSpec((tm * SLAB_ROWS, LANES), lambda i, te, tv: (i, 0))),
        out_shape=jax.ShapeDtypeStruct((n_rows * SLAB_ROWS, LANES), F32),
        compiler_params=_cparams("parallel"),
        name="expert_ffn",
    )(tile_expert, tile_valid, xs, w13, w2)


def _combine_kernel(pos_ref, x_ref, route_ref, g_ref, y_ref, out_ref, buf, sem, *, tm,
                    final_norm):
    i = pl.program_id(0)

    def issue(tile, slot):
        base = tile * tm * TOP_K

        def start(r, c):
            for kk in range(TOP_K):
                _start_row_copy(y_ref, pos_ref[base + r * TOP_K + kk], buf.at[slot], kk * tm + r,
                                sem.at[slot], SLAB_ROWS, kk % DMA_QUEUES)
            return c

        lax.fori_loop(0, tm, start, 0, unroll=DMA_ISSUE_UNROLL)

    @pl.when(i == 0)
    def _():
        issue(0, 0)

    @pl.when(i + 1 < pl.num_programs(0))
    def _():
        issue(i + 1, (i + 1) % 2)

    slot = i % 2
    _wait_rows(y_ref, buf.at[slot], tm * TOP_K, sem.at[slot], SLAB_ROWS)
    route = route_ref[...]
    acc = x_ref[...]
    for kk in range(TOP_K):
        gate = route[:, TOP_K + kk:TOP_K + kk + 1]
        acc = acc + gate * _load_slab(buf.at[slot], kk * tm, tm, SLAB_ROWS)
    out_ref[...] = _rms(acc, g_ref[...]) if final_norm else acc


def _combine(x, route, y, pos, final_g, final_norm, tm):
    T, D = x.shape
    return pl.pallas_call(
        partial(_combine_kernel, tm=tm, final_norm=final_norm),
        grid_spec=pltpu.PrefetchScalarGridSpec(
            num_scalar_prefetch=1,
            grid=(T // tm,),
            in_specs=[pl.BlockSpec((tm, D), lambda i, *_: (i, 0)),
                      pl.BlockSpec((tm, LANES), lambda i, *_: (i, 0)),
                      pl.BlockSpec((1, D), lambda i, *_: (0, 0)),
                      pl.BlockSpec(memory_space=pl.ANY)],
            out_specs=pl.BlockSpec((tm, D), lambda i, *_: (i, 0)),
            scratch_shapes=[pltpu.VMEM((2, tm * TOP_K * SLAB_ROWS, LANES), F32),
                            pltpu.SemaphoreType.DMA((2,))]),
        out_shape=jax.ShapeDtypeStruct((T, D), F32),
        compiler_params=_cparams("arbitrary"),
        name="combine",
    )(pos, x, route, final_g, y)


def _routing_tables(expert_ids, tm, n_tiles):
    e_flat = expert_ids.reshape(-1)
    onehot = (e_flat[:, None] == jnp.arange(N_EXPERTS, dtype=jnp.int32)[None, :]).astype(jnp.int32)
    csum = jnp.cumsum(onehot, axis=0)
    counts = csum[-1]
    rank = jnp.sum(onehot * (csum - 1), axis=1)
    padded = ((counts + tm - 1) // tm) * tm
    ends = jnp.cumsum(padded)
    starts = ends - padded
    pos = (jnp.sum(onehot * starts[None, :], axis=1) + rank).astype(jnp.int32)

    tile_start = jnp.arange(n_tiles, dtype=jnp.int32) * tm
    tile_valid = (tile_start < ends[-1]).astype(jnp.int32)
    te = jnp.sum((tile_start[:, None] >= ends[None, :]).astype(jnp.int32), axis=1)
    last_e = jnp.sum((ends[-1] - 1 >= ends).astype(jnp.int32))
    tile_expert = jnp.where(tile_valid > 0, te, last_e).astype(jnp.int32)

    npad_e = padded - counts
    pcum = jnp.cumsum(npad_e)
    k = jnp.arange(N_EXPERTS * tm, dtype=jnp.int32)
    ek = jnp.sum((k[:, None] >= pcum[None, :]).astype(jnp.int32), axis=1)
    ohk = (ek[:, None] == jnp.arange(N_EXPERTS, dtype=jnp.int32)[None, :]).astype(jnp.int32)
    in_group = jnp.sum(ohk * (starts + counts - (pcum - npad_e))[None, :], axis=1) + k
    pad = jnp.where(ek < N_EXPERTS, in_group, ends[-1] + k - pcum[-1]).astype(jnp.int32)
    return pos, tile_expert, tile_valid, pad


def _moe_ffn(x, hn_slab, route, w13, w2, layer, final_g, final_norm, tm, tme):
    T, D = x.shape
    n_tiles = (T * TOP_K) // tme + N_EXPERTS
    expert_ids = route[:, :TOP_K].astype(jnp.int32)
    pos, tile_expert, tile_valid, pad = _routing_tables(expert_ids, tme, n_tiles)
    xs = _dispatch(hn_slab, pos, pad, n_tiles * tme, tm)
    y = _expert_ffn(xs, tile_expert, tile_valid, w13, w2, layer, tme)
    return _combine(x, route, y, pos, final_g, final_norm, min(COMBINE_ROW_TILE, tm))


def _final_norm_kernel(x_ref, g_ref, o_ref):
    o_ref[...] = _rms(x_ref[...], g_ref[...])


def _final_norm(x, g, tm):
    T, D = x.shape
    return pl.pallas_call(
        _final_norm_kernel,
        grid=(T // tm,),
        in_specs=[pl.BlockSpec((tm, D), lambda i: (i, 0)), pl.BlockSpec((1, D), lambda i: (0, 0))],
        out_specs=pl.BlockSpec((tm, D), lambda i: (i, 0)),
        out_shape=jax.ShapeDtypeStruct((T, D), F32),
        compiler_params=_cparams("parallel"),
        name="final_norm",
    )(x, g)


def _block_diag(pw):
    G, C, _ = pw.shape
    eye = jnp.eye(G, dtype=pw.dtype)
    return (eye[:, None, :, None] * pw[:, :, None, :]).reshape(G * C, G * C)


def kernel(x, mem, rel_bias, mem_norm_g, mix_norm_g, w_in, conv_w, pool_w, pool_scale, w_out,
           xattn_norm_g, wq_x, wkv_x, wo_x, ffn_norm_g, w13_dense, w2_dense, router_w, w13_moe,
           w2_moe, final_norm_g):
    B, S, D = x.shape
    depth = w_in.shape[0]
    M = mem.shape[1]
    T = B * S
    tm = min(ROW_TILE, S)
    assert D == D_MODEL and S % (max(DILATIONS) * Q_BLOCK) == 0 and S % tm == 0

    xf = x.reshape(T, D)
    row2 = lambda v: v.reshape(1, -1)
    q_scale = jnp.where(jnp.arange(w_in.shape[-1]) < ATTN_WIDTH,
                        ATTN_HEAD_DIM ** -0.5 * LOG2_E, 1.0)
    w_in_b = (w_in * q_scale).astype(BF16)
    biases = [_band_bias(rel_bias * LOG2_E, d) for d in DILATIONS]
    wr_pad = jnp.pad(router_w, ((0, 0), (0, 0), (0, LANES - N_EXPERTS))).astype(BF16)
    w_out_b, wkv_b, wo_b = (w.astype(BF16) for w in (w_out, wkv_x, wo_x))
    wq_b = (wq_x * (XATTN_HEAD_DIM ** -0.5 * LOG2_E)).astype(BF16)
    w13_dense_b, w2_dense_b = w13_dense.astype(BF16), w2_dense.astype(BF16)
    w13_moe_b, w2_moe_b = w13_moe.astype(BF16), w2_moe.astype(BF16)

    mem_f = mem.reshape(B * M, D)
    for layer in range(depth):
        *qkvs, rest = _in_proj(xf, row2(mix_norm_g[layer]), w_in_b, layer, B, S, tm)
        branch = []
        for qkv, bias, d in zip(qkvs, biases, DILATIONS):
            o, lse = _window_attn(qkv.reshape(B * d, S // d, qkv.shape[-1]), bias, ATTN_Q_BLOCKS)
            branch.append((o.reshape(B, d, S // d, -1), lse.reshape(B, d, S // d, -1)))
        kv = _norm_proj(mem_f, row2(mem_norm_g), wkv_b, layer, BF16, min(tm, B * M))
        g = row2(ffn_norm_g[layer])
        routed = layer % 2 == 1
        res = _mixer_out(xf, [o for o, _ in branch], [l for _, l in branch], rest,
                         conv_w[layer], _block_diag(pool_w[layer]).astype(BF16),
                         row2(pool_scale[layer]), w_out_b, row2(xattn_norm_g[layer]), wq_b,
                         kv.reshape(B, M, 2 * XATTN_WIDTH), wo_b, layer, g,
                         wr_pad[min(layer // 2, wr_pad.shape[0] - 1)], routed, S,
                         min(MIXER_ROW_TILE, tm) if routed else tm)
        last = layer == depth - 1
        if not routed:
            xf = _dense_ffn(res, g, w13_dense_b, w2_dense_b, layer // 2, min(FFN_ROW_TILE, tm))
            if last:
                xf = _final_norm(xf, row2(final_norm_g), tm)
        else:
            xf, hn_slab, route = res
            xf = _moe_ffn(xf, hn_slab, route, w13_moe_b, w2_moe_b, layer // 2,
                          row2(final_norm_g), last, tm, FFN_ROW_TILE)
    return xf.reshape(B, S, D)
```

```python
from functools import partial

import numpy as np
import jax
import jax.numpy as jnp
from jax import lax
from jax.experimental import pallas as pl
from jax.experimental.pallas import tpu as pltpu

F32 = jnp.float32
BF16 = jnp.bfloat16

D_MODEL = 1024
ATTN_HEADS = 8
ATTN_HEAD_DIM = 64
ATTN_WIDTH = ATTN_HEADS * ATTN_HEAD_DIM
POOL_WIDTH = 256
POOL_GROUP_DIM = 64
POOL_WINDOWS = (2, 4, 8, 16)
POOL_HALO = 16
CONV_WIDTH = 256
REST_WIDTH = POOL_WIDTH + 3 * CONV_WIDTH
DILATIONS = (1, 4, 16)
WINDOW_STEPS = 128
Q_BLOCK = 128
NEG_INF = -1e30
REL_BUCKETS = 32
REL_MAX_DIST = 128
XATTN_HEADS = 4
XATTN_HEAD_DIM = 128
XATTN_WIDTH = XATTN_HEADS * XATTN_HEAD_DIM
N_EXPERTS = 8
TOP_K = 2
EPS = 1e-6

LANES = 128
MXU_COLS = 256
SLAB_ROWS = D_MODEL // LANES
PACKED_ROWS = SLAB_ROWS // 2
ROW_TILE = 1024
FFN_ROW_TILE = 512
COMBINE_ROW_TILE = 512
MIXER_ROW_TILE = 1024
LOG2_E = 1.4426950408889634
LN_2 = 0.6931471805599453
ATTN_Q_BLOCKS = 16
DMA_ISSUE_UNROLL = 16
DMA_QUEUES = 2
VMEM_LIMIT = 60 * 1024 * 1024


def _cparams(*sem):
    return pltpu.CompilerParams(dimension_semantics=sem, vmem_limit_bytes=VMEM_LIMIT)


def _rms(x, g):
    return x * lax.rsqrt(jnp.mean(x * x, axis=-1, keepdims=True) + EPS) * g


def _norm_proj_kernel(x_ref, g_ref, w_ref, o_ref):
    xn = _rms(x_ref[...], g_ref[...]).astype(BF16)
    o_ref[...] = jnp.dot(xn, w_ref[...], preferred_element_type=F32).astype(o_ref.dtype)


def _layer_spec(w, layer):
    return pl.BlockSpec((None,) + w.shape[1:], lambda *_: (layer, 0, 0),
                        pipeline_mode=pl.Buffered(1))


def _norm_proj(x, g, w, layer, out_dtype, tm):
    T, D = x.shape
    N = w.shape[-1]
    return pl.pallas_call(
        _norm_proj_kernel,
        grid=(T // tm,),
        in_specs=[pl.BlockSpec((tm, D), lambda i: (i, 0)),
                  pl.BlockSpec((1, D), lambda i: (0, 0)),
                  _layer_spec(w, layer)],
        out_specs=pl.BlockSpec((tm, N), lambda i: (i, 0)),
        out_shape=jax.ShapeDtypeStruct((T, N), out_dtype),
        compiler_params=_cparams("parallel"),
        name="norm_proj",
    )(x, g, w)


def _in_proj_kernel(x_ref, g_ref, w_ref, *refs):
    nd = len(DILATIONS)
    qkv_refs, rest_ref, acc_refs, mid_ref = refs[:nd], refs[nd], refs[nd + 1:-1], refs[-1]
    per, tm, _ = acc_refs[0].shape
    nq = len(acc_refs) * MXU_COLS
    d_mid, d_far = DILATIONS[1], DILATIONS[2]
    step = d_far // d_mid
    xn = _rms(x_ref[...], g_ref[...]).astype(BF16)
    for nb, acc_ref in enumerate(acc_refs):
        cols = slice(nb * MXU_COLS, (nb + 1) * MXU_COLS)
        acc = jnp.dot(xn, w_ref[:, cols], preferred_element_type=F32)
        qkv_refs[0][0, 0, :, cols] = acc.astype(BF16)
        for c in range(per):
            acc_ref[c] = acc[:, c * LANES:(c + 1) * LANES]
        plane = tm // d_mid
        for r in range(d_mid):
            rows = [acc_ref[c, pl.ds(r, plane, stride=d_mid), :] for c in range(per)]
            qkv_refs[1][0, r, :, cols] = jnp.concatenate(rows, axis=1).astype(BF16)
            for c in range(per):
                mid_ref[c, r * plane:(r + 1) * plane, :] = rows[c]
        for r in range(d_far):
            r0, m = r % d_mid, r // d_mid
            rows = [mid_ref[c, pl.ds(r0 * plane + m, tm // d_far, stride=step), :]
                    for c in range(per)]
            qkv_refs[2][0, r, :, cols] = jnp.concatenate(rows, axis=1).astype(BF16)
    rest_ref[...] = jnp.dot(xn, w_ref[:, nq:], preferred_element_type=F32)


def _in_proj(x, g, w, layer, B, S, tm):
    T, D = x.shape
    N = w.shape[-1]
    nq = 3 * ATTN_WIDTH
    tps = S // tm
    res_spec = lambda d: pl.BlockSpec((1, d, tm // d, nq), lambda i: (i // tps, 0, i % tps, 0))
    return pl.pallas_call(
        _in_proj_kernel,
        grid=(T // tm,),
        in_specs=[pl.BlockSpec((tm, D), lambda i: (i, 0)),
                  pl.BlockSpec((1, D), lambda i: (0, 0)),
                  _layer_spec(w, layer)],
        out_specs=[res_spec(d) for d in DILATIONS]
                  + [pl.BlockSpec((tm, N - nq), lambda i: (i, 0))],
        out_shape=[jax.ShapeDtypeStruct((B, d, S // d, nq), BF16) for d in DILATIONS]
                  + [jax.ShapeDtypeStruct((T, N - nq), F32)],
        scratch_shapes=[pltpu.VMEM((MXU_COLS // LANES, tm, LANES), F32)] * (nq // MXU_COLS + 1),
        compiler_params=_cparams("parallel"),
        name="in_proj",
    )(x, g, w)


def _t5_causal_bucket(dist):
    max_exact = REL_BUCKETS // 2
    d = np.maximum(dist, 1).astype(np.float32)
    large = max_exact + (np.log(d / max_exact) / np.log(REL_MAX_DIST / max_exact)
                         * (REL_BUCKETS - max_exact)).astype(np.int32)
    large = np.minimum(large, REL_BUCKETS - 1)
    return np.where(dist < max_exact, dist, large).astype(np.int32)


def _band_bias(rel_bias, dil):
    row = np.arange(Q_BLOCK)[:, None]
    col = np.arange(2 * Q_BLOCK)[None, :]
    steps = Q_BLOCK + row - col
    valid = (steps >= 0) & (steps <= WINDOW_STEPS)
    bucket = _t5_causal_bucket(dil * np.clip(steps, 0, WINDOW_STEPS))
    onehot = (bucket[..., None] == np.arange(REL_BUCKETS)).astype(np.float32)
    b = jnp.einsum('qcb,bh->hqc', jnp.asarray(onehot), rel_bias.astype(F32),
                   precision=lax.Precision.HIGHEST)
    masks = np.stack([valid, valid & (col >= Q_BLOCK)])
    return jnp.where(jnp.asarray(masks)[:, None], b[None], NEG_INF)


def _window_attn_kernel(q_ref, kp_ref, k_ref, vp_ref, v_ref, bias_ref, o_ref, lse_ref, *,
                        nseq, nq):
    first_step = jnp.where(pl.program_id(1) == 0, 1, 0)
    lane = lax.broadcasted_iota(jnp.int32, (Q_BLOCK, LANES), 1)
    low_half = lane < ATTN_HEAD_DIM
    Q = Q_BLOCK
    for n, j in [(n, j) for n in range(nseq) for j in range(nq)]:
        rows = slice(j * Q, (j + 1) * Q)
        if j == 0:
            k2 = jnp.concatenate([kp_ref[n], k_ref[n, :Q, :]], axis=0)
            v2 = jnp.concatenate([vp_ref[n], v_ref[n, :Q, :]], axis=0)
            table = first_step
        else:
            k2 = k_ref[n, (j - 1) * Q:(j + 1) * Q, :]
            v2 = v_ref[n, (j - 1) * Q:(j + 1) * Q, :]
            table = 0
        m_all = jnp.zeros((Q, LANES), F32)
        l_all = jnp.ones((Q, LANES), F32)
        for g in range(ATTN_WIDTH // LANES):
            cols = slice(g * LANES, (g + 1) * LANES)
            qg, kg, vg = q_ref[n, rows, cols], k2[:, cols], v2[:, cols]
            og = lg = None
            for half in range(2):
                h = 2 * g + half
                keep = low_half if half == 0 else jnp.logical_not(low_half)
                qh = jnp.where(keep, qg, jnp.zeros_like(qg))
                s = lax.dot_general(qh, kg, (((1,), (1,)), ((), ())),
                                    preferred_element_type=F32)
                s = s + bias_ref[table, h]
                m = jnp.max(s, axis=-1, keepdims=True)
                p = jnp.exp2(s - m)
                l = jnp.sum(p, axis=-1, keepdims=True)
                o = jnp.dot(p.astype(BF16), vg, preferred_element_type=F32)
                og = o if half == 0 else jnp.where(low_half, og, o)
                lg = l if half == 0 else jnp.where(low_half, lg, l)
                m_all = jnp.where(lane == h, m, m_all)
                l_all = jnp.where(lane == h, l, l_all)
            o_ref[n, rows, cols] = (og * (1.0 / lg)).astype(o_ref.dtype)
        lse_ref[n, rows, :] = (m_all + jnp.log2(l_all)) * LN_2


def _window_attn(qkv, bias, blocks_per_step):
    N, L, _ = qkv.shape
    W = ATTN_WIDTH
    nq = min(blocks_per_step, L // Q_BLOCK)
    nseq = min(max(blocks_per_step // nq, 1), N)
    assert N % nseq == 0
    step = nq * Q_BLOCK
    prev = lambda i: jnp.maximum(i * nq - 1, 0)
    return pl.pallas_call(
        partial(_window_attn_kernel, nseq=nseq, nq=nq),
        grid=(N // nseq, L // step),
        in_specs=[pl.BlockSpec((nseq, step, W), lambda n, i: (n, i, 0)),
                  pl.BlockSpec((nseq, Q_BLOCK, W), lambda n, i: (n, prev(i), 1)),
                  pl.BlockSpec((nseq, step, W), lambda n, i: (n, i, 1)),
                  pl.BlockSpec((nseq, Q_BLOCK, W), lambda n, i: (n, prev(i), 2)),
                  pl.BlockSpec((nseq, step, W), lambda n, i: (n, i, 2)),
                  pl.BlockSpec(bias.shape, lambda n, i: (0, 0, 0, 0))],
        out_specs=[pl.BlockSpec((nseq, step, W), lambda n, i: (n, i, 0)),
                   pl.BlockSpec((nseq, step, LANES), lambda n, i: (n, i, 0))],
        out_shape=[jax.ShapeDtypeStruct((N, L, W), BF16),
                   jax.ShapeDtypeStruct((N, L, LANES), F32)],
        compiler_params=_cparams("parallel", "parallel"),
        name="window_attn",
    )(qkv, qkv, qkv, qkv, qkv, bias)


def _token_order(src_ref, scratch_ref, d):
    if d == 1:
        return src_ref[0, 0].astype(F32)
    nchunk, n, _ = scratch_ref.shape
    for r in range(d):
        part = src_ref[0, r].astype(F32)
        for c in range(nchunk):
            scratch_ref[c, pl.ds(r, n // d, stride=d), :] = part[:, c * LANES:(c + 1) * LANES]
    return jnp.concatenate([scratch_ref[c] for c in range(nchunk)], axis=1)


def _memory_xattn(x, g_ref, wq_ref, kv_ref, wo_ref):
    xn = _rms(x, g_ref[...]).astype(BF16)
    q = jnp.dot(xn, wq_ref[...], preferred_element_type=F32).astype(BF16)
    heads = []
    for h in range(XATTN_HEADS):
        sl = slice(h * XATTN_HEAD_DIM, (h + 1) * XATTN_HEAD_DIM)
        k = kv_ref[0, :, sl]
        v = kv_ref[0, :, XATTN_WIDTH + h * XATTN_HEAD_DIM:XATTN_WIDTH + (h + 1) * XATTN_HEAD_DIM]
        s = lax.dot_general(q[:, sl], k, (((1,), (1,)), ((), ())), preferred_element_type=F32)
        m = jnp.max(s, axis=-1, keepdims=True)
        p = jnp.exp2(s - m)
        l = jnp.sum(p, axis=-1, keepdims=True)
        heads.append(jnp.dot(p.astype(BF16), v, preferred_element_type=F32) * (1.0 / l))
    o = jnp.concatenate(heads, axis=1).astype(BF16)
    return x + jnp.dot(o, wo_ref[...], preferred_element_type=F32)


def _mixer_out_kernel(x_ref, o1_ref, o2_ref, o3_ref, l1_ref, l2_ref, l3_ref, rest_ref, halo_ref,
                      convw_ref, poolw_ref, pscale_ref, pwin_ref, expand_ref, wout_ref,
                      xg_ref, wq_ref, kv_ref, wo_ref, fg_ref, wr_ref, out_ref, *refs,
                      tm, tiles_per_seq, routed):
    os2_ref, os3_ref, ls2_ref, ls3_ref = refs[-4:]
    it = pl.program_id(0) % tiles_per_seq
    lses = [_token_order(r, s, d)
            for r, s, d in zip((l1_ref, l2_ref, l3_ref), (None, ls2_ref, ls3_ref), DILATIONS)]
    outs = [_token_order(r, s, d)
            for r, s, d in zip((o1_ref, o2_ref, o3_ref), (None, os2_ref, os3_ref), DILATIONS)]
    lm = jnp.maximum(jnp.maximum(lses[0], lses[1]), lses[2])
    es = [jnp.exp(l - lm) for l in lses]
    inv = 1.0 / (es[0] + es[1] + es[2])
    y_attn = jnp.zeros((tm, ATTN_WIDTH), F32)
    for e, o in zip(es, outs):
        w = e * inv
        w_hi = w.astype(BF16)
        w_lo = (w - w_hi.astype(F32)).astype(BF16)
        wide = jnp.dot(jnp.concatenate([w_hi, w_lo], axis=1), expand_ref[...],
                       preferred_element_type=F32)
        y_attn = y_attn + wide * o
    y_attn = y_attn.astype(BF16)

    rest = rest_ref[...]
    halo = jnp.where(it == 0, 0.0, halo_ref[...])
    ext = jnp.concatenate([halo, rest], axis=0)

    u = ext[:, :POOL_WIDTH]
    s2 = u + pltpu.roll(u, 1, 0)
    s4 = s2 + pltpu.roll(s2, 2, 0)
    s8 = s4 + pltpu.roll(s4, 4, 0)
    s16 = s8 + pltpu.roll(s8, 8, 0)
    col = lax.broadcasted_iota(jnp.int32, u.shape, 1)
    g = POOL_GROUP_DIM
    wsum = jnp.where(col < g, s2, jnp.where(col < 2 * g, s4, jnp.where(col < 3 * g, s8, s16)))
    pos = it * tm + lax.broadcasted_iota(jnp.int32, (tm, POOL_WIDTH), 0)
    count = jnp.minimum((pos + 1).astype(F32), pwin_ref[...])
    pooled = wsum[POOL_HALO:] / count - u[POOL_HALO:]
    y_pool = jnp.dot(pooled.astype(BF16), poolw_ref[...], preferred_element_type=F32)
    y_pool = (y_pool * pscale_ref[...]).astype(BF16)

    b_gate = rest[:, POOL_WIDTH:POOL_WIDTH + CONV_WIDTH]
    cu = ext[:, POOL_WIDTH + CONV_WIDTH:POOL_WIDTH + 2 * CONV_WIDTH] * ext[:, POOL_WIDTH + 2 * CONV_WIDTH:]
    y = (convw_ref[0:1, :] * cu + convw_ref[1:2, :] * pltpu.roll(cu, 1, 0)
         + convw_ref[2:3, :] * pltpu.roll(cu, 2, 0))
    y_conv = (b_gate * y[POOL_HALO:]).astype(BF16)

    a0, a1 = ATTN_WIDTH, ATTN_WIDTH + POOL_WIDTH
    acc = jnp.dot(y_attn, wout_ref[:a0, :], preferred_element_type=F32)
    acc += jnp.dot(y_pool, wout_ref[a0:a1, :], preferred_element_type=F32)
    acc += jnp.dot(y_conv, wout_ref[a1:, :], preferred_element_type=F32)
    x_new = _memory_xattn(x_ref[...] + acc, xg_ref, wq_ref, kv_ref, wo_ref)
    out_ref[...] = x_new
    if routed:
        _route_tokens(x_new, fg_ref, wr_ref, *refs[:2])


def _mixer_out(x, outs, lses, rest, conv_w, pool_w_bd, pool_scale, w_out, xg, wq, kv, wo, layer,
               ffn_g, wr, routed, S, tm):
    T, D = x.shape
    M = kv.shape[1]
    hb = tm // POOL_HALO
    tps = S // tm
    row = lambda i: (i, 0)
    const = lambda i: (0, 0)
    res_spec = lambda d, c: pl.BlockSpec((1, d, tm // d, c), lambda i: (i // tps, 0, i % tps, 0))
    pwin = jnp.asarray(np.repeat(np.array(POOL_WINDOWS, np.float32), POOL_GROUP_DIM)[None, :])
    expand = np.zeros((2, LANES, ATTN_WIDTH), np.float32)
    for h in range(ATTN_HEADS):
        expand[:, h, h * ATTN_HEAD_DIM:(h + 1) * ATTN_HEAD_DIM] = 1.0
    expand = jnp.asarray(expand.reshape(2 * LANES, ATTN_WIDTH), dtype=BF16)
    out_specs = [pl.BlockSpec((tm, D), row)]
    out_shape = [jax.ShapeDtypeStruct((T, D), F32)]
    if routed:
        out_specs += [pl.BlockSpec((tm * PACKED_ROWS, LANES), row), pl.BlockSpec((tm, LANES), row)]
        out_shape += [jax.ShapeDtypeStruct((T * PACKED_ROWS, LANES), jnp.uint32),
                      jax.ShapeDtypeStruct((T, LANES), F32)]
    res = pl.pallas_call(
        partial(_mixer_out_kernel, tm=tm, tiles_per_seq=tps, routed=routed),
        grid=(T // tm,),
        scratch_shapes=[pltpu.VMEM((ATTN_WIDTH // LANES, tm, LANES), F32)] * 2
                       + [pltpu.VMEM((1, tm, LANES), F32)] * 2,
        name="mixer_out",
        in_specs=[pl.BlockSpec((tm, D), row)]
                 + [res_spec(d, ATTN_WIDTH) for d in DILATIONS]
                 + [res_spec(d, LANES) for d in DILATIONS]
                 + [pl.BlockSpec((tm, REST_WIDTH), row),
                    pl.BlockSpec((POOL_HALO, REST_WIDTH), lambda i: (jnp.maximum(i * hb - 1, 0), 0)),
                    pl.BlockSpec(conv_w.shape, const),
                    pl.BlockSpec(pool_w_bd.shape, const),
                    pl.BlockSpec((1, POOL_WIDTH), const),
                    pl.BlockSpec((1, POOL_WIDTH), const),
                    pl.BlockSpec(expand.shape, const),
                    _layer_spec(w_out, layer),
                    pl.BlockSpec((1, D), const),
                    _layer_spec(wq, layer),
                    pl.BlockSpec((1, M, 2 * XATTN_WIDTH), lambda i: (i // tps, 0, 0)),
                    _layer_spec(wo, layer),
                    pl.BlockSpec((1, D), const),
                    pl.BlockSpec(wr.shape, const)],
        out_specs=out_specs,
        out_shape=out_shape,
        compiler_params=_cparams("parallel"),
    )(x, *outs, *lses, rest, rest, conv_w, pool_w_bd, pool_scale, pwin, expand, w_out,
      xg, wq, kv, wo, ffn_g, wr)
    return res if routed else res[0]


def _swiglu(hn, w13_ref, w2_ref):
    dff = w2_ref.shape[0]
    a = jnp.dot(hn, w13_ref[:, :dff], preferred_element_type=F32)
    b = jnp.dot(hn, w13_ref[:, dff:], preferred_element_type=F32)
    h = (a * jax.nn.sigmoid(a) * b).astype(BF16)
    return jnp.dot(h, w2_ref[...], preferred_element_type=F32)


def _dense_ffn_kernel(x_ref, g_ref, w13_ref, w2_ref, out_ref):
    x = x_ref[...]
    out_ref[...] = x + _swiglu(_rms(x, g_ref[...]).astype(BF16), w13_ref, w2_ref)


def _dense_ffn(x, g, w13, w2, layer, tm):
    T, D = x.shape
    return pl.pallas_call(
        _dense_ffn_kernel,
        grid=(T // tm,),
        in_specs=[pl.BlockSpec((tm, D), lambda i: (i, 0)),
                  pl.BlockSpec((1, D), lambda i: (0, 0)),
                  _layer_spec(w13, layer),
                  _layer_spec(w2, layer)],
        out_specs=pl.BlockSpec((tm, D), lambda i: (i, 0)),
        out_shape=jax.ShapeDtypeStruct((T, D), F32),
        compiler_params=_cparams("parallel"),
        name="dense_ffn",
    )(x, g, w13, w2)


def _store_slab(ref, v):
    n, rows = v.shape[0], v.shape[1] // LANES
    for s in range(rows):
        ref[pl.ds(s, n, stride=rows), :] = v[:, s * LANES:(s + 1) * LANES]


def _load_slab(ref, first, n, rows):
    return jnp.concatenate(
        [ref[pl.ds(first * rows + s, n, stride=rows), :] for s in range(rows)], axis=1)


def _slab(row, rows):
    return pl.ds(pl.multiple_of(row * rows, rows), rows)


def _pack_bf16_pairs(v):
    half = v.shape[1] // 2
    bits = pltpu.bitcast(v.astype(BF16).astype(F32), jnp.uint32)
    return (bits[:, half:] & jnp.uint32(0xFFFF0000)) | (bits[:, :half] >> 16)


def _unpack_bf16_pairs(u):
    low = pltpu.bitcast(u << 16, F32)
    high = pltpu.bitcast(u & jnp.uint32(0xFFFF0000), F32)
    return jnp.concatenate([low, high], axis=1).astype(BF16)


def _route_tokens(x, g_ref, wr_ref, hn_ref, route_ref):
    hn = _rms(x, g_ref[...])
    _store_slab(hn_ref, _pack_bf16_pairs(hn))
    logits = jnp.dot(hn.astype(BF16), wr_ref[...], preferred_element_type=F32)
    lane = lax.broadcasted_iota(jnp.int32, logits.shape, 1).astype(F32)
    logits = jnp.where(lane < N_EXPERTS, logits, -jnp.inf)
    m1 = jnp.max(logits, axis=-1, keepdims=True)
    i1 = jnp.min(jnp.where(logits == m1, lane, float(LANES)), axis=-1, keepdims=True)
    rest = jnp.where(lane == i1, -jnp.inf, logits)
    m2 = jnp.max(rest, axis=-1, keepdims=True)
    i2 = jnp.min(jnp.where(rest == m2, lane, float(LANES)), axis=-1, keepdims=True)
    e2 = jnp.exp(m2 - m1)
    p1 = 1.0 / (1.0 + e2)
    p2 = e2 * p1
    route_ref[...] = jnp.where(lane == 0, i1, jnp.where(lane == 1, i2,
                     jnp.where(lane == 2, p1, jnp.where(lane == 3, p2, 0.0))))


def _start_row_copy(src_ref, src_row, dst_ref, dst_row, sem, rows, queue):
    pltpu.async_copy(src_ref.at[_slab(src_row, rows)], dst_ref.at[_slab(dst_row, rows)], sem,
                     priority=queue)


def _wait_rows(src_ref, dst_ref, n, sem, rows):
    span = pl.ds(0, n * rows)
    pltpu.make_async_copy(src_ref.at[span], dst_ref.at[span], sem).wait()


def _dispatch_kernel(pos_ref, pad_ref, hn_ref, xs_ref, sem, *, tm):
    i = pl.program_id(0)
    base = i * tm * TOP_K

    def start(r, c):
        for kk in range(TOP_K):
            _start_row_copy(hn_ref, r, xs_ref, pos_ref[base + r * TOP_K + kk], sem, PACKED_ROWS,
                            kk % DMA_QUEUES)
        return c

    lax.fori_loop(0, tm, start, 0, unroll=DMA_ISSUE_UNROLL)
    _wait_rows(xs_ref, xs_ref, tm * TOP_K, sem, PACKED_ROWS)

    @pl.when(i == 0)
    def _():
        npad = pad_ref.shape[0]

        def pstart(r, c):
            for q in range(DMA_QUEUES):
                _start_row_copy(hn_ref, 0, xs_ref, pad_ref[r * DMA_QUEUES + q], sem, PACKED_ROWS, q)
            return c

        lax.fori_loop(0, npad // DMA_QUEUES, pstart, 0, unroll=DMA_ISSUE_UNROLL)
        _wait_rows(xs_ref, xs_ref, npad, sem, PACKED_ROWS)


def _dispatch(hn_slab, pos, pad_rows, n_rows, tm):
    T = hn_slab.shape[0] // PACKED_ROWS
    return pl.pallas_call(
        partial(_dispatch_kernel, tm=tm),
        grid_spec=pltpu.PrefetchScalarGridSpec(
            num_scalar_prefetch=2,
            grid=(T // tm,),
            in_specs=[pl.BlockSpec((tm * PACKED_ROWS, LANES), lambda i, *_: (i, 0))],
            out_specs=pl.BlockSpec(memory_space=pl.ANY),
            scratch_shapes=[pltpu.SemaphoreType.DMA(())]),
        out_shape=jax.ShapeDtypeStruct((n_rows * PACKED_ROWS, LANES), jnp.uint32),
        compiler_params=_cparams("arbitrary"),
        name="dispatch",
    )(pos, pad_rows, hn_slab)


def _expert_ffn_kernel(te_ref, tv_ref, xs_ref, w13_ref, w2_ref, y_ref, *, tm):
    valid = tv_ref[pl.program_id(0)] > 0

    @pl.when(valid)
    def _():
        hn = _unpack_bf16_pairs(_load_slab(xs_ref, 0, tm, PACKED_ROWS))
        _store_slab(y_ref, _swiglu(hn, w13_ref, w2_ref))

    @pl.when(jnp.logical_not(valid))
    def _():
        y_ref[...] = jnp.zeros_like(y_ref)


def _expert_ffn(xs, tile_expert, tile_valid, w13, w2, layer, tm):
    n_rows = xs.shape[0] // PACKED_ROWS
    expert_spec = lambda w, buffers: pl.BlockSpec((None, None) + w.shape[2:],
                                                  lambda i, te, tv: (layer, te[i], 0, 0),
                                                  pipeline_mode=pl.Buffered(buffers))
    return pl.pallas_call(
        partial(_expert_ffn_kernel, tm=tm),
        grid_spec=pltpu.PrefetchScalarGridSpec(
            num_scalar_prefetch=2,
            grid=(n_rows // tm,),
            in_specs=[pl.BlockSpec((tm * PACKED_ROWS, LANES), lambda i, te, tv: (i, 0)),
                      expert_spec(w13, 1), expert_spec(w2, 2)],
            out_specs=pl.BlockSpec((tm * SLAB_ROWS, LANES), lambda i, te, tv: (i, 0))),
        out_shape=jax.ShapeDtypeStruct((n_rows * SLAB_ROWS, LANES), F32),
        compiler_params=_cparams("parallel"),
        name="expert_ffn",
    )(tile_expert, tile_valid, xs, w13, w2)


def _combine_kernel(pos_ref, x_ref, route_ref, g_ref, y_ref, out_ref, buf, sem, *, tm,
                    final_norm):
    i = pl.program_id(0)

    def issue(tile, slot):
        base = tile * tm * TOP_K

        def start(r, c):
            for kk in range(TOP_K):
                _start_row_copy(y_ref, pos_ref[base + r * TOP_K + kk], buf.at[slot], kk * tm + r,
                                sem.at[slot], SLAB_ROWS, kk % DMA_QUEUES)
            return c

        lax.fori_loop(0, tm, start, 0, unroll=DMA_ISSUE_UNROLL)

    @pl.when(i == 0)
    def _():
        issue(0, 0)

    @pl.when(i + 1 < pl.num_programs(0))
    def _():
        issue(i + 1, (i + 1) % 2)

    slot = i % 2
    _wait_rows(y_ref, buf.at[slot], tm * TOP_K, sem.at[slot], SLAB_ROWS)
    route = route_ref[...]
    acc = x_ref[...]
    for kk in range(TOP_K):
        gate = route[:, TOP_K + kk:TOP_K + kk + 1]
        acc = acc + gate * _load_slab(buf.at[slot], kk * tm, tm, SLAB_ROWS)
    out_ref[...] = _rms(acc, g_ref[...]) if final_norm else acc


def _combine(x, route, y, pos, final_g, final_norm, tm):
    T, D = x.shape
    return pl.pallas_call(
        partial(_combine_kernel, tm=tm, final_norm=final_norm),
        grid_spec=pltpu.PrefetchScalarGridSpec(
            num_scalar_prefetch=1,
            grid=(T // tm,),
            in_specs=[pl.BlockSpec((tm, D), lambda i, *_: (i, 0)),
                      pl.BlockSpec((tm, LANES), lambda i, *_: (i, 0)),
                      pl.BlockSpec((1, D), lambda i, *_: (0, 0)),
                      pl.BlockSpec(memory_space=pl.ANY)],
            out_specs=pl.BlockSpec((tm, D), lambda i, *_: (i, 0)),
            scratch_shapes=[pltpu.VMEM((2, tm * TOP_K * SLAB_ROWS, LANES), F32),
                            pltpu.SemaphoreType.DMA((2,))]),
        out_shape=jax.ShapeDtypeStruct((T, D), F32),
        compiler_params=_cparams("arbitrary"),
        name="combine",
    )(pos, x, route, final_g, y)


def _routing_tables(expert_ids, tm, n_tiles):
    e_flat = expert_ids.reshape(-1)
    onehot = (e_flat[:, None] == jnp.arange(N_EXPERTS, dtype=jnp.int32)[None, :]).astype(jnp.int32)
    csum = jnp.cumsum(onehot, axis=0)
    counts = csum[-1]
    rank = jnp.sum(onehot * (csum - 1), axis=1)
    padded = ((counts + tm - 1) // tm) * tm
    ends = jnp.cumsum(padded)
    starts = ends - padded
    pos = (jnp.sum(onehot * starts[None, :], axis=1) + rank).astype(jnp.int32)

    tile_start = jnp.arange(n_tiles, dtype=jnp.int32) * tm
    tile_valid = (tile_start < ends[-1]).astype(jnp.int32)
    te = jnp.sum((tile_start[:, None] >= ends[None, :]).astype(jnp.int32), axis=1)
    last_e = jnp.sum((ends[-1] - 1 >= ends).astype(jnp.int32))
    tile_expert = jnp.where(tile_valid > 0, te, last_e).astype(jnp.int32)

    npad_e = padded - counts
    pcum = jnp.cumsum(npad_e)
    k = jnp.arange(N_EXPERTS * tm, dtype=jnp.int32)
    ek = jnp.sum((k[:, None] >= pcum[None, :]).astype(jnp.int32), axis=1)
    ohk = (ek[:, None] == jnp.arange(N_EXPERTS, dtype=jnp.int32)[None, :]).astype(jnp.int32)
    in_group = jnp.sum(ohk * (starts + counts - (pcum - npad_e))[None, :], axis=1) + k
    pad = jnp.where(ek < N_EXPERTS, in_group, ends[-1] + k - pcum[-1]).astype(jnp.int32)
    return pos, tile_expert, tile_valid, pad


def _moe_ffn(x, hn_slab, route, w13, w2, layer, final_g, final_norm, tm, tme):
    T, D = x.shape
    n_tiles = (T * TOP_K) // tme + N_EXPERTS
    expert_ids = route[:, :TOP_K].astype(jnp.int32)
    pos, tile_expert, tile_valid, pad = _routing_tables(expert_ids, tme, n_tiles)
    xs = _dispatch(hn_slab, pos, pad, n_tiles * tme, tm)
    y = _expert_ffn(xs, tile_expert, tile_valid, w13, w2, layer, tme)
    return _combine(x, route, y, pos, final_g, final_norm, min(COMBINE_ROW_TILE, tm))


def _final_norm_kernel(x_ref, g_ref, o_ref):
    o_ref[...] = _rms(x_ref[...], g_ref[...])


def _final_norm(x, g, tm):
    T, D = x.shape
    return pl.pallas_call(
        _final_norm_kernel,
        grid=(T // tm,),
        in_specs=[pl.BlockSpec((tm, D), lambda i: (i, 0)), pl.BlockSpec((1, D), lambda i: (0, 0))],
        out_specs=pl.BlockSpec((tm, D), lambda i: (i, 0)),
        out_shape=jax.ShapeDtypeStruct((T, D), F32),
        compiler_params=_cparams("parallel"),
        name="final_norm",
    )(x, g)


def _block_diag(pw):
    G, C, _ = pw.shape
    eye = jnp.eye(G, dtype=pw.dtype)
    return (eye[:, None, :, None] * pw[:, :, None, :]).reshape(G * C, G * C)


def kernel(x, mem, rel_bias, mem_norm_g, mix_norm_g, w_in, conv_w, pool_w, pool_scale, w_out,
           xattn_norm_g, wq_x, wkv_x, wo_x, ffn_norm_g, w13_dense, w2_dense, router_w, w13_moe,
           w2_moe, final_norm_g):
    B, S, D = x.shape
    depth = w_in.shape[0]
    M = mem.shape[1]
    T = B * S
    tm = min(ROW_TILE, S)
    assert D == D_MODEL and S % (max(DILATIONS) * Q_BLOCK) == 0 and S % tm == 0

    xf = x.reshape(T, D)
    row2 = lambda v: v.reshape(1, -1)
    q_scale = jnp.where(jnp.arange(w_in.shape[-1]) < ATTN_WIDTH,
                        ATTN_HEAD_DIM ** -0.5 * LOG2_E, 1.0)
    w_in_b = (w_in * q_scale).astype(BF16)
    biases = [_band_bias(rel_bias * LOG2_E, d) for d in DILATIONS]
    wr_pad = jnp.pad(router_w, ((0, 0), (0, 0), (0, LANES - N_EXPERTS))).astype(BF16)
    w_out_b, wkv_b, wo_b = (w.astype(BF16) for w in (w_out, wkv_x, wo_x))
    wq_b = (wq_x * (XATTN_HEAD_DIM ** -0.5 * LOG2_E)).astype(BF16)
    w13_dense_b, w2_dense_b = w13_dense.astype(BF16), w2_dense.astype(BF16)
    w13_moe_b, w2_moe_b = w13_moe.astype(BF16), w2_moe.astype(BF16)

    mem_f = mem.reshape(B * M, D)
    for layer in range(depth):
        *qkvs, rest = _in_proj(xf, row2(mix_norm_g[layer]), w_in_b, layer, B, S, tm)
        branch = []
        for qkv, bias, d in zip(qkvs, biases, DILATIONS):
            o, lse = _window_attn(qkv.reshape(B * d, S // d, qkv.shape[-1]), bias, ATTN_Q_BLOCKS)
            branch.append((o.reshape(B, d, S // d, -1), lse.reshape(B, d, S // d, -1)))
        kv = _norm_proj(mem_f, row2(mem_norm_g), wkv_b, layer, BF16, min(tm, B * M))
        g = row2(ffn_norm_g[layer])
        routed = layer % 2 == 1
        res = _mixer_out(xf, [o for o, _ in branch], [l for _, l in branch], rest,
                         conv_w[layer], _block_diag(pool_w[layer]).astype(BF16),
                         row2(pool_scale[layer]), w_out_b, row2(xattn_norm_g[layer]), wq_b,
                         kv.reshape(B, M, 2 * XATTN_WIDTH), wo_b, layer, g,
                         wr_pad[min(layer // 2, wr_pad.shape[0] - 1)], routed, S,
                         min(MIXER_ROW_TILE, tm) if routed else tm)
        last = layer == depth - 1
        if not routed:
            xf = _dense_ffn(res, g, w13_dense_b, w2_dense_b, layer // 2, min(FFN_ROW_TILE, tm))
            if last:
                xf = _final_norm(xf, row2(final_norm_g), tm)
        else:
            xf, hn_slab, route = res
            xf = _moe_ffn(xf, hn_slab, route, w13_moe_b, w2_moe_b, layer // 2,
                          row2(final_norm_g), last, tm, FFN_ROW_TILE)
    return xf.reshape(B, S, D)
```
